```python
import jax, jax.numpy as jnp
from jax import lax
import numpy as np

D_MODEL = 1024
BATCH = 32
SEQ = 256
DEPTH = 2
DEC_BATCH = 4
DEC_SEQ = 1024
PAST_LEN = 256

GRID_W = 64
HEAD_DIM = 64
BRANCH_W = 256
N_BRANCH = 4
CHUNK = 128
GMLP_GROUPS = 4
CONV_W = 31
RWKV_HEADS = 4
N_DIR = 2
DECAY_LORA = 64
ICL_LORA = 64
GATE_LORA = 128
ATT_HEADS = 4
KV_HEADS = 2
Q_PER_KV = ATT_HEADS // KV_HEADS
WINDOW = 128
BLOCK = 128
ROPE_BASE = 10000.0
ATT_SCALE = HEAD_DIM ** -0.5
NEG_INF = -1e30
N_GROUPS = 4
EXPERTS_PER_GROUP = 4
N_EXPERTS = N_GROUPS * EXPERTS_PER_GROUP
TOP_K = 2
EXPERT_FF = 256
N_MOD = 6
RMS_EPS = 1e-6
LN_EPS = 1e-5
GN_EPS = 64e-5
SPLIT_SIZES = (BRANCH_W, BRANCH_W, BRANCH_W, BRANCH_W, BRANCH_W, BRANCH_W, BRANCH_W, N_DIR * DECAY_LORA, N_DIR * ICL_LORA, GATE_LORA, ATT_HEADS * HEAD_DIM, KV_HEADS * HEAD_DIM, KV_HEADS * HEAD_DIM)
IN_COLS = sum(SPLIT_SIZES)

kernel_name = 'hybrid_prefix_diffusion_step'


def rmsnorm(x, g):
    xf = x.astype(jnp.float32)
    y = xf * lax.rsqrt(jnp.mean(xf * xf, axis=-1, keepdims=True) + RMS_EPS)
    return (y * g.astype(jnp.float32)).astype(x.dtype)


def layernorm(x, g, b=None, eps=LN_EPS):
    xf = x.astype(jnp.float32)
    mu = jnp.mean(xf, axis=-1, keepdims=True)
    var = jnp.mean(jnp.square(xf - mu), axis=-1, keepdims=True)
    y = (xf - mu) * lax.rsqrt(var + eps) * g.astype(jnp.float32)
    if b is not None:
        y = y + b.astype(jnp.float32)
    return y.astype(x.dtype)


def split_cols(z):
    idx, acc = [], 0
    for s in SPLIT_SIZES[:-1]:
        acc += s
        idx.append(acc)
    return jnp.split(z, idx, axis=-1)


def rope_axis(x, pos):
    half = x.shape[-1] // 2
    inv = ROPE_BASE ** (-jnp.arange(half, dtype=jnp.float32) / half)
    ang = pos.astype(jnp.float32)[:, None] * inv[None, :]
    cos = jnp.cos(ang)[None, :, None, :].astype(x.dtype)
    sin = jnp.sin(ang)[None, :, None, :].astype(x.dtype)
    x1, x2 = x[..., :half], x[..., half:]
    return jnp.concatenate([x1 * cos - x2 * sin, x2 * cos + x1 * sin], axis=-1)


def rope_2d(x, row, col):
    r = HEAD_DIM // 2
    return jnp.concatenate([rope_axis(x[..., :r], row), rope_axis(x[..., r:], col)], axis=-1)


def gmlp_spatial_gate(u, v, ln_g, ws, bs):
    B, L, W = v.shape
    vn = layernorm(v, ln_g).reshape(B, L // CHUNK, CHUNK, GMLP_GROUPS, W // GMLP_GROUPS)
    mixed = jnp.einsum('gpq,bcqgd->bcpgd', ws, vn) + bs.T[:, :, None]
    return u * mixed.reshape(B, L, W)


def conformer_conv(val, gate, conv_w, conv_b, ln_g, ln_b):
    y = val * jax.nn.sigmoid(gate)
    y = lax.conv_general_dilated(y, conv_w[:, None, :], window_strides=(1,), padding=[(CONV_W // 2, CONV_W // 2)], dimension_numbers=('NWC', 'WIO', 'NWC'), feature_group_count=BRANCH_W) + conv_b
    return jax.nn.silu(layernorm(y, ln_g, ln_b))


def rwkv_scan(r, w, k, v, kk, a, s0, reverse):
    def step(S, inp):
        r_t, w_t, k_t, v_t, kk_t, a_t = inp
        sa = jnp.einsum('bhvk,bhk->bhv', S, -kk_t)
        S = S * w_t[:, :, None, :] + sa[..., None] * (kk_t * a_t)[:, :, None, :] + v_t[..., None] * k_t[:, :, None, :]
        return S, jnp.einsum('bhvk,bhk->bhv', S, r_t)
    xs = tuple(jnp.moveaxis(t, 1, 0) for t in (r, w, k, v, kk, a))
    s_fin, ys = lax.scan(step, s0, xs, reverse=reverse)
    return s_fin, jnp.moveaxis(ys, 0, 1)


def rwkv7_bidir(r, k, v, wd, ad, gd, lp, s_init):
    B, L, W = r.shape
    f32 = jnp.float32
    heads = lambda t: t.reshape(B, L, RWKV_HEADS, HEAD_DIM)
    w_raw = lp['rwkv_w0'] + jnp.einsum('bldr,drc->bldc', jnp.tanh(wd.reshape(B, L, N_DIR, DECAY_LORA)), lp['rwkv_w2'])
    decay = jnp.exp(-jnp.exp(-jax.nn.softplus(-w_raw.astype(f32)) - 0.5))
    a = jax.nn.sigmoid((lp['rwkv_a0'] + jnp.einsum('bldr,drc->bldc', ad.reshape(B, L, N_DIR, ICL_LORA), lp['rwkv_a2'])).astype(f32))
    g = jax.nn.sigmoid(gd) @ lp['rwkv_g2']
    kk = heads((k * lp['rwkv_kk']).astype(f32))
    kk = kk / jnp.maximum(jnp.sqrt(jnp.sum(kk * kk, axis=-1, keepdims=True)), 1e-12)
    k_dir = k.astype(f32)[:, :, None, :] * (1.0 + (a - 1.0) * lp['rwkv_ka'].astype(f32))
    rf, vf = heads(r.astype(f32)), heads(v.astype(f32))
    s0 = jnp.zeros((B, N_DIR, RWKV_HEADS, HEAD_DIM, HEAD_DIM), f32) if s_init is None else s_init.astype(f32)
    s_f, y_f = rwkv_scan(rf, heads(decay[:, :, 0]), heads(k_dir[:, :, 0]), vf, kk, heads(a[:, :, 0]), s0[:, 0], False)
    s_b, y_b = rwkv_scan(rf, heads(decay[:, :, 1]), heads(k_dir[:, :, 1]), vf, kk, heads(a[:, :, 1]), s0[:, 1], True)
    y = y_f + y_b
    mu = jnp.mean(y, axis=-1, keepdims=True)
    var = jnp.mean(jnp.square(y - mu), axis=-1, keepdims=True)
    y = ((y - mu) * lax.rsqrt(var + GN_EPS)).reshape(B, L, W) * lp['rwkv_lnx_g'].astype(f32) + lp['rwkv_lnx_b'].astype(f32)
    bonus = jnp.sum(rf * heads(k_dir[:, :, 0] + k_dir[:, :, 1]) * lp['rwkv_rk'].astype(f32), axis=-1, keepdims=True) * vf
    y = y + bonus.reshape(B, L, W)
    return y.astype(r.dtype) * g, jnp.stack([s_f, s_b], axis=1)


def sink_attend(q, keys, vals, masks, sink):
    scores = []
    for kx, mk in zip(keys, masks):
        s = jnp.einsum('bqkgd,bskd->bkgqs', q, kx).astype(jnp.float32) * ATT_SCALE
        scores.append(s if mk is None else jnp.where(mk, s, NEG_INF))
    sink_col = jnp.broadcast_to(sink.astype(jnp.float32)[None, :, :, None, None], scores[0].shape[:-1] + (1,))
    p = jax.nn.softmax(jnp.concatenate(scores + [sink_col], axis=-1), axis=-1)
    out, off = None, 0
    for s, vx in zip(scores, vals):
        n = s.shape[-1]
        o = jnp.einsum('bkgqs,bskd->bqkgd', p[..., off:off + n].astype(vx.dtype), vx)
        out = o if out is None else out + o
        off += n
    return out


def context_attention(q, k, v, sink):
    B, L = q.shape[:2]
    nb = L // BLOCK
    qb = jnp.moveaxis(q.reshape(B, nb, BLOCK, KV_HEADS, Q_PER_KV, HEAD_DIM), 1, 0)
    o = lax.map(lambda qi: sink_attend(qi, [k], [v], [None], sink), qb)
    return jnp.moveaxis(o, 0, 1).reshape(B, L, ATT_HEADS * HEAD_DIM)


def latent_attention(q, k, v, ctx_k, ctx_v, sink):
    B, T = q.shape[:2]
    nb = T // BLOCK
    qb = jnp.moveaxis(q.reshape(B, nb, BLOCK, KV_HEADS, Q_PER_KV, HEAD_DIM), 1, 0)

    def band(t):
        tp = jnp.pad(t, ((0, 0), (BLOCK, BLOCK), (0, 0), (0, 0))).reshape(B, nb + 2, BLOCK, KV_HEADS, HEAD_DIM)
        return jnp.moveaxis(jnp.concatenate([tp[:, :-2], tp[:, 1:-1], tp[:, 2:]], axis=2), 1, 0)

    i_rel = jnp.arange(BLOCK)[:, None]
    j_rel = jnp.arange(3 * BLOCK)[None, :]

    def one_block(args):
        qi, kw, vw, n = args
        i_abs = n * BLOCK + i_rel
        j_abs = n * BLOCK - BLOCK + j_rel
        mask = (jnp.abs(i_abs - j_abs) <= WINDOW) & (j_abs >= 0) & (j_abs < T)
        return sink_attend(qi, [kw, ctx_k], [vw, ctx_v], [mask, None], sink)

    o = lax.map(one_block, (qb, band(k), band(v), jnp.arange(nb)))
    return jnp.moveaxis(o, 0, 1).reshape(B, T, ATT_HEADS * HEAD_DIM)


def mixing_sublayer(h, lp, ctx):
    B, L, _ = h.shape
    (a_u, a_v, b_val, b_gate, c_r, c_k, c_v, c_wd, c_ad, c_gd, d_q, d_k, d_v) = split_cols(h @ lp['w_in'])
    o_a = gmlp_spatial_gate(jax.nn.gelu(a_u), jax.nn.gelu(a_v), lp['gmlp_ln_g'], lp['gmlp_ws'], lp['gmlp_bs'])
    o_b = conformer_conv(b_val, b_gate, lp['conv_w'], lp['conv_b'], lp['conv_ln_g'], lp['conv_ln_b'])
    o_c, s_fin = rwkv7_bidir(c_r, c_k, c_v, c_wd, c_ad, c_gd, lp, None if ctx is None else ctx[2])
    q = d_q.reshape(B, L, ATT_HEADS, HEAD_DIM)
    k = d_k.reshape(B, L, KV_HEADS, HEAD_DIM)
    v = d_v.reshape(B, L, KV_HEADS, HEAD_DIM)
    sink = lp['attn_sink'].reshape(KV_HEADS, Q_PER_KV)
    if ctx is None:
        o_d = context_attention(q, k, v, sink)
    else:
        ctx_k, ctx_v, _, row, col = ctx
        o_d = latent_attention(rope_2d(q, row, col), rope_2d(k, row, col), v, ctx_k, ctx_v, sink)
    branches = jnp.stack([o_a, o_b, o_c, o_d], axis=2)
    proj = jnp.einsum('blnw,nwd->blnd', branches, lp['w_branch'])
    gates = jax.nn.sigmoid(h @ lp['w_gate'] + lp['b_gate']).reshape(B, L, N_BRANCH, D_MODEL)
    out = jnp.einsum('blnd,blnd->bld', gates, proj) @ lp['w_out']
    return out, (k, v, s_fin)


def hier_moe(h, lp):
    B, L, D = h.shape
    t = h.reshape(B * L, D)
    g_logit = (t @ lp['w_rg'] + lp['b_rg']).astype(jnp.float32)
    g_prob = jax.nn.softmax(g_logit, axis=-1)
    _, g_idx = lax.top_k(g_logit, 1)
    g_sel = jax.nn.one_hot(g_idx[:, 0], N_GROUPS, dtype=jnp.float32)
    g_w = jnp.sum(g_prob * g_sel, axis=-1, keepdims=True)
    e_logit = (t @ lp['w_re'] + lp['b_re']).astype(jnp.float32).reshape(-1, N_GROUPS, EXPERTS_PER_GROUP)
    e_in = jnp.einsum('ng,nge->ne', g_sel, e_logit)
    e_val, e_idx = lax.top_k(e_in, TOP_K)
    w = g_w * jax.nn.softmax(e_val, axis=-1)
    expert = g_idx * EXPERTS_PER_GROUP + e_idx
    combine = jnp.einsum('nk,nke->ne', w, jax.nn.one_hot(expert, N_EXPERTS, dtype=jnp.float32)).astype(h.dtype)
    hid = jax.nn.silu(jnp.einsum('nd,edf->nef', t, lp['w_e_gate'])) * jnp.einsum('nd,edf->nef', t, lp['w_e_up'])
    out = jnp.einsum('nef,efd->nd', hid * combine[:, :, None], lp['w_e_down'])
    return out.reshape(B, L, D)


def trunk_layer(x, mod, lp, ctx):
    h = rmsnorm(x, lp['norm1_g']) * (1 + mod[:, 1, None]) + mod[:, 0, None]
    mix, ctx_out = mixing_sublayer(h, lp, ctx)
    x = x + mod[:, 2, None] * mix
    h = rmsnorm(x, lp['norm2_g']) * (1 + mod[:, 4, None]) + mod[:, 3, None]
    x = x + mod[:, 5, None] * hier_moe(h, lp)
    return x, ctx_out


def setup_inputs(seed: int = 0) -> dict:
    key = jax.random.key(seed)
    ks = iter(jax.random.split(key, 64))

    def nrm(shape, scale=1.0):
        return jax.random.normal(next(ks), shape, jnp.float32) * scale

    D, W = D_MODEL, BRANCH_W
    return {
        'x_prompt': nrm((BATCH, SEQ, D)),
        'x_sample': nrm((DEC_BATCH, DEC_SEQ, D)),
        'cache_k': nrm((DEC_BATCH, DEPTH, PAST_LEN, KV_HEADS, HEAD_DIM)),
        'cache_v': nrm((DEC_BATCH, DEPTH, PAST_LEN, KV_HEADS, HEAD_DIM)),
        'state_rwkv': nrm((DEC_BATCH, DEPTH, N_DIR, RWKV_HEADS, HEAD_DIM, HEAD_DIM), 0.5),
        'c': nrm((DEC_BATCH, D)),
        'c_ctx': nrm((D,)),
        'norm1_g': 1.0 + nrm((DEPTH, D), 0.02),
        'norm2_g': 1.0 + nrm((DEPTH, D), 0.02),
        'final_norm_g': 1.0 + nrm((D,), 0.02),
        'w_mod': nrm((DEPTH, D, N_MOD * D), 0.5 * D ** -0.5),
        'b_mod': nrm((DEPTH, N_MOD * D), 0.02),
        'w_in': nrm((DEPTH, D, IN_COLS), D ** -0.5),
        'gmlp_ln_g': 1.0 + nrm((DEPTH, W), 0.02),
        'gmlp_ws': nrm((DEPTH, GMLP_GROUPS, CHUNK, CHUNK), CHUNK ** -0.5),
        'gmlp_bs': 1.0 + nrm((DEPTH, GMLP_GROUPS, CHUNK), 0.02),
        'conv_w': nrm((DEPTH, CONV_W, W), CONV_W ** -0.5),
        'conv_b': nrm((DEPTH, W), 0.02),
        'conv_ln_g': 1.0 + nrm((DEPTH, W), 0.02),
        'conv_ln_b': nrm((DEPTH, W), 0.02),
        'rwkv_w0': -2.0 + nrm((DEPTH, N_DIR, W), 0.5),
        'rwkv_w2': nrm((DEPTH, N_DIR, DECAY_LORA, W), 0.5 * DECAY_LORA ** -0.5),
        'rwkv_a0': nrm((DEPTH, N_DIR, W), 0.5),
        'rwkv_a2': nrm((DEPTH, N_DIR, ICL_LORA, W), 0.5 * ICL_LORA ** -0.5),
        'rwkv_g2': nrm((DEPTH, GATE_LORA, W), GATE_LORA ** -0.5),
        'rwkv_kk': 1.0 + nrm((DEPTH, W), 0.1),
        'rwkv_ka': 1.0 + nrm((DEPTH, W), 0.1),
        'rwkv_rk': nrm((DEPTH, RWKV_HEADS, HEAD_DIM), 0.1),
        'rwkv_lnx_g': 1.0 + nrm((DEPTH, W), 0.02),
        'rwkv_lnx_b': nrm((DEPTH, W), 0.02),
        'attn_sink': nrm((DEPTH, ATT_HEADS), 0.5),
        'w_gate': nrm((DEPTH, D, N_BRANCH * D), D ** -0.5),
        'b_gate': nrm((DEPTH, N_BRANCH * D), 0.02),
        'w_branch': nrm((DEPTH, N_BRANCH, W, D), W ** -0.5),
        'w_out': nrm((DEPTH, D, D), D ** -0.5),
        'w_rg': nrm((DEPTH, D, N_GROUPS), D ** -0.5),
        'b_rg': nrm((DEPTH, N_GROUPS), 0.01),
        'w_re': nrm((DEPTH, D, N_EXPERTS), D ** -0.5),
        'b_re': nrm((DEPTH, N_EXPERTS), 0.01),
        'w_e_gate': nrm((DEPTH, N_EXPERTS, D, EXPERT_FF), D ** -0.5),
        'w_e_up': nrm((DEPTH, N_EXPERTS, D, EXPERT_FF), D ** -0.5),
        'w_e_down': nrm((DEPTH, N_EXPERTS, EXPERT_FF, D), EXPERT_FF ** -0.5),
    }


def reference(x_prompt, x_sample, cache_k, cache_v, state_rwkv, c, c_ctx, norm1_g, norm2_g, final_norm_g, w_mod, b_mod, w_in, gmlp_ln_g, gmlp_ws, gmlp_bs, conv_w, conv_b, conv_ln_g, conv_ln_b, rwkv_w0, rwkv_w2, rwkv_a0, rwkv_a2, rwkv_g2, rwkv_kk, rwkv_ka, rwkv_rk, rwkv_lnx_g, rwkv_lnx_b, attn_sink, w_gate, b_gate, w_branch, w_out, w_rg, b_rg, w_re, b_re, w_e_gate, w_e_up, w_e_down):
    layer_params = {
        'norm1_g': norm1_g, 'norm2_g': norm2_g, 'w_mod': w_mod, 'b_mod': b_mod, 'w_in': w_in,
        'gmlp_ln_g': gmlp_ln_g, 'gmlp_ws': gmlp_ws, 'gmlp_bs': gmlp_bs,
        'conv_w': conv_w, 'conv_b': conv_b, 'conv_ln_g': conv_ln_g, 'conv_ln_b': conv_ln_b,
        'rwkv_w0': rwkv_w0, 'rwkv_w2': rwkv_w2, 'rwkv_a0': rwkv_a0, 'rwkv_a2': rwkv_a2, 'rwkv_g2': rwkv_g2,
        'rwkv_kk': rwkv_kk, 'rwkv_ka': rwkv_ka, 'rwkv_rk': rwkv_rk, 'rwkv_lnx_g': rwkv_lnx_g, 'rwkv_lnx_b': rwkv_lnx_b,
        'attn_sink': attn_sink, 'w_gate': w_gate, 'b_gate': b_gate, 'w_branch': w_branch, 'w_out': w_out,
        'w_rg': w_rg, 'b_rg': b_rg, 'w_re': w_re, 'b_re': b_re,
        'w_e_gate': w_e_gate, 'w_e_up': w_e_up, 'w_e_down': w_e_down,
    }
    n_lat = x_sample.shape[1]
    rows = n_lat // GRID_W
    row = jnp.repeat(jnp.arange(rows), GRID_W)
    col = jnp.tile(jnp.arange(GRID_W), rows)

    xp, xs = x_prompt, x_sample
    ks_out, vs_out, ss_out = [], [], []
    for l in range(DEPTH):
        lp = {name: arr[l] for name, arr in layer_params.items()}
        mod_ctx = (jax.nn.silu(c_ctx) @ lp['w_mod'] + lp['b_mod']).reshape(1, N_MOD, D_MODEL)
        mod_lat = (jax.nn.silu(c) @ lp['w_mod'] + lp['b_mod']).reshape(-1, N_MOD, D_MODEL)
        xp, (k_l, v_l, s_l) = trunk_layer(xp, mod_ctx, lp, None)
        ks_out.append(k_l)
        vs_out.append(v_l)
        ss_out.append(s_l.astype(x_prompt.dtype))
        xs, _ = trunk_layer(xs, mod_lat, lp, (cache_k[:, l], cache_v[:, l], state_rwkv[:, l], row, col))

    y_prompt = rmsnorm(xp, final_norm_g)
    y_sample = rmsnorm(xs, final_norm_g)
    new_cache_k = jnp.stack(ks_out, axis=1)
    new_cache_v = jnp.stack(vs_out, axis=1)
    new_state_rwkv = jnp.stack(ss_out, axis=1)
    return (y_prompt, y_sample, new_cache_k, new_cache_v, new_state_rwkv)
```

```python
import functools
import math

import jax
import jax.numpy as jnp
from jax import lax
from jax.experimental import pallas as pl
from jax.experimental.pallas import tpu as pltpu

D_MODEL = 1024
BATCH = 32
SEQ = 256
DEPTH = 2
DEC_BATCH = 4
DEC_SEQ = 1024
PAST_LEN = 256
GRID_W = 64
HEAD_DIM = 64
BRANCH_W = 256
N_BRANCH = 4
CHUNK = 128
GMLP_GROUPS = 4
CONV_W = 31
RWKV_HEADS = 4
N_DIR = 2
DECAY_LORA = 64
ICL_LORA = 64
GATE_LORA = 128
ATT_HEADS = 4
KV_HEADS = 2
Q_PER_KV = ATT_HEADS // KV_HEADS
WINDOW = 128
BLOCK = 128
ROPE_BASE = 10000.0
ATT_SCALE = HEAD_DIM ** -0.5
N_GROUPS = 4
EXPERTS_PER_GROUP = 4
N_EXPERTS = N_GROUPS * EXPERTS_PER_GROUP
EXPERT_FF = 256
N_MOD = 6
RMS_EPS = 1e-6
LN_EPS = 1e-5
GN_EPS = 64e-5
IN_COLS = 2688

V7X_LANES = 128
V7X_SUBLANES = 8
V7X_VMEM_LIMIT = 56 * 1024 * 1024

N_CTX_TOK = BATCH * SEQ
N_LAT_TOK = DEC_BATCH * DEC_SEQ
N_TOK = N_CTX_TOK + N_LAT_TOK
MOD_ROWS = 8
TOK_TILE = 512
SCAN_TB = 32
CONV_PAD = 16

Z_AB, Z_C, Z_D, Z_G = 0, 1024, 2048, 2560

F32 = jnp.float32
BF16 = jnp.bfloat16
HIGHEST = lax.Precision.HIGHEST


def _cparams(n_axes, vmem=None):
    return pltpu.CompilerParams(dimension_semantics=("arbitrary",) * n_axes, vmem_limit_bytes=vmem)


def _const_spec(shape):
    nd = len(shape)
    return pl.BlockSpec(shape, lambda *_: (0,) * nd)


def _dot(a, b):
    return jnp.dot(a, b, preferred_element_type=F32)


def _dot_hi(a, b):
    return jnp.dot(a, b, preferred_element_type=F32, precision=HIGHEST)


def _head_ones():
    r = lax.broadcasted_iota(jnp.int32, (BRANCH_W, BRANCH_W), 0) // HEAD_DIM
    c = lax.broadcasted_iota(jnp.int32, (BRANCH_W, BRANCH_W), 1) // HEAD_DIM
    return (r == c).astype(F32)


def _mod_row(tile, n_ctx_tiles, tiles_per_seq):
    return jnp.where(tile < n_ctx_tiles, 0, 1 + jnp.maximum(tile - n_ctx_tiles, 0) // tiles_per_seq)


def _mod_norm(x, g, mod_ref, row, j_shift, j_scale):
    sh = mod_ref[pl.ds(row, 1), j_shift * D_MODEL:(j_shift + 1) * D_MODEL]
    sc = mod_ref[pl.ds(row, 1), j_scale * D_MODEL:(j_scale + 1) * D_MODEL]
    y = x * lax.rsqrt(jnp.mean(x * x, axis=-1, keepdims=True) + RMS_EPS) * g
    return y * (1.0 + sc) + sh


def _layernorm(x, g, b=None, eps=LN_EPS):
    mu = jnp.mean(x, axis=-1, keepdims=True)
    d = x - mu
    var = jnp.mean(d * d, axis=-1, keepdims=True)
    y = d * lax.rsqrt(var + eps) * g
    return y if b is None else y + b


def _mod_kernel(c_ref, w_ref, b_ref, o_ref):
    c = c_ref[...]
    a = c * jax.nn.sigmoid(c)
    o_ref[0] = _dot_hi(a, w_ref[0]) + b_ref[0]


def _modulation(cvec, w_mod, b_mod):
    nj = N_MOD
    return pl.pallas_call(
        _mod_kernel,
        grid=(DEPTH, nj),
        in_specs=[
            pl.BlockSpec((MOD_ROWS, D_MODEL), lambda l, j: (0, 0)),
            pl.BlockSpec((1, D_MODEL, D_MODEL), lambda l, j: (l, 0, j)),
            pl.BlockSpec((1, 1, D_MODEL), lambda l, j: (l, 0, j)),
        ],
        out_specs=pl.BlockSpec((1, MOD_ROWS, D_MODEL), lambda l, j: (l, 0, j)),
        out_shape=jax.ShapeDtypeStruct((DEPTH, MOD_ROWS, N_MOD * D_MODEL), F32),
        compiler_params=_cparams(2),
        name="modulation",
    )(cvec, w_mod, b_mod.reshape(DEPTH, 1, N_MOD * D_MODEL))


def _inproj_kernel(x_ref, mod_ref, g_ref, w_ref, z_ref, *, n_ctx_tiles, tiles_per_seq):
    row = _mod_row(pl.program_id(0), n_ctx_tiles, tiles_per_seq)
    h = _mod_norm(x_ref[...], g_ref[...], mod_ref, row, 0, 1)
    z_ref[...] = _dot(h.astype(BF16), w_ref[...])


def _inproj(x, mod, g1, w_in_b):
    tm = TOK_TILE
    kern = functools.partial(_inproj_kernel, n_ctx_tiles=N_CTX_TOK // tm, tiles_per_seq=DEC_SEQ // tm)
    return pl.pallas_call(
        kern,
        grid=(N_TOK // tm,),
        in_specs=[
            pl.BlockSpec((tm, D_MODEL), lambda i: (i, 0)),
            _const_spec((MOD_ROWS, N_MOD * D_MODEL)),
            _const_spec((1, D_MODEL)),
            _const_spec((D_MODEL, IN_COLS)),
        ],
        out_specs=pl.BlockSpec((tm, IN_COLS), lambda i: (i, 0)),
        out_shape=jax.ShapeDtypeStruct((N_TOK, IN_COLS), F32),
        compiler_params=_cparams(1, V7X_VMEM_LIMIT),
        name="inproj",
    )(x, mod, g1, w_in_b)


def _local_mix_kernel(z_ref, lng_ref, wcat_ref, bsx_ref, cw_ref, cb_ref, clg_ref, clb_ref,
                      oa_ref, ob_ref, ypad, *, seq_len):
    lane_group = lax.broadcasted_iota(jnp.int32, (CHUNK, BRANCH_W), 1) // (BRANCH_W // GMLP_GROUPS)
    for c in range(seq_len // CHUNK):
        rows = pl.ds(c * CHUNK, CHUNK)
        u = jax.nn.gelu(z_ref[rows, 0:BRANCH_W], approximate=True)
        v = jax.nn.gelu(z_ref[rows, BRANCH_W:2 * BRANCH_W], approximate=True)
        vn = _layernorm(v, lng_ref[...])
        vblk = jnp.concatenate(
            [jnp.where(lane_group == g, vn, 0.0) for g in range(GMLP_GROUPS)], axis=0).astype(BF16)
        mixed = _dot(wcat_ref[...], vblk) + bsx_ref[...]
        oa_ref[rows, :] = (u * mixed).astype(BF16)

    zeros = jnp.zeros((CONV_PAD, BRANCH_W), F32)
    ypad[0:CONV_PAD, :] = zeros
    ypad[CONV_PAD + seq_len:2 * CONV_PAD + seq_len, :] = zeros
    ypad[CONV_PAD:CONV_PAD + seq_len, :] = (
        z_ref[:, 2 * BRANCH_W:3 * BRANCH_W] * jax.nn.sigmoid(z_ref[:, 3 * BRANCH_W:4 * BRANCH_W]))
    base = CONV_PAD - CONV_W // 2
    for c in range(seq_len // CHUNK):
        acc = jnp.zeros((CHUNK, BRANCH_W), F32)
        for j in range(CONV_W):
            acc = acc + cw_ref[j:j + 1, :] * ypad[pl.ds(c * CHUNK + base + j, CHUNK), :]
        y = _layernorm(acc + cb_ref[...], clg_ref[...], clb_ref[...])
        ob_ref[pl.ds(c * CHUNK, CHUNK), :] = (y * jax.nn.sigmoid(y)).astype(BF16)


def _local_mix(z, seq_len, n_seq, row_block0, p):
    kern = functools.partial(_local_mix_kernel, seq_len=seq_len)
    out = jax.ShapeDtypeStruct((n_seq * seq_len, BRANCH_W), BF16)
    return pl.pallas_call(
        kern,
        grid=(n_seq,),
        in_specs=[
            pl.BlockSpec((seq_len, 4 * BRANCH_W), lambda i: (i + row_block0, Z_AB // (4 * BRANCH_W))),
            _const_spec((1, BRANCH_W)),
            _const_spec((CHUNK, GMLP_GROUPS * CHUNK)),
            _const_spec((CHUNK, BRANCH_W)),
            _const_spec((CONV_W, BRANCH_W)),
            _const_spec((1, BRANCH_W)),
            _const_spec((1, BRANCH_W)),
            _const_spec((1, BRANCH_W)),
        ],
        out_specs=[pl.BlockSpec((seq_len, BRANCH_W), lambda i: (i, 0))] * 2,
        out_shape=[out, out],
        scratch_shapes=[pltpu.VMEM((seq_len + 2 * CONV_PAD, BRANCH_W), F32)],
        compiler_params=_cparams(1, V7X_VMEM_LIMIT),
        name=f"local_mix_{seq_len}",
    )(z, p["gmlp_ln_g"], p["gmlp_wcat"], p["gmlp_bsx"], p["conv_w"], p["conv_b"],
      p["conv_ln_g"], p["conv_ln_b"])


def _rwkv_prep_kernel(zc_ref, zg_ref, w0_ref, w2_ref, a0_ref, a2_ref, g2_ref, kkp_ref, ka_ref,
                      rk_ref, ones_ref, r_o, v_o, kk_o, w_o, kd_o, b_o, g_o, bonus_o):
    W = BRANCH_W
    r = zc_ref[:, 0:W]
    k = zc_ref[:, W:2 * W]
    v = zc_ref[:, 2 * W:3 * W]
    wd = zc_ref[:, 3 * W:3 * W + N_DIR * DECAY_LORA]
    ad = zc_ref[:, 3 * W + N_DIR * DECAY_LORA:4 * W]
    ones = ones_ref[...]
    w_raw = _dot_hi(jnp.tanh(wd), w2_ref[...]) + w0_ref[...]
    decay = jnp.exp(-math.exp(-0.5) * jax.nn.sigmoid(w_raw))
    a = jax.nn.sigmoid(_dot_hi(ad, a2_ref[...]) + a0_ref[...])
    g_o[...] = _dot_hi(jax.nn.sigmoid(zg_ref[...]), g2_ref[...])
    kkr = k * kkp_ref[...]
    kk = kkr / jnp.maximum(jnp.sqrt(_dot_hi(kkr * kkr, ones)), 1e-12)
    ka = ka_ref[...]
    kd_sum = None
    for d in range(N_DIR):
        a_d = a[:, d * W:(d + 1) * W]
        kd = k * (1.0 + (a_d - 1.0) * ka)
        w_o[d] = decay[:, d * W:(d + 1) * W]
        kd_o[d] = kd
        b_o[d] = kk * a_d
        kd_sum = kd if kd_sum is None else kd_sum + kd
    bonus_o[...] = _dot_hi(r * kd_sum * rk_ref[...], ones) * v
    r_o[...] = r
    v_o[...] = v
    kk_o[...] = kk


def _rwkv_prep(z, p, ones):
    tm = TOK_TILE
    tok = jax.ShapeDtypeStruct((N_TOK, BRANCH_W), F32)
    tok2 = jax.ShapeDtypeStruct((N_DIR, N_TOK, BRANCH_W), F32)
    s1 = pl.BlockSpec((tm, BRANCH_W), lambda i: (i, 0))
    s2 = pl.BlockSpec((N_DIR, tm, BRANCH_W), lambda i: (0, i, 0))
    return pl.pallas_call(
        _rwkv_prep_kernel,
        grid=(N_TOK // tm,),
        in_specs=[
            pl.BlockSpec((tm, 4 * BRANCH_W), lambda i: (i, Z_C // (4 * BRANCH_W))),
            pl.BlockSpec((tm, GATE_LORA), lambda i: (i, Z_G // GATE_LORA)),
            _const_spec((1, N_DIR * BRANCH_W)),
            _const_spec((N_DIR * DECAY_LORA, N_DIR * BRANCH_W)),
            _const_spec((1, N_DIR * BRANCH_W)),
            _const_spec((N_DIR * ICL_LORA, N_DIR * BRANCH_W)),
            _const_spec((GATE_LORA, BRANCH_W)),
            _const_spec((1, BRANCH_W)),
            _const_spec((1, BRANCH_W)),
            _const_spec((1, BRANCH_W)),
            _const_spec((BRANCH_W, BRANCH_W)),
        ],
        out_specs=[s1, s1, s1, s2, s2, s2, s1, s1],
        out_shape=[tok, tok, tok, tok2, tok2, tok2, tok, tok],
        compiler_params=_cparams(1, V7X_VMEM_LIMIT),
        name="rwkv_prep",
    )(z, z, p["rwkv_w0"], p["rwkv_w2blk"], p["rwkv_a0"], p["rwkv_a2blk"], p["rwkv_g2"],
      p["rwkv_kk"], p["rwkv_ka"], p["rwkv_rk"], ones)


def _scan_kernel(*refs, n_v8, tb, has_init):
    if has_init:
        r_ref, kk_ref, v_ref, w_ref, kd_ref, b_ref, s0_ref, y_ref, sfin_ref, S = refs
    else:
        r_ref, kk_ref, v_ref, w_ref, kd_ref, b_ref, y_ref, sfin_ref, S = refs
    d = pl.program_id(0)
    i = pl.program_id(1)
    n8 = V7X_SUBLANES

    @pl.when(i == 0)
    def _():
        if has_init:
            S[...] = s0_ref[...]
        else:
            S[...] = jnp.zeros_like(S)

    def row8(ref, t, k):
        return jnp.broadcast_to(ref[t, pl.ds(k, 1), :], (n8, V7X_LANES))

    def step(s, carry):
        t = s + d * (tb - 1 - 2 * s)

        def pass1(k, sa):
            kkb = row8(kk_ref, t, k)
            return tuple(sa[vo] + S[k, vo * n8:(vo + 1) * n8, :] * kkb for vo in range(n_v8))

        zero = tuple(jnp.zeros((n8, V7X_LANES), F32) for _ in range(n_v8))
        sa = lax.fori_loop(0, HEAD_DIM, pass1, zero, unroll=8)
        nsa = tuple(-x for x in sa)
        vt = tuple(v_ref[t, vo * n8:(vo + 1) * n8, :] for vo in range(n_v8))

        def pass2(k, y):
            wb = row8(w_ref, t, k)
            bb = row8(b_ref, t, k)
            kb = row8(kd_ref, t, k)
            rb = row8(r_ref, t, k)
            out = []
            for vo in range(n_v8):
                sl = slice(vo * n8, (vo + 1) * n8)
                sn = S[k, sl, :] * wb + (nsa[vo] * bb + vt[vo] * kb)
                S[k, sl, :] = sn
                out.append(y[vo] + sn * rb)
            return tuple(out)

        y = lax.fori_loop(0, HEAD_DIM, pass2, zero, unroll=8)
        for vo in range(n_v8):
            y_ref[t, vo * n8:(vo + 1) * n8, :] = y[vo]
        return carry

    lax.fori_loop(0, tb, step, 0)

    @pl.when(i == pl.num_programs(1) - 1)
    def _():
        sfin_ref[...] = S[...]


def _rwkv_scan(r_s, kk_s, v_s, w_s, kd_s, b_s, s0):
    nd, seq_len = w_s.shape[0], w_s.shape[1]
    nv = v_s.shape[1]
    tb = SCAN_TB
    nt = seq_len // tb

    def tmap(d, i):
        return i + d * (nt - 1 - 2 * i)

    shared_k = pl.BlockSpec((tb, HEAD_DIM, V7X_LANES), lambda d, i: (tmap(d, i), 0, 0))
    shared_v = pl.BlockSpec((tb, nv, V7X_LANES), lambda d, i: (tmap(d, i), 0, 0))
    per_dir = pl.BlockSpec((None, tb, HEAD_DIM, V7X_LANES), lambda d, i: (d, tmap(d, i), 0, 0))
    state = pl.BlockSpec((None, HEAD_DIM, nv, V7X_LANES), lambda d, i: (d, 0, 0, 0))
    in_specs = [shared_k, shared_k, shared_v, per_dir, per_dir, per_dir]
    args = [r_s, kk_s, v_s, w_s, kd_s, b_s]
    if s0 is not None:
        in_specs.append(state)
        args.append(s0)
    kern = functools.partial(_scan_kernel, n_v8=nv // V7X_SUBLANES, tb=tb, has_init=s0 is not None)
    return pl.pallas_call(
        kern,
        grid=(nd, nt),
        in_specs=in_specs,
        out_specs=[pl.BlockSpec((None, tb, nv, V7X_LANES), lambda d, i: (d, tmap(d, i), 0, 0)), state],
        out_shape=[jax.ShapeDtypeStruct((nd, seq_len, nv, V7X_LANES), F32),
                   jax.ShapeDtypeStruct((nd, HEAD_DIM, nv, V7X_LANES), F32)],
        scratch_shapes=[pltpu.VMEM((HEAD_DIM, nv, V7X_LANES), F32)],
        compiler_params=_cparams(2, V7X_VMEM_LIMIT),
        name=f"rwkv_scan_{seq_len}",
    )(*args)


def _softmax_pv(scores, vals, sink_col):
    m = sink_col
    for s in scores:
        m = jnp.maximum(m, jnp.max(s, axis=-1, keepdims=True))
    den = jnp.exp(sink_col - m)
    out = None
    for s, vx in zip(scores, vals):
        p = jnp.exp(s - m)
        den = den + jnp.sum(p, axis=-1, keepdims=True)
        o = _dot(p.astype(BF16), vx)
        out = o if out is None else out + o
    return out / den


def _sink_col(sink_ref, kv, n_rows):
    row = lax.broadcasted_iota(jnp.int32, (Q_PER_KV * n_rows, 1), 0)
    col = jnp.full((Q_PER_KV * n_rows, 1), sink_ref[kv * Q_PER_KV], F32)
    for g in range(1, Q_PER_KV):
        col = jnp.where(row >= g * n_rows, sink_ref[kv * Q_PER_KV + g], col)
    return col


def _qk(q2, kh):
    return lax.dot_general(q2, kh, (((1,), (1,)), ((), ())), preferred_element_type=F32) * ATT_SCALE


def _ctx_attn_kernel(sink_ref, z_ref, o_ref):
    hd = HEAD_DIM
    q = z_ref[:, 0:ATT_HEADS * hd]
    outs = []
    for kv in range(KV_HEADS):
        kh = z_ref[:, ATT_HEADS * hd + kv * hd:ATT_HEADS * hd + (kv + 1) * hd].astype(BF16)
        vh = z_ref[:, (ATT_HEADS + KV_HEADS) * hd + kv * hd:(ATT_HEADS + KV_HEADS) * hd + (kv + 1) * hd].astype(BF16)
        q2 = jnp.concatenate(
            [q[:, (kv * Q_PER_KV + g) * hd:(kv * Q_PER_KV + g + 1) * hd] for g in range(Q_PER_KV)],
            axis=0).astype(BF16)
        o2 = _softmax_pv([_qk(q2, kh)], [vh], _sink_col(sink_ref, kv, SEQ))
        outs += [o2[g * SEQ:(g + 1) * SEQ] for g in range(Q_PER_KV)]
    o_ref[...] = jnp.concatenate(outs, axis=1).astype(BF16)


def _rope(x, cos, sin_signed):
    lane = lax.broadcasted_iota(jnp.int32, x.shape, 1)
    first = (lane % (HEAD_DIM // 2)) < (HEAD_DIM // 4)
    partner = jnp.where(first, pltpu.roll(x, V7X_LANES - HEAD_DIM // 4, 1), pltpu.roll(x, HEAD_DIM // 4, 1))
    return x * cos + partner * sin_signed


def _lat_attn_kernel(sink_ref, z_ref, ck_ref, cv_ref, cos_ref, sin_ref, o_ref, q_s, k_s):
    hd = HEAD_DIM
    cos = cos_ref[...]
    sin = sin_ref[...]
    for j in range(ATT_HEADS * hd // V7X_LANES):
        q_s[:, j * V7X_LANES:(j + 1) * V7X_LANES] = _rope(
            z_ref[:, j * V7X_LANES:(j + 1) * V7X_LANES], cos, sin).astype(BF16)
    k_s[...] = _rope(z_ref[:, ATT_HEADS * hd:(ATT_HEADS + KV_HEADS) * hd], cos, sin).astype(BF16)
    voff = (ATT_HEADS + KV_HEADS) * hd
    nb = DEC_SEQ // BLOCK
    for n in range(nb):
        lo = max(n - 1, 0) * BLOCK
        hi = min(n + 2, nb) * BLOCK
        i_abs = n * BLOCK + lax.broadcasted_iota(jnp.int32, (Q_PER_KV * BLOCK, hi - lo), 0) % BLOCK
        j_abs = lo + lax.broadcasted_iota(jnp.int32, (Q_PER_KV * BLOCK, hi - lo), 1)
        band = jnp.abs(i_abs - j_abs) <= WINDOW
        outs = []
        for kv in range(KV_HEADS):
            q2 = jnp.concatenate(
                [q_s[n * BLOCK:(n + 1) * BLOCK, (kv * Q_PER_KV + g) * hd:(kv * Q_PER_KV + g + 1) * hd]
                 for g in range(Q_PER_KV)], axis=0)
            kw = k_s[lo:hi, kv * hd:(kv + 1) * hd]
            vw = z_ref[lo:hi, voff + kv * hd:voff + (kv + 1) * hd].astype(BF16)
            kc = ck_ref[:, kv * hd:(kv + 1) * hd].astype(BF16)
            vc = cv_ref[:, kv * hd:(kv + 1) * hd].astype(BF16)
            s_lat = jnp.where(band, _qk(q2, kw), -1e30)
            o2 = _softmax_pv([s_lat, _qk(q2, kc)], [vw, vc], _sink_col(sink_ref, kv, BLOCK))
            outs += [o2[g * BLOCK:(g + 1) * BLOCK] for g in range(Q_PER_KV)]
        o_ref[n * BLOCK:(n + 1) * BLOCK, :] = jnp.concatenate(outs, axis=1).astype(BF16)


def _ctx_attn(z, sink):
    wd = (ATT_HEADS + 2 * KV_HEADS) * HEAD_DIM
    return pl.pallas_call(
        _ctx_attn_kernel,
        grid=(BATCH,),
        in_specs=[
            pl.BlockSpec(memory_space=pltpu.SMEM),
            pl.BlockSpec((SEQ, wd), lambda i: (i, Z_D // wd)),
        ],
        out_specs=pl.BlockSpec((SEQ, BRANCH_W), lambda i: (i, 0)),
        out_shape=jax.ShapeDtypeStruct((N_CTX_TOK, BRANCH_W), BF16),
        compiler_params=_cparams(1, V7X_VMEM_LIMIT),
        name="ctx_attn",
    )(sink, z)


def _lat_attn(z, sink, ck, cv, cos_t, sin_t):
    wd = (ATT_HEADS + 2 * KV_HEADS) * HEAD_DIM
    kvw = KV_HEADS * HEAD_DIM
    row0 = N_CTX_TOK // DEC_SEQ
    return pl.pallas_call(
        _lat_attn_kernel,
        grid=(DEC_BATCH,),
        in_specs=[
            pl.BlockSpec(memory_space=pltpu.SMEM),
            pl.BlockSpec((DEC_SEQ, wd), lambda i: (i + row0, Z_D // wd)),
            pl.BlockSpec((None, PAST_LEN, kvw), lambda i: (i, 0, 0)),
            pl.BlockSpec((None, PAST_LEN, kvw), lambda i: (i, 0, 0)),
            _const_spec((DEC_SEQ, V7X_LANES)),
            _const_spec((DEC_SEQ, V7X_LANES)),
        ],
        out_specs=pl.BlockSpec((DEC_SEQ, BRANCH_W), lambda i: (i, 0)),
        out_shape=jax.ShapeDtypeStruct((N_LAT_TOK, BRANCH_W), BF16),
        scratch_shapes=[pltpu.VMEM((DEC_SEQ, ATT_HEADS * HEAD_DIM), BF16),
                        pltpu.VMEM((DEC_SEQ, kvw), BF16)],
        compiler_params=_cparams(1, V7X_VMEM_LIMIT),
        name="lat_attn",
    )(sink, z, ck, cv, cos_t, sin_t)


def _mix_kernel(x_ref, mod_ref, g1_ref, oa_ref, ob_ref, od_ref, yf_ref, yb_ref, g_ref, bonus_ref,
                lnxg_ref, lnxb_ref, ones_ref, wg_ref, bg_ref, wb_ref, wo_ref, x1_ref,
                *, n_ctx_tiles, tiles_per_seq):
    row = _mod_row(pl.program_id(0), n_ctx_tiles, tiles_per_seq)
    x = x_ref[...]
    hb = _mod_norm(x, g1_ref[...], mod_ref, row, 0, 1).astype(BF16)
    ones = ones_ref[...]
    y = yf_ref[...] + yb_ref[...]
    mu = _dot_hi(y, ones) * (1.0 / HEAD_DIM)
    dl = y - mu
    var = _dot_hi(dl * dl, ones) * (1.0 / HEAD_DIM)
    yn = dl * lax.rsqrt(var + GN_EPS) * lnxg_ref[...] + lnxb_ref[...]
    oc = ((yn + bonus_ref[...]) * g_ref[...]).astype(BF16)
    mixed = None
    for n, br in enumerate((oa_ref[...], ob_ref[...], oc, od_ref[...])):
        cols = slice(n * D_MODEL, (n + 1) * D_MODEL)
        gate = jax.nn.sigmoid(_dot(hb, wg_ref[:, cols]) + bg_ref[:, cols])
        term = gate * _dot(br, wb_ref[n])
        mixed = term if mixed is None else mixed + term
    mix = _dot(mixed.astype(BF16), wo_ref[...])
    g1 = mod_ref[pl.ds(row, 1), 2 * D_MODEL:3 * D_MODEL]
    x1_ref[...] = x + g1 * mix


def _mix(x, mod, p, oa, ob, od, yf, yb, g, bonus, ones):
    tm = TOK_TILE
    kern = functools.partial(_mix_kernel, n_ctx_tiles=N_CTX_TOK // tm, tiles_per_seq=DEC_SEQ // tm)
    tokw = pl.BlockSpec((tm, BRANCH_W), lambda i: (i, 0))
    tokd = pl.BlockSpec((tm, D_MODEL), lambda i: (i, 0))
    return pl.pallas_call(
        kern,
        grid=(N_TOK // tm,),
        in_specs=[
            tokd,
            _const_spec((MOD_ROWS, N_MOD * D_MODEL)),
            _const_spec((1, D_MODEL)),
            tokw, tokw, tokw, tokw, tokw, tokw, tokw,
            _const_spec((1, BRANCH_W)),
            _const_spec((1, BRANCH_W)),
            _const_spec((BRANCH_W, BRANCH_W)),
            _const_spec((D_MODEL, N_BRANCH * D_MODEL)),
            _const_spec((1, N_BRANCH * D_MODEL)),
            _const_spec((N_BRANCH, BRANCH_W, D_MODEL)),
            _const_spec((D_MODEL, D_MODEL)),
        ],
        out_specs=tokd,
        out_shape=jax.ShapeDtypeStruct((N_TOK, D_MODEL), F32),
        compiler_params=_cparams(1, V7X_VMEM_LIMIT),
        name="branch_mix",
    )(x, mod, p["norm1_g"], oa, ob, od, yf, yb, g, bonus, p["rwkv_lnx_g"], p["rwkv_lnx_b"], ones,
      p["w_gate"], p["b_gate"], p["w_branch"], p["w_out"])


ROUTER_LANES = V7X_LANES


def _route(logits):
    lane = lax.broadcasted_iota(jnp.int32, logits.shape, 1).astype(F32)
    ninf = -jnp.inf
    big = float(ROUTER_LANES)
    gmask = lane < N_GROUPS
    gl = jnp.where(gmask, logits, ninf)
    gmax = jnp.max(gl, axis=-1, keepdims=True)
    gidx = jnp.min(jnp.where(gl == gmax, lane, big), axis=-1, keepdims=True)
    g_w = 1.0 / jnp.sum(jnp.where(gmask, jnp.exp(gl - gmax), 0.0), axis=-1, keepdims=True)
    egroup = jnp.floor((lane - N_GROUPS) * (1.0 / EXPERTS_PER_GROUP))
    emask = (lane >= N_GROUPS) & (lane < N_GROUPS + N_EXPERTS) & (egroup == gidx)
    el = jnp.where(emask, logits, ninf)
    e1 = jnp.max(el, axis=-1, keepdims=True)
    i1 = jnp.min(jnp.where(emask & (el == e1), lane, big), axis=-1, keepdims=True)
    el2 = jnp.where(lane == i1, ninf, el)
    e2 = jnp.max(el2, axis=-1, keepdims=True)
    i2 = jnp.min(jnp.where(emask & (lane != i1) & (el2 == e2), lane, big), axis=-1, keepdims=True)
    t = jnp.exp(e2 - e1)
    den = 1.0 + t
    return jnp.where(lane == i1, g_w * (1.0 / den), 0.0) + jnp.where(lane == i2, g_w * (t / den), 0.0)


def _moe_kernel(x1_ref, mod_ref, g2_ref, wr_ref, br_ref, weg_ref, weu_ref, wed_ref, fg_ref,
                *out_refs, n_ctx_tiles, tiles_per_seq, final):
    row = _mod_row(pl.program_id(0), n_ctx_tiles, tiles_per_seq)
    x1 = x1_ref[...]
    h2 = _mod_norm(x1, g2_ref[...], mod_ref, row, 3, 4)
    comb = _route(_dot_hi(h2, wr_ref[...]) + br_ref[...])
    hb = h2.astype(BF16)
    gw = EXPERTS_PER_GROUP * EXPERT_FF
    acc = None
    for grp in range(N_GROUPS):
        cols = slice(grp * gw, (grp + 1) * gw)
        hg = _dot(hb, weg_ref[:, cols])
        hu = _dot(hb, weu_ref[:, cols])
        hid = hg * jax.nn.sigmoid(hg) * hu
        parts = []
        for e in range(EXPERTS_PER_GROUP):
            lane = N_GROUPS + grp * EXPERTS_PER_GROUP + e
            parts.append(hid[:, e * EXPERT_FF:(e + 1) * EXPERT_FF] * comb[:, lane:lane + 1])
        term = _dot(jnp.concatenate(parts, axis=1).astype(BF16), wed_ref[cols, :])
        acc = term if acc is None else acc + term
    g2 = mod_ref[pl.ds(row, 1), 5 * D_MODEL:6 * D_MODEL]
    x2 = x1 + g2 * acc
    out_refs[0][...] = x2
    if final:
        out_refs[1][...] = x2 * lax.rsqrt(jnp.mean(x2 * x2, axis=-1, keepdims=True) + RMS_EPS) * fg_ref[...]


def _moe(x1, mod, p, final_g, final):
    tm = TOK_TILE
    kern = functools.partial(_moe_kernel, n_ctx_tiles=N_CTX_TOK // tm, tiles_per_seq=DEC_SEQ // tm, final=final)
    tokd = pl.BlockSpec((tm, D_MODEL), lambda i: (i, 0))
    ff = N_EXPERTS * EXPERT_FF
    single = pl.Buffered(1)
    out = jax.ShapeDtypeStruct((N_TOK, D_MODEL), F32)
    return pl.pallas_call(
        kern,
        grid=(N_TOK // tm,),
        in_specs=[
            tokd,
            _const_spec((MOD_ROWS, N_MOD * D_MODEL)),
            _const_spec((1, D_MODEL)),
            _const_spec((D_MODEL, ROUTER_LANES)),
            _const_spec((1, ROUTER_LANES)),
            pl.BlockSpec((D_MODEL, ff), lambda i: (0, 0), pipeline_mode=single),
            pl.BlockSpec((D_MODEL, ff), lambda i: (0, 0), pipeline_mode=single),
            pl.BlockSpec((ff, D_MODEL), lambda i: (0, 0), pipeline_mode=single),
            _const_spec((1, D_MODEL)),
        ],
        out_specs=[tokd, tokd] if final else [tokd],
        out_shape=[out, out] if final else [out],
        compiler_params=_cparams(1, V7X_VMEM_LIMIT),
        name="moe_final" if final else "moe",
    )(x1, mod, p["norm2_g"], p["w_router"], p["b_router"], p["w_e_gate"], p["w_e_up"], p["w_e_down"], final_g)


def _block_diag2(w):
    z = jnp.zeros_like(w[0])
    return jnp.concatenate([jnp.concatenate([w[0], z], axis=1), jnp.concatenate([z, w[1]], axis=1)], axis=0)


def _layer_params(l, a):
    row = lambda v: v.reshape(1, -1)
    perm = jnp.concatenate([a["w_in"][l][:, 0:2048], a["w_in"][l][:, 2176:2688], a["w_in"][l][:, 2048:2176]], axis=1)
    w_router = jnp.zeros((D_MODEL, ROUTER_LANES), F32)
    w_router = w_router.at[:, 0:N_GROUPS].set(a["w_rg"][l]).at[:, N_GROUPS:N_GROUPS + N_EXPERTS].set(a["w_re"][l])
    b_router = jnp.zeros((1, ROUTER_LANES), F32)
    b_router = b_router.at[0, 0:N_GROUPS].set(a["b_rg"][l]).at[0, N_GROUPS:N_GROUPS + N_EXPERTS].set(a["b_re"][l])
    ff = N_EXPERTS * EXPERT_FF
    return {
        "norm1_g": row(a["norm1_g"][l]), "norm2_g": row(a["norm2_g"][l]),
        "w_in": perm.astype(BF16),
        "gmlp_ln_g": row(a["gmlp_ln_g"][l]),
        "gmlp_wcat": a["gmlp_ws"][l].transpose(1, 0, 2).reshape(CHUNK, GMLP_GROUPS * CHUNK).astype(BF16),
        "gmlp_bsx": jnp.repeat(a["gmlp_bs"][l].T, BRANCH_W // GMLP_GROUPS, axis=1),
        "conv_w": a["conv_w"][l], "conv_b": row(a["conv_b"][l]),
        "conv_ln_g": row(a["conv_ln_g"][l]), "conv_ln_b": row(a["conv_ln_b"][l]),
        "rwkv_w0": row(a["rwkv_w0"][l]), "rwkv_w2blk": _block_diag2(a["rwkv_w2"][l]),
        "rwkv_a0": row(a["rwkv_a0"][l]), "rwkv_a2blk": _block_diag2(a["rwkv_a2"][l]),
        "rwkv_g2": a["rwkv_g2"][l], "rwkv_kk": row(a["rwkv_kk"][l]), "rwkv_ka": row(a["rwkv_ka"][l]),
        "rwkv_rk": row(a["rwkv_rk"][l]),
        "rwkv_lnx_g": row(a["rwkv_lnx_g"][l]), "rwkv_lnx_b": row(a["rwkv_lnx_b"][l]),
        "attn_sink": a["attn_sink"][l],
        "w_gate": a["w_gate"][l].astype(BF16), "b_gate": row(a["b_gate"][l]),
        "w_branch": a["w_branch"][l].astype(BF16), "w_out": a["w_out"][l].astype(BF16),
        "w_router": w_router, "b_router": b_router,
        "w_e_gate": a["w_e_gate"][l].transpose(1, 0, 2).reshape(D_MODEL, ff).astype(BF16),
        "w_e_up": a["w_e_up"][l].transpose(1, 0, 2).reshape(D_MODEL, ff).astype(BF16),
        "w_e_down": a["w_e_down"][l].reshape(ff, D_MODEL).astype(BF16),
    }


def _rope_tables():
    half = HEAD_DIM // 4
    inv = ROPE_BASE ** (-jnp.arange(half, dtype=F32) / half)
    t = jnp.arange(DEC_SEQ)
    row = (t // GRID_W).astype(F32)[:, None] * inv[None, :]
    col = (t % GRID_W).astype(F32)[:, None] * inv[None, :]
    cos_h = jnp.concatenate([jnp.cos(row), jnp.cos(row), jnp.cos(col), jnp.cos(col)], axis=1)
    sin_h = jnp.concatenate([-jnp.sin(row), jnp.sin(row), -jnp.sin(col), jnp.sin(col)], axis=1)
    reps = V7X_LANES // HEAD_DIM
    return jnp.tile(cos_h, (1, reps)), jnp.tile(sin_h, (1, reps))


def _ctx_chain_layout(x):
    return x.reshape(BATCH, SEQ, RWKV_HEADS, HEAD_DIM).transpose(1, 3, 0, 2).reshape(SEQ, HEAD_DIM, BATCH * RWKV_HEADS)


def _ctx_token_layout(y):
    return y.reshape(SEQ, HEAD_DIM, BATCH, RWKV_HEADS).transpose(2, 0, 3, 1).reshape(N_CTX_TOK, BRANCH_W)


LAT_VL = V7X_LANES // (N_DIR * DEC_BATCH * RWKV_HEADS)
LAT_NV = HEAD_DIM // LAT_VL


def _lat_both_dirs(x_fwd, x_bwd):
    return jnp.stack([x_fwd, x_bwd[:, ::-1]], axis=0)


def _lat_chain_layout_k(x2):
    y = x2.transpose(2, 4, 0, 1, 3).reshape(DEC_SEQ, HEAD_DIM, 1, N_DIR * DEC_BATCH * RWKV_HEADS)
    return jnp.broadcast_to(y, (DEC_SEQ, HEAD_DIM, LAT_VL, y.shape[-1])).reshape(DEC_SEQ, HEAD_DIM, V7X_LANES)


def _lat_chain_layout_v(x2):
    y = x2.reshape(N_DIR, DEC_BATCH, DEC_SEQ, RWKV_HEADS, LAT_VL, LAT_NV)
    return y.transpose(2, 5, 4, 0, 1, 3).reshape(DEC_SEQ, LAT_NV, V7X_LANES)


def _lat_token_layout(y):
    y = y.reshape(DEC_SEQ, LAT_NV, LAT_VL, N_DIR, DEC_BATCH, RWKV_HEADS).transpose(3, 4, 0, 5, 2, 1)
    y = jnp.stack([y[0], y[1][:, ::-1]], axis=0)
    return y.reshape(N_DIR, N_LAT_TOK, BRANCH_W)


def _rwkv_mixer(z, p, ones, state_l):
    r, v, kk, w, kd, b, g, bonus = _rwkv_prep(z, p, ones)
    nc = N_CTX_TOK
    y_c, s_c = _rwkv_scan(
        _ctx_chain_layout(r[:nc]), _ctx_chain_layout(kk[:nc]), _ctx_chain_layout(v[:nc]),
        jnp.stack([_ctx_chain_layout(w[d, :nc]) for d in range(N_DIR)]),
        jnp.stack([_ctx_chain_layout(kd[d, :nc]) for d in range(N_DIR)]),
        jnp.stack([_ctx_chain_layout(b[d, :nc]) for d in range(N_DIR)]), None)
    yf_c, yb_c = _ctx_token_layout(y_c[0]), _ctx_token_layout(y_c[1])
    s_fin = s_c.reshape(N_DIR, HEAD_DIM, HEAD_DIM, BATCH, RWKV_HEADS).transpose(3, 0, 4, 2, 1)
    heads = lambda x: x.reshape(DEC_BATCH, DEC_SEQ, RWKV_HEADS, HEAD_DIM)
    same = lambda x: _lat_both_dirs(heads(x[nc:]), heads(x[nc:]))
    per_dir = lambda x: _lat_both_dirs(heads(x[0, nc:]), heads(x[1, nc:]))
    s0 = state_l.reshape(DEC_BATCH, N_DIR, RWKV_HEADS, LAT_VL, LAT_NV, HEAD_DIM)
    s0 = s0.transpose(5, 4, 3, 1, 0, 2).reshape(1, HEAD_DIM, LAT_NV, V7X_LANES)
    y_l, _ = _rwkv_scan(
        _lat_chain_layout_k(same(r)), _lat_chain_layout_k(same(kk)), _lat_chain_layout_v(same(v)),
        _lat_chain_layout_k(per_dir(w))[None], _lat_chain_layout_k(per_dir(kd))[None],
        _lat_chain_layout_k(per_dir(b))[None], s0)
    y_l = _lat_token_layout(y_l[0])
    yf = jnp.concatenate([yf_c, y_l[0]], axis=0)
    yb = jnp.concatenate([yb_c, y_l[1]], axis=0)
    return yf, yb, g, bonus, s_fin


def kernel(x_prompt, x_sample, cache_k, cache_v, state_rwkv, c, c_ctx, norm1_g, norm2_g, final_norm_g, w_mod, b_mod, w_in, gmlp_ln_g, gmlp_ws, gmlp_bs, conv_w, conv_b, conv_ln_g, conv_ln_b, rwkv_w0, rwkv_w2, rwkv_a0, rwkv_a2, rwkv_g2, rwkv_kk, rwkv_ka, rwkv_rk, rwkv_lnx_g, rwkv_lnx_b, attn_sink, w_gate, b_gate, w_branch, w_out, w_rg, b_rg, w_re, b_re, w_e_gate, w_e_up, w_e_down):
    arrays = dict(norm1_g=norm1_g, norm2_g=norm2_g, w_in=w_in, gmlp_ln_g=gmlp_ln_g, gmlp_ws=gmlp_ws,
                  gmlp_bs=gmlp_bs, conv_w=conv_w, conv_b=conv_b, conv_ln_g=conv_ln_g, conv_ln_b=conv_ln_b,
                  rwkv_w0=rwkv_w0, rwkv_w2=rwkv_w2, rwkv_a0=rwkv_a0, rwkv_a2=rwkv_a2, rwkv_g2=rwkv_g2,
                  rwkv_kk=rwkv_kk, rwkv_ka=rwkv_ka, rwkv_rk=rwkv_rk, rwkv_lnx_g=rwkv_lnx_g,
                  rwkv_lnx_b=rwkv_lnx_b, attn_sink=attn_sink, w_gate=w_gate, b_gate=b_gate,
                  w_branch=w_branch, w_out=w_out, w_rg=w_rg, b_rg=b_rg, w_re=w_re, b_re=b_re,
                  w_e_gate=w_e_gate, w_e_up=w_e_up, w_e_down=w_e_down)
    x = jnp.concatenate([x_prompt.reshape(N_CTX_TOK, D_MODEL), x_sample.reshape(N_LAT_TOK, D_MODEL)], axis=0)
    cvec = jnp.concatenate([c_ctx[None], c, jnp.zeros((MOD_ROWS - 1 - DEC_BATCH, D_MODEL), F32)], axis=0)
    mod_all = _modulation(cvec, w_mod, b_mod)
    ones = _head_ones()
    cos_t, sin_t = _rope_tables()
    final_g = final_norm_g.reshape(1, D_MODEL)
    kvw = KV_HEADS * HEAD_DIM
    ks_out, vs_out, ss_out = [], [], []
    y = None
    for l in range(DEPTH):
        p = _layer_params(l, arrays)
        mod = mod_all[l]
        z = _inproj(x, mod, p["norm1_g"], p["w_in"])
        oa_c, ob_c = _local_mix(z, SEQ, BATCH, 0, p)
        oa_l, ob_l = _local_mix(z, DEC_SEQ, DEC_BATCH, N_CTX_TOK // DEC_SEQ, p)
        yf, yb, g, bonus, s_fin = _rwkv_mixer(z, p, ones, state_rwkv[:, l])
        od_c = _ctx_attn(z, p["attn_sink"])
        od_l = _lat_attn(z, p["attn_sink"], cache_k[:, l].reshape(DEC_BATCH, PAST_LEN, kvw),
                         cache_v[:, l].reshape(DEC_BATCH, PAST_LEN, kvw), cos_t, sin_t)
        x1 = _mix(x, mod, p, jnp.concatenate([oa_c, oa_l], axis=0), jnp.concatenate([ob_c, ob_l], axis=0),
                  jnp.concatenate([od_c, od_l], axis=0), yf, yb, g, bonus, ones)
        outs = _moe(x1, mod, p, final_g, l == DEPTH - 1)
        x = outs[0]
        if l == DEPTH - 1:
            y = outs[1]
        k_off = Z_D + ATT_HEADS * HEAD_DIM
        ks_out.append(z[:N_CTX_TOK, k_off:k_off + kvw].reshape(BATCH, SEQ, KV_HEADS, HEAD_DIM))
        vs_out.append(z[:N_CTX_TOK, k_off + kvw:k_off + 2 * kvw].reshape(BATCH, SEQ, KV_HEADS, HEAD_DIM))
        ss_out.append(s_fin)
    y_prompt = y[:N_CTX_TOK].reshape(BATCH, SEQ, D_MODEL)
    y_sample = y[N_CTX_TOK:].reshape(DEC_BATCH, DEC_SEQ, D_MODEL)
    return (y_prompt, y_sample, jnp.stack(ks_out, axis=1), jnp.stack(vs_out, axis=1), jnp.stack(ss_out, axis=1))
```

```python
import functools
import math
from typing import NamedTuple

import jax
import jax.numpy as jnp
from jax import lax
from jax.experimental import pallas as pl
from jax.experimental.pallas import tpu as pltpu

D_MODEL = 1024
BATCH = 32
SEQ = 256
DEPTH = 2
DEC_BATCH = 4
DEC_SEQ = 1024
PAST_LEN = 256
GRID_W = 64
HEAD_DIM = 64
BRANCH_W = 256
N_BRANCH = 4
CHUNK = 128
GMLP_GROUPS = 4
CONV_W = 31
RWKV_HEADS = 4
N_DIR = 2
DECAY_LORA = 64
ICL_LORA = 64
GATE_LORA = 128
ATT_HEADS = 4
KV_HEADS = 2
Q_PER_KV = ATT_HEADS // KV_HEADS
WINDOW = 128
BLOCK = 128
ROPE_BASE = 10000.0
ATT_SCALE = HEAD_DIM ** -0.5
N_GROUPS = 4
EXPERTS_PER_GROUP = 4
N_EXPERTS = N_GROUPS * EXPERTS_PER_GROUP
EXPERT_FF = 256
N_MOD = 6
RMS_EPS = 1e-6
LN_EPS = 1e-5
GN_EPS = 64e-5
IN_COLS = 2688

V7X_LANES = 128
V7X_SUBLANES = 8
V7X_VMEM_LIMIT = 56 * 1024 * 1024

MOD_ROWS = 8
TOK_TILE = 512
SCAN_TB = 32
CONV_PAD = 16

Z_AB, Z_C, Z_D, Z_G = 0, 1024, 2048, 2560

LAT_Q_R, LAT_Q_KK, LAT_Q_W, LAT_Q_KD, LAT_Q_B = 0, 1, 2, 4, 6
LAT_NQ = 8
LAT_VL = V7X_LANES // (DEC_BATCH * RWKV_HEADS)

F32 = jnp.float32
BF16 = jnp.bfloat16
HIGHEST = lax.Precision.HIGHEST


class _Path(NamedTuple):
    nb: int
    seq: int
    mod_row0: int
    mod_rows: int
    tt: int

    @property
    def n_tok(self):
        return self.nb * self.seq


CTX = _Path(BATCH, SEQ, 0, 1, TOK_TILE // BATCH)
LAT = _Path(DEC_BATCH, DEC_SEQ, 1, DEC_BATCH, TOK_TILE // DEC_BATCH)


def _cparams(n_axes, vmem=V7X_VMEM_LIMIT):
    return pltpu.CompilerParams(dimension_semantics=("arbitrary",) * n_axes, vmem_limit_bytes=vmem)


def _const_spec(shape):
    nd = len(shape)
    return pl.BlockSpec(shape, lambda *_: (0,) * nd)


def _dot(a, b):
    return jnp.dot(a, b, preferred_element_type=F32)


def _dot_hi(a, b):
    return jnp.dot(a, b, preferred_element_type=F32, precision=HIGHEST)


def _head_ones():
    r = lax.broadcasted_iota(jnp.int32, (BRANCH_W, BRANCH_W), 0) // HEAD_DIM
    c = lax.broadcasted_iota(jnp.int32, (BRANCH_W, BRANCH_W), 1) // HEAD_DIM
    return (r == c).astype(F32)


def _mod_row(tile, path):
    if path.mod_rows == 1:
        return path.mod_row0
    return path.mod_row0 + tile // (path.seq // TOK_TILE)


def _mod_slice(mod_ref, row, j):
    return mod_ref[pl.ds(row, 1), j * D_MODEL:(j + 1) * D_MODEL]


def _rms(x, g):
    return x * lax.rsqrt(jnp.mean(x * x, axis=-1, keepdims=True) + RMS_EPS) * g


def _layernorm(x, g, b=None, eps=LN_EPS):
    mu = jnp.mean(x, axis=-1, keepdims=True)
    d = x - mu
    var = jnp.mean(d * d, axis=-1, keepdims=True)
    y = d * lax.rsqrt(var + eps) * g
    return y if b is None else y + b


def _mod_kernel(c_ref, w_ref, b_ref, o_ref):
    c = c_ref[...]
    a = c * jax.nn.sigmoid(c)
    o_ref[0] = _dot_hi(a, w_ref[0]) + b_ref[0]


def _modulation(cvec, w_mod, b_mod):
    return pl.pallas_call(
        _mod_kernel,
        grid=(DEPTH, N_MOD),
        in_specs=[
            pl.BlockSpec((MOD_ROWS, D_MODEL), lambda l, j: (0, 0)),
            pl.BlockSpec((1, D_MODEL, D_MODEL), lambda l, j: (l, 0, j)),
            pl.BlockSpec((1, 1, D_MODEL), lambda l, j: (l, 0, j)),
        ],
        out_specs=pl.BlockSpec((1, MOD_ROWS, D_MODEL), lambda l, j: (l, 0, j)),
        out_shape=jax.ShapeDtypeStruct((DEPTH, MOD_ROWS, N_MOD * D_MODEL), F32),
        compiler_params=_cparams(2),
        name="modulation",
    )(cvec, w_mod, b_mod.reshape(DEPTH, 1, N_MOD * D_MODEL))


def _inproj_kernel(x_ref, mod_ref, g_ref, w_ref, z_ref, *, path):
    row = _mod_row(pl.program_id(0), path)
    h = _rms(x_ref[...], g_ref[...]) * (1.0 + _mod_slice(mod_ref, row, 1)) + _mod_slice(mod_ref, row, 0)
    z_ref[...] = _dot(h.astype(BF16), w_ref[...])


def _inproj(x, mod, g1, w_in_b, path):
    tm = TOK_TILE
    return pl.pallas_call(
        functools.partial(_inproj_kernel, path=path),
        grid=(path.n_tok // tm,),
        in_specs=[
            pl.BlockSpec((tm, D_MODEL), lambda i: (i, 0)),
            _const_spec((MOD_ROWS, N_MOD * D_MODEL)),
            _const_spec((1, D_MODEL)),
            _const_spec((D_MODEL, IN_COLS)),
        ],
        out_specs=pl.BlockSpec((tm, IN_COLS), lambda i: (i, 0)),
        out_shape=jax.ShapeDtypeStruct((path.n_tok, IN_COLS), F32),
        compiler_params=_cparams(1),
        name=f"inproj_{path.seq}",
    )(x, mod, g1, w_in_b)


def _local_mix_kernel(z_ref, lng_ref, wcat_ref, bsx_ref, cw_ref, cb_ref, clg_ref, clb_ref,
                      oa_ref, ob_ref, ypad, *, seq_len):
    lane_group = lax.broadcasted_iota(jnp.int32, (CHUNK, BRANCH_W), 1) // (BRANCH_W // GMLP_GROUPS)
    for c in range(seq_len // CHUNK):
        rows = pl.ds(c * CHUNK, CHUNK)
        u = jax.nn.gelu(z_ref[rows, 0:BRANCH_W], approximate=True)
        v = jax.nn.gelu(z_ref[rows, BRANCH_W:2 * BRANCH_W], approximate=True)
        vn = _layernorm(v, lng_ref[...])
        vblk = jnp.concatenate(
            [jnp.where(lane_group == g, vn, 0.0) for g in range(GMLP_GROUPS)], axis=0).astype(BF16)
        mixed = _dot(wcat_ref[...], vblk) + bsx_ref[...]
        oa_ref[rows, :] = (u * mixed).astype(BF16)

    zeros = jnp.zeros((CONV_PAD, BRANCH_W), F32)
    ypad[0:CONV_PAD, :] = zeros
    ypad[CONV_PAD + seq_len:2 * CONV_PAD + seq_len, :] = zeros
    ypad[CONV_PAD:CONV_PAD + seq_len, :] = (
        z_ref[:, 2 * BRANCH_W:3 * BRANCH_W] * jax.nn.sigmoid(z_ref[:, 3 * BRANCH_W:4 * BRANCH_W]))
    base = CONV_PAD - CONV_W // 2
    for c in range(seq_len // CHUNK):
        acc = jnp.zeros((CHUNK, BRANCH_W), F32)
        for j in range(CONV_W):
            acc = acc + cw_ref[j:j + 1, :] * ypad[pl.ds(c * CHUNK + base + j, CHUNK), :]
        y = _layernorm(acc + cb_ref[...], clg_ref[...], clb_ref[...])
        ob_ref[pl.ds(c * CHUNK, CHUNK), :] = (y * jax.nn.sigmoid(y)).astype(BF16)


def _local_mix(z, p, path):
    out = jax.ShapeDtypeStruct((path.n_tok, BRANCH_W), BF16)
    return pl.pallas_call(
        functools.partial(_local_mix_kernel, seq_len=path.seq),
        grid=(path.nb,),
        in_specs=[
            pl.BlockSpec((path.seq, 4 * BRANCH_W), lambda i: (i, Z_AB // (4 * BRANCH_W))),
            _const_spec((1, BRANCH_W)),
            _const_spec((CHUNK, GMLP_GROUPS * CHUNK)),
            _const_spec((CHUNK, BRANCH_W)),
            _const_spec((CONV_W, BRANCH_W)),
            _const_spec((1, BRANCH_W)),
            _const_spec((1, BRANCH_W)),
            _const_spec((1, BRANCH_W)),
        ],
        out_specs=[pl.BlockSpec((path.seq, BRANCH_W), lambda i: (i, 0))] * 2,
        out_shape=[out, out],
        scratch_shapes=[pltpu.VMEM((path.seq + 2 * CONV_PAD, BRANCH_W), F32)],
        compiler_params=_cparams(1),
        name=f"local_mix_{path.seq}",
    )(z, p["gmlp_ln_g"], p["gmlp_wcat"], p["gmlp_bsx"], p["conv_w"], p["conv_b"],
      p["conv_ln_g"], p["conv_ln_b"])


def _rwkv_quantities(zc, gd, w0_ref, w2_ref, a0_ref, a2_ref, g2_ref, kkp_ref, ka_ref, rk_ref, ones):
    W = BRANCH_W
    r = zc[:, 0:W]
    k = zc[:, W:2 * W]
    v = zc[:, 2 * W:3 * W]
    wd = zc[:, 3 * W:3 * W + N_DIR * DECAY_LORA]
    ad = zc[:, 3 * W + N_DIR * DECAY_LORA:4 * W]
    w_raw = _dot_hi(jnp.tanh(wd), w2_ref[...]) + w0_ref[...]
    decay = jnp.exp(-math.exp(-0.5) * jax.nn.sigmoid(w_raw))
    a = jax.nn.sigmoid(_dot_hi(ad, a2_ref[...]) + a0_ref[...])
    g = _dot_hi(jax.nn.sigmoid(gd), g2_ref[...])
    kkr = k * kkp_ref[...]
    kk = kkr / jnp.maximum(jnp.sqrt(_dot_hi(kkr * kkr, ones)), 1e-12)
    ka = ka_ref[...]
    w, kd, b = [], [], []
    for d in range(N_DIR):
        a_d = a[:, d * W:(d + 1) * W]
        w.append(decay[:, d * W:(d + 1) * W])
        kd.append(k * (1.0 + (a_d - 1.0) * ka))
        b.append(kk * a_d)
    bonus = _dot_hi(r * (kd[0] + kd[1]) * rk_ref[...], ones) * v
    return g, bonus, r, kk, v, w, kd, b


def _prep_ctx_kernel(zc_ref, zg_ref, w0_ref, w2_ref, a0_ref, a2_ref, g2_ref, kkp_ref, ka_ref, rk_ref,
                     ones_ref, g_o, bonus_o, r_o, kk_o, v_o, w_o, kd_o, b_o, slab):
    nb, tt = CTX.nb, CTX.tt
    rows = nb * tt
    zc = zc_ref[...].reshape(rows, 4 * BRANCH_W)
    gd = zg_ref[...].reshape(rows, GATE_LORA)
    g, bonus, r, kk, v, w, kd, b = _rwkv_quantities(
        zc, gd, w0_ref, w2_ref, a0_ref, a2_ref, g2_ref, kkp_ref, ka_ref, rk_ref, ones_ref[...])
    g_o[...] = g.reshape(nb, tt, BRANCH_W)
    bonus_o[...] = bonus.reshape(nb, tt, BRANCH_W)
    n_slab = BRANCH_W // V7X_LANES
    lanes_per_head = V7X_LANES // RWKV_HEADS

    def to_chains(q, put):
        for s in range(n_slab):
            slab[s] = q[:, s * V7X_LANES:(s + 1) * V7X_LANES]
        by_t = jnp.concatenate(
            [jnp.concatenate([slab[s, pl.ds(t, nb, stride=tt), :] for s in range(n_slab)], axis=1)
             for t in range(tt)], axis=0)
        tr = by_t.T
        for t in range(tt):
            put(t, jnp.concatenate(
                [tr[h * HEAD_DIM:(h + 1) * HEAD_DIM, t * nb:(t + 1) * nb] for h in range(RWKV_HEADS)], axis=1))
    assert lanes_per_head == nb

    def put_into(ref, *lead):
        def put(t, tile):
            ref[(*lead, t)] = tile
        return put

    to_chains(r, put_into(r_o))
    to_chains(kk, put_into(kk_o))
    to_chains(v, put_into(v_o))
    for d in range(N_DIR):
        to_chains(w[d], put_into(w_o, d))
        to_chains(kd[d], put_into(kd_o, d))
        to_chains(b[d], put_into(b_o, d))


def _prep_lat_kernel(zc_ref, zg_ref, w0_ref, w2_ref, a0_ref, a2_ref, g2_ref, kkp_ref, ka_ref, rk_ref,
                     ones_ref, g_o, bonus_o, v_o, kq_o, gk, gv):
    nb, tt = LAT.nb, LAT.tt
    rows = nb * tt
    qi = pl.program_id(1)
    chains = nb * RWKV_HEADS

    @pl.when(qi == 0)
    def _():
        zc = zc_ref[...].reshape(rows, 4 * BRANCH_W)
        gd = zg_ref[...].reshape(rows, GATE_LORA)
        g, bonus, r, kk, v, w, kd, b = _rwkv_quantities(
            zc, gd, w0_ref, w2_ref, a0_ref, a2_ref, g2_ref, kkp_ref, ka_ref, rk_ref, ones_ref[...])
        g_o[...] = g.reshape(nb, tt, BRANCH_W)
        bonus_o[...] = bonus.reshape(nb, tt, BRANCH_W)
        k_indexed = {LAT_Q_R: r, LAT_Q_KK: kk, LAT_Q_W: w[0], LAT_Q_W + 1: w[1],
                     LAT_Q_KD: kd[0], LAT_Q_KD + 1: kd[1], LAT_Q_B: b[0], LAT_Q_B + 1: b[1]}
        for bi in range(nb):
            for q_idx, q in k_indexed.items():
                gk[q_idx, bi * BRANCH_W:(bi + 1) * BRANCH_W, :] = q[bi * tt:(bi + 1) * tt, :].T
            gv[bi * BRANCH_W:(bi + 1) * BRANCH_W, :] = v[bi * tt:(bi + 1) * tt, :].T
        for vp in range(V7X_SUBLANES):
            a = jnp.concatenate(
                [gv[pl.ds(vp + V7X_SUBLANES * vl, chains, stride=HEAD_DIM), :] for vl in range(LAT_VL)], axis=0)
            v_o[pl.ds(vp, tt, stride=V7X_SUBLANES), :] = a.T

    def body(k, carry):
        a = gk[qi, pl.ds(k, chains, stride=HEAD_DIM), :]
        kq_o[k] = jnp.concatenate([a] * LAT_VL, axis=0).T
        return carry
    lax.fori_loop(0, HEAD_DIM, body, 0)


def _prep_param_specs():
    return [
        _const_spec((1, N_DIR * BRANCH_W)),
        _const_spec((N_DIR * DECAY_LORA, N_DIR * BRANCH_W)),
        _const_spec((1, N_DIR * BRANCH_W)),
        _const_spec((N_DIR * ICL_LORA, N_DIR * BRANCH_W)),
        _const_spec((GATE_LORA, BRANCH_W)),
        _const_spec((1, BRANCH_W)),
        _const_spec((1, BRANCH_W)),
        _const_spec((1, BRANCH_W)),
        _const_spec((BRANCH_W, BRANCH_W)),
    ]


def _prep_params(p, ones):
    return (p["rwkv_w0"], p["rwkv_w2blk"], p["rwkv_a0"], p["rwkv_a2blk"], p["rwkv_g2"],
            p["rwkv_kk"], p["rwkv_ka"], p["rwkv_rk"], ones)


def _prep_ctx(z, p, ones):
    nb, tt, seq = CTX.nb, CTX.tt, CTX.seq
    z3 = z.reshape(nb, seq, IN_COLS)
    tok = jax.ShapeDtypeStruct((nb, seq, BRANCH_W), F32)
    tok_spec = pl.BlockSpec((nb, tt, BRANCH_W), lambda i: (0, i, 0))
    ch = jax.ShapeDtypeStruct((seq, HEAD_DIM, V7X_LANES), F32)
    ch2 = jax.ShapeDtypeStruct((N_DIR, seq, HEAD_DIM, V7X_LANES), F32)
    ch_spec = pl.BlockSpec((tt, HEAD_DIM, V7X_LANES), lambda i: (i, 0, 0))
    ch2_spec = pl.BlockSpec((N_DIR, tt, HEAD_DIM, V7X_LANES), lambda i: (0, i, 0, 0))
    return pl.pallas_call(
        _prep_ctx_kernel,
        grid=(seq // tt,),
        in_specs=[
            pl.BlockSpec((nb, tt, 4 * BRANCH_W), lambda i: (0, i, Z_C // (4 * BRANCH_W))),
            pl.BlockSpec((nb, tt, GATE_LORA), lambda i: (0, i, Z_G // GATE_LORA)),
        ] + _prep_param_specs(),
        out_specs=[tok_spec, tok_spec, ch_spec, ch_spec, ch_spec, ch2_spec, ch2_spec, ch2_spec],
        out_shape=[tok, tok, ch, ch, ch, ch2, ch2, ch2],
        scratch_shapes=[pltpu.VMEM((BRANCH_W // V7X_LANES, nb * tt, V7X_LANES), F32)],
        compiler_params=_cparams(1),
        name="rwkv_prep_ctx",
    )(z3, z3, *_prep_params(p, ones))


def _prep_lat(z, p, ones):
    nb, tt, seq = LAT.nb, LAT.tt, LAT.seq
    z3 = z.reshape(nb, seq, IN_COLS)
    tok = jax.ShapeDtypeStruct((nb, seq, BRANCH_W), F32)
    tok_spec = pl.BlockSpec((nb, tt, BRANCH_W), lambda i, q: (0, i, 0))
    return pl.pallas_call(
        _prep_lat_kernel,
        grid=(seq // tt, LAT_NQ),
        in_specs=[
            pl.BlockSpec((nb, tt, 4 * BRANCH_W), lambda i, q: (0, i, Z_C // (4 * BRANCH_W))),
            pl.BlockSpec((nb, tt, GATE_LORA), lambda i, q: (0, i, Z_G // GATE_LORA)),
        ] + _prep_param_specs(),
        out_specs=[
            tok_spec, tok_spec,
            pl.BlockSpec((tt * V7X_SUBLANES, V7X_LANES), lambda i, q: (i, 0)),
            pl.BlockSpec((None, HEAD_DIM, tt, V7X_LANES), lambda i, q: (q, 0, i, 0)),
        ],
        out_shape=[tok, tok,
                   jax.ShapeDtypeStruct((seq * V7X_SUBLANES, V7X_LANES), F32),
                   jax.ShapeDtypeStruct((LAT_NQ, HEAD_DIM, seq, V7X_LANES), F32)],
        scratch_shapes=[pltpu.VMEM((LAT_NQ, nb * BRANCH_W, V7X_LANES), F32),
                        pltpu.VMEM((nb * BRANCH_W, V7X_LANES), F32)],
        compiler_params=_cparams(2),
        name="rwkv_prep_lat",
    )(z3, z3, *_prep_params(p, ones))


def _scan_kernel(*refs, n_v8, tb, has_init, key_major):
    if has_init:
        r_ref, kk_ref, v_ref, w_ref, kd_ref, b_ref, s0_ref, y_ref, sfin_ref, S = refs
    else:
        r_ref, kk_ref, v_ref, w_ref, kd_ref, b_ref, y_ref, sfin_ref, S = refs
    d = pl.program_id(0)
    i = pl.program_id(1)
    n8 = V7X_SUBLANES

    @pl.when(i == 0)
    def _():
        if has_init:
            S[...] = s0_ref[...]
        else:
            S[...] = jnp.zeros_like(S)

    def row8(ref, t, k):
        row = ref[k, pl.ds(t, 1), :] if key_major else ref[t, pl.ds(k, 1), :]
        return jnp.broadcast_to(row, (n8, V7X_LANES))

    def vrows(t, vo):
        if key_major:
            return pl.ds(pl.multiple_of(t * n8, n8), n8), slice(None)
        return t, slice(vo * n8, (vo + 1) * n8), slice(None)

    def step(s, carry):
        t = s + d * (tb - 1 - 2 * s)

        def pass1(k, sa):
            kkb = row8(kk_ref, t, k)
            return tuple(sa[vo] + S[k, vo * n8:(vo + 1) * n8, :] * kkb for vo in range(n_v8))

        zero = tuple(jnp.zeros((n8, V7X_LANES), F32) for _ in range(n_v8))
        sa = lax.fori_loop(0, HEAD_DIM, pass1, zero, unroll=8)
        nsa = tuple(-x for x in sa)
        vt = tuple(v_ref[vrows(t, vo)] for vo in range(n_v8))

        def pass2(k, y):
            wb = row8(w_ref, t, k)
            bb = row8(b_ref, t, k)
            kb = row8(kd_ref, t, k)
            rb = row8(r_ref, t, k)
            out = []
            for vo in range(n_v8):
                sl = slice(vo * n8, (vo + 1) * n8)
                sn = S[k, sl, :] * wb + (nsa[vo] * bb + vt[vo] * kb)
                S[k, sl, :] = sn
                out.append(y[vo] + sn * rb)
            return tuple(out)

        y = lax.fori_loop(0, HEAD_DIM, pass2, zero, unroll=8)
        for vo in range(n_v8):
            y_ref[vrows(t, vo)] = y[vo]
        return carry

    lax.fori_loop(0, tb, step, 0)

    @pl.when(i == pl.num_programs(1) - 1)
    def _():
        sfin_ref[...] = S[...]


def _scan_call(kern, seq_len, nv, in_specs, y_spec, y_shape, args, name):
    nt = seq_len // SCAN_TB
    state = pl.BlockSpec((None, HEAD_DIM, nv, V7X_LANES), lambda d, i: (d, 0, 0, 0))
    if len(args) > len(in_specs):
        in_specs = in_specs + [state]
    return pl.pallas_call(
        kern,
        grid=(N_DIR, nt),
        in_specs=in_specs,
        out_specs=[y_spec, state],
        out_shape=[y_shape, jax.ShapeDtypeStruct((N_DIR, HEAD_DIM, nv, V7X_LANES), F32)],
        scratch_shapes=[pltpu.VMEM((HEAD_DIM, nv, V7X_LANES), F32)],
        compiler_params=_cparams(2),
        name=name,
    )(*args)


def _tmap(nt):
    return lambda d, i: i + d * (nt - 1 - 2 * i)


def _scan_ctx(r_s, kk_s, v_s, w_s, kd_s, b_s):
    seq, tb = CTX.seq, SCAN_TB
    tm = _tmap(seq // tb)
    shared = pl.BlockSpec((tb, HEAD_DIM, V7X_LANES), lambda d, i: (tm(d, i), 0, 0))
    per_dir = pl.BlockSpec((None, tb, HEAD_DIM, V7X_LANES), lambda d, i: (d, tm(d, i), 0, 0))
    kern = functools.partial(_scan_kernel, n_v8=HEAD_DIM // V7X_SUBLANES, tb=tb, has_init=False, key_major=False)
    return _scan_call(kern, seq, HEAD_DIM, [shared, shared, shared, per_dir, per_dir, per_dir], per_dir,
                      jax.ShapeDtypeStruct((N_DIR, seq, HEAD_DIM, V7X_LANES), F32),
                      (r_s, kk_s, v_s, w_s, kd_s, b_s), "rwkv_scan_ctx")


def _scan_lat(kq, v_s, s0):
    seq, tb = LAT.seq, SCAN_TB
    tm = _tmap(seq // tb)
    n8 = V7X_SUBLANES

    def plane(q0, per_dir):
        return pl.BlockSpec((None, HEAD_DIM, tb, V7X_LANES),
                            lambda d, i: (q0 + (d if per_dir else 0), 0, tm(d, i), 0))
    v_spec = pl.BlockSpec((tb * n8, V7X_LANES), lambda d, i: (tm(d, i), 0))
    y_spec = pl.BlockSpec((None, tb * n8, V7X_LANES), lambda d, i: (d, tm(d, i), 0))
    kern = functools.partial(_scan_kernel, n_v8=1, tb=tb, has_init=True, key_major=True)
    in_specs = [plane(LAT_Q_R, False), plane(LAT_Q_KK, False), v_spec,
                plane(LAT_Q_W, True), plane(LAT_Q_KD, True), plane(LAT_Q_B, True)]
    return _scan_call(kern, seq, n8, in_specs, y_spec,
                      jax.ShapeDtypeStruct((N_DIR, seq * n8, V7X_LANES), F32),
                      (kq, kq, v_s, kq, kq, kq, s0), "rwkv_scan_lat")


def _softmax_pv(scores, vals, sink_col):
    m = sink_col
    for s in scores:
        m = jnp.maximum(m, jnp.max(s, axis=-1, keepdims=True))
    den = jnp.exp(sink_col - m)
    out = None
    for s, vx in zip(scores, vals):
        p = jnp.exp(s - m)
        den = den + jnp.sum(p, axis=-1, keepdims=True)
        o = _dot(p.astype(BF16), vx)
        out = o if out is None else out + o
    return out / den


def _sink_col(sink_ref, kv, n_rows):
    row = lax.broadcasted_iota(jnp.int32, (Q_PER_KV * n_rows, 1), 0)
    col = jnp.full((Q_PER_KV * n_rows, 1), sink_ref[kv * Q_PER_KV], F32)
    for g in range(1, Q_PER_KV):
        col = jnp.where(row >= g * n_rows, sink_ref[kv * Q_PER_KV + g], col)
    return col


def _qk(q2, kh):
    return lax.dot_general(q2, kh, (((1,), (1,)), ((), ())), preferred_element_type=F32) * ATT_SCALE


def _ctx_attn_kernel(sink_ref, z_ref, o_ref):
    hd = HEAD_DIM
    q = z_ref[:, 0:ATT_HEADS * hd]
    outs = []
    for kv in range(KV_HEADS):
        kh = z_ref[:, ATT_HEADS * hd + kv * hd:ATT_HEADS * hd + (kv + 1) * hd].astype(BF16)
        vh = z_ref[:, (ATT_HEADS + KV_HEADS) * hd + kv * hd:(ATT_HEADS + KV_HEADS) * hd + (kv + 1) * hd].astype(BF16)
        q2 = jnp.concatenate(
            [q[:, (kv * Q_PER_KV + g) * hd:(kv * Q_PER_KV + g + 1) * hd] for g in range(Q_PER_KV)],
            axis=0).astype(BF16)
        o2 = _softmax_pv([_qk(q2, kh)], [vh], _sink_col(sink_ref, kv, SEQ))
        outs += [o2[g * SEQ:(g + 1) * SEQ] for g in range(Q_PER_KV)]
    o_ref[...] = jnp.concatenate(outs, axis=1).astype(BF16)


def _rope(x, cos, sin_signed):
    lane = lax.broadcasted_iota(jnp.int32, x.shape, 1)
    first = (lane % (HEAD_DIM // 2)) < (HEAD_DIM // 4)
    partner = jnp.where(first, pltpu.roll(x, V7X_LANES - HEAD_DIM // 4, 1), pltpu.roll(x, HEAD_DIM // 4, 1))
    return x * cos + partner * sin_signed


def _lat_attn_kernel(sink_ref, z_ref, ck_ref, cv_ref, cos_ref, sin_ref, o_ref, q_s, k_s):
    hd = HEAD_DIM
    cos = cos_ref[...]
    sin = sin_ref[...]
    for j in range(ATT_HEADS * hd // V7X_LANES):
        q_s[:, j * V7X_LANES:(j + 1) * V7X_LANES] = _rope(
            z_ref[:, j * V7X_LANES:(j + 1) * V7X_LANES], cos, sin).astype(BF16)
    k_s[...] = _rope(z_ref[:, ATT_HEADS * hd:(ATT_HEADS + KV_HEADS) * hd], cos, sin).astype(BF16)
    voff = (ATT_HEADS + KV_HEADS) * hd
    nb = DEC_SEQ // BLOCK
    for n in range(nb):
        lo = max(n - 1, 0) * BLOCK
        hi = min(n + 2, nb) * BLOCK
        i_abs = n * BLOCK + lax.broadcasted_iota(jnp.int32, (Q_PER_KV * BLOCK, hi - lo), 0) % BLOCK
        j_abs = lo + lax.broadcasted_iota(jnp.int32, (Q_PER_KV * BLOCK, hi - lo), 1)
        band = jnp.abs(i_abs - j_abs) <= WINDOW
        outs = []
        for kv in range(KV_HEADS):
            q2 = jnp.concatenate(
                [q_s[n * BLOCK:(n + 1) * BLOCK, (kv * Q_PER_KV + g) * hd:(kv * Q_PER_KV + g + 1) * hd]
                 for g in range(Q_PER_KV)], axis=0)
            kw = k_s[lo:hi, kv * hd:(kv + 1) * hd]
            vw = z_ref[lo:hi, voff + kv * hd:voff + (kv + 1) * hd].astype(BF16)
            kc = ck_ref[:, kv * hd:(kv + 1) * hd].astype(BF16)
            vc = cv_ref[:, kv * hd:(kv + 1) * hd].astype(BF16)
            s_lat = jnp.where(band, _qk(q2, kw), -1e30)
            o2 = _softmax_pv([s_lat, _qk(q2, kc)], [vw, vc], _sink_col(sink_ref, kv, BLOCK))
            outs += [o2[g * BLOCK:(g + 1) * BLOCK] for g in range(Q_PER_KV)]
        o_ref[n * BLOCK:(n + 1) * BLOCK, :] = jnp.concatenate(outs, axis=1).astype(BF16)


def _ctx_attn(z, sink):
    wd = (ATT_HEADS + 2 * KV_HEADS) * HEAD_DIM
    return pl.pallas_call(
        _ctx_attn_kernel,
        grid=(BATCH,),
        in_specs=[
            pl.BlockSpec(memory_space=pltpu.SMEM),
            pl.BlockSpec((SEQ, wd), lambda i: (i, Z_D // wd)),
        ],
        out_specs=pl.BlockSpec((SEQ, BRANCH_W), lambda i: (i, 0)),
        out_shape=jax.ShapeDtypeStruct((CTX.n_tok, BRANCH_W), BF16),
        compiler_params=_cparams(1),
        name="ctx_attn",
    )(sink, z)


def _lat_attn(z, sink, ck, cv, cos_t, sin_t):
    wd = (ATT_HEADS + 2 * KV_HEADS) * HEAD_DIM
    kvw = KV_HEADS * HEAD_DIM
    return pl.pallas_call(
        _lat_attn_kernel,
        grid=(DEC_BATCH,),
        in_specs=[
            pl.BlockSpec(memory_space=pltpu.SMEM),
            pl.BlockSpec((DEC_SEQ, wd), lambda i: (i, Z_D // wd)),
            pl.BlockSpec((None, PAST_LEN, kvw), lambda i: (i, 0, 0)),
            pl.BlockSpec((None, PAST_LEN, kvw), lambda i: (i, 0, 0)),
            _const_spec((DEC_SEQ, V7X_LANES)),
            _const_spec((DEC_SEQ, V7X_LANES)),
        ],
        out_specs=pl.BlockSpec((DEC_SEQ, BRANCH_W), lambda i: (i, 0)),
        out_shape=jax.ShapeDtypeStruct((LAT.n_tok, BRANCH_W), BF16),
        scratch_shapes=[pltpu.VMEM((DEC_SEQ, ATT_HEADS * HEAD_DIM), BF16),
                        pltpu.VMEM((DEC_SEQ, kvw), BF16)],
        compiler_params=_cparams(1),
        name="lat_attn",
    )(sink, z, ck, cv, cos_t, sin_t)


def _scan_out_ctx(y_ref, slab):
    nb, tt = CTX.nb, CTX.tt
    cols = []
    for t in range(tt):
        ys = y_ref[0, t] + y_ref[1, t]
        cols.append(jnp.concatenate([ys[:, h * nb:(h + 1) * nb] for h in range(RWKV_HEADS)], axis=0))
    by_t = jnp.concatenate(cols, axis=1).T
    n_slab = BRANCH_W // V7X_LANES
    for t in range(tt):
        for s in range(n_slab):
            slab[s, pl.ds(t, nb, stride=tt), :] = by_t[t * nb:(t + 1) * nb, s * V7X_LANES:(s + 1) * V7X_LANES]
    return jnp.concatenate([slab[s] for s in range(n_slab)], axis=1)


def _scan_out_lat(y_ref, ysum, gy):
    nb, tt = LAT.nb, LAT.tt
    chains = nb * RWKV_HEADS
    ysum[...] = y_ref[0] + y_ref[1]
    for vp in range(V7X_SUBLANES):
        a = ysum[pl.ds(vp, tt, stride=V7X_SUBLANES), :].T
        for vl in range(LAT_VL):
            gy[pl.ds(vp + V7X_SUBLANES * vl, chains, stride=HEAD_DIM), :] = a[vl * chains:(vl + 1) * chains, :]
    return jnp.concatenate([gy[b * BRANCH_W:(b + 1) * BRANCH_W, :].T for b in range(nb)], axis=0)


def _mix_kernel(x_ref, mod_ref, g1_ref, oa_ref, ob_ref, od_ref, y_ref, g_ref, bonus_ref,
                lnxg_ref, lnxb_ref, ones_ref, wg_ref, bg_ref, wb_ref, wo_ref, x1_ref, *scratch, path):
    nb, tt = path.nb, path.tt
    rows = nb * tt
    r0, nr = path.mod_row0, path.mod_rows

    def mod3(j):
        return mod_ref[r0:r0 + nr, j * D_MODEL:(j + 1) * D_MODEL][:, None, :]

    x3 = x_ref[...]
    h3 = _rms(x3, g1_ref[...]) * (1.0 + mod3(1)) + mod3(0)
    hb = h3.reshape(rows, D_MODEL).astype(BF16)
    y = _scan_out_ctx(y_ref, *scratch) if path is CTX else _scan_out_lat(y_ref, *scratch)
    ones = ones_ref[...]
    mu = _dot_hi(y, ones) * (1.0 / HEAD_DIM)
    dl = y - mu
    var = _dot_hi(dl * dl, ones) * (1.0 / HEAD_DIM)
    yn = dl * lax.rsqrt(var + GN_EPS) * lnxg_ref[...] + lnxb_ref[...]
    bonus = bonus_ref[...].reshape(rows, BRANCH_W)
    gate_c = g_ref[...].reshape(rows, BRANCH_W)
    oc = ((yn + bonus) * gate_c).astype(BF16)
    branches = (oa_ref[...].reshape(rows, BRANCH_W), ob_ref[...].reshape(rows, BRANCH_W), oc,
                od_ref[...].reshape(rows, BRANCH_W))
    mixed = None
    for n, br in enumerate(branches):
        cols = slice(n * D_MODEL, (n + 1) * D_MODEL)
        gate = jax.nn.sigmoid(_dot(hb, wg_ref[:, cols]) + bg_ref[:, cols])
        term = gate * _dot(br, wb_ref[n])
        mixed = term if mixed is None else mixed + term
    mix = _dot(mixed.astype(BF16), wo_ref[...])
    x1_ref[...] = x3 + mod3(2) * mix.reshape(nb, tt, D_MODEL)


def _mix(x, mod, p, oa, ob, od, y, g, bonus, ones, path):
    nb, tt, seq = path.nb, path.tt, path.seq
    tokw = pl.BlockSpec((nb, tt, BRANCH_W), lambda i: (0, i, 0))
    tokd = pl.BlockSpec((nb, tt, D_MODEL), lambda i: (0, i, 0))
    n_slab = BRANCH_W // V7X_LANES
    if path is CTX:
        y_spec = pl.BlockSpec((N_DIR, tt, HEAD_DIM, V7X_LANES), lambda i: (0, i, 0, 0))
        scratch = [pltpu.VMEM((n_slab, nb * tt, V7X_LANES), F32)]
    else:
        y_spec = pl.BlockSpec((N_DIR, tt * V7X_SUBLANES, V7X_LANES), lambda i: (0, i, 0))
        scratch = [pltpu.VMEM((tt * V7X_SUBLANES, V7X_LANES), F32),
                   pltpu.VMEM((nb * BRANCH_W, V7X_LANES), F32)]
    as3 = lambda a: a.reshape(nb, seq, a.shape[-1])
    out = pl.pallas_call(
        functools.partial(_mix_kernel, path=path),
        grid=(seq // tt,),
        in_specs=[
            tokd,
            _const_spec((MOD_ROWS, N_MOD * D_MODEL)),
            _const_spec((1, D_MODEL)),
            tokw, tokw, tokw, y_spec, tokw, tokw,
            _const_spec((1, BRANCH_W)),
            _const_spec((1, BRANCH_W)),
            _const_spec((BRANCH_W, BRANCH_W)),
            _const_spec((D_MODEL, N_BRANCH * D_MODEL)),
            _const_spec((1, N_BRANCH * D_MODEL)),
            _const_spec((N_BRANCH, BRANCH_W, D_MODEL)),
            _const_spec((D_MODEL, D_MODEL)),
        ],
        out_specs=tokd,
        out_shape=jax.ShapeDtypeStruct((nb, seq, D_MODEL), F32),
        scratch_shapes=scratch,
        compiler_params=_cparams(1),
        name=f"branch_mix_{seq}",
    )(as3(x), mod, p["norm1_g"], as3(oa), as3(ob), as3(od), y, g, bonus, p["rwkv_lnx_g"], p["rwkv_lnx_b"],
      ones, p["w_gate"], p["b_gate"], p["w_branch"], p["w_out"])
    return out.reshape(path.n_tok, D_MODEL)


ROUTER_LANES = V7X_LANES


def _route(logits):
    lane = lax.broadcasted_iota(jnp.int32, logits.shape, 1).astype(F32)
    ninf = -jnp.inf
    big = float(ROUTER_LANES)
    gmask = lane < N_GROUPS
    gl = jnp.where(gmask, logits, ninf)
    gmax = jnp.max(gl, axis=-1, keepdims=True)
    gidx = jnp.min(jnp.where(gl == gmax, lane, big), axis=-1, keepdims=True)
    g_w = 1.0 / jnp.sum(jnp.where(gmask, jnp.exp(gl - gmax), 0.0), axis=-1, keepdims=True)
    egroup = jnp.floor((lane - N_GROUPS) * (1.0 / EXPERTS_PER_GROUP))
    emask = (lane >= N_GROUPS) & (lane < N_GROUPS + N_EXPERTS) & (egroup == gidx)
    el = jnp.where(emask, logits, ninf)
    e1 = jnp.max(el, axis=-1, keepdims=True)
    i1 = jnp.min(jnp.where(emask & (el == e1), lane, big), axis=-1, keepdims=True)
    el2 = jnp.where(lane == i1, ninf, el)
    e2 = jnp.max(el2, axis=-1, keepdims=True)
    i2 = jnp.min(jnp.where(emask & (lane != i1) & (el2 == e2), lane, big), axis=-1, keepdims=True)
    t = jnp.exp(e2 - e1)
    den = 1.0 + t
    return jnp.where(lane == i1, g_w * (1.0 / den), 0.0) + jnp.where(lane == i2, g_w * (t / den), 0.0)


def _moe_kernel(x1_ref, mod_ref, g2_ref, wr_ref, br_ref, weg_ref, weu_ref, wed_ref, fg_ref,
                *out_refs, path, final):
    row = _mod_row(pl.program_id(0), path)
    x1 = x1_ref[...]
    h2 = _rms(x1, g2_ref[...]) * (1.0 + _mod_slice(mod_ref, row, 4)) + _mod_slice(mod_ref, row, 3)
    comb = _route(_dot_hi(h2, wr_ref[...]) + br_ref[...])
    hb = h2.astype(BF16)
    gw = EXPERTS_PER_GROUP * EXPERT_FF
    acc = None
    for grp in range(N_GROUPS):
        cols = slice(grp * gw, (grp + 1) * gw)
        hg = _dot(hb, weg_ref[:, cols])
        hu = _dot(hb, weu_ref[:, cols])
        hid = hg * jax.nn.sigmoid(hg) * hu
        parts = []
        for e in range(EXPERTS_PER_GROUP):
            lane = N_GROUPS + grp * EXPERTS_PER_GROUP + e
            parts.append(hid[:, e * EXPERT_FF:(e + 1) * EXPERT_FF] * comb[:, lane:lane + 1])
        term = _dot(jnp.concatenate(parts, axis=1).astype(BF16), wed_ref[cols, :])
        acc = term if acc is None else acc + term
    x2 = x1 + _mod_slice(mod_ref, row, 5) * acc
    out_refs[0][...] = x2
    if final:
        out_refs[1][...] = _rms(x2, fg_ref[...])


def _moe(x1, mod, p, final_g, final, path):
    tm = TOK_TILE
    tokd = pl.BlockSpec((tm, D_MODEL), lambda i: (i, 0))
    ff = N_EXPERTS * EXPERT_FF
    single = pl.Buffered(1)
    out = jax.ShapeDtypeStruct((path.n_tok, D_MODEL), F32)
    return pl.pallas_call(
        functools.partial(_moe_kernel, path=path, final=final),
        grid=(path.n_tok // tm,),
        in_specs=[
            tokd,
            _const_spec((MOD_ROWS, N_MOD * D_MODEL)),
            _const_spec((1, D_MODEL)),
            _const_spec((D_MODEL, ROUTER_LANES)),
            _const_spec((1, ROUTER_LANES)),
            pl.BlockSpec((D_MODEL, ff), lambda i: (0, 0), pipeline_mode=single),
            pl.BlockSpec((D_MODEL, ff), lambda i: (0, 0), pipeline_mode=single),
            pl.BlockSpec((ff, D_MODEL), lambda i: (0, 0), pipeline_mode=single),
            _const_spec((1, D_MODEL)),
        ],
        out_specs=[tokd, tokd] if final else [tokd],
        out_shape=[out, out] if final else [out],
        compiler_params=_cparams(1),
        name=f"moe_{path.seq}" + ("_final" if final else ""),
    )(x1, mod, p["norm2_g"], p["w_router"], p["b_router"], p["w_e_gate"], p["w_e_up"], p["w_e_down"], final_g)


def _block_diag2(w):
    z = jnp.zeros_like(w[0])
    return jnp.concatenate([jnp.concatenate([w[0], z], axis=1), jnp.concatenate([z, w[1]], axis=1)], axis=0)


def _layer_params(l, a):
    row = lambda v: v.reshape(1, -1)
    perm = jnp.concatenate([a["w_in"][l][:, 0:2048], a["w_in"][l][:, 2176:2688], a["w_in"][l][:, 2048:2176]], axis=1)
    w_router = jnp.zeros((D_MODEL, ROUTER_LANES), F32)
    w_router = w_router.at[:, 0:N_GROUPS].set(a["w_rg"][l]).at[:, N_GROUPS:N_GROUPS + N_EXPERTS].set(a["w_re"][l])
    b_router = jnp.zeros((1, ROUTER_LANES), F32)
    b_router = b_router.at[0, 0:N_GROUPS].set(a["b_rg"][l]).at[0, N_GROUPS:N_GROUPS + N_EXPERTS].set(a["b_re"][l])
    ff = N_EXPERTS * EXPERT_FF
    return {
        "norm1_g": row(a["norm1_g"][l]), "norm2_g": row(a["norm2_g"][l]),
        "w_in": perm.astype(BF16),
        "gmlp_ln_g": row(a["gmlp_ln_g"][l]),
        "gmlp_wcat": a["gmlp_ws"][l].transpose(1, 0, 2).reshape(CHUNK, GMLP_GROUPS * CHUNK).astype(BF16),
        "gmlp_bsx": jnp.repeat(a["gmlp_bs"][l].T, BRANCH_W // GMLP_GROUPS, axis=1),
        "conv_w": a["conv_w"][l], "conv_b": row(a["conv_b"][l]),
        "conv_ln_g": row(a["conv_ln_g"][l]), "conv_ln_b": row(a["conv_ln_b"][l]),
        "rwkv_w0": row(a["rwkv_w0"][l]), "rwkv_w2blk": _block_diag2(a["rwkv_w2"][l]),
        "rwkv_a0": row(a["rwkv_a0"][l]), "rwkv_a2blk": _block_diag2(a["rwkv_a2"][l]),
        "rwkv_g2": a["rwkv_g2"][l], "rwkv_kk": row(a["rwkv_kk"][l]), "rwkv_ka": row(a["rwkv_ka"][l]),
        "rwkv_rk": row(a["rwkv_rk"][l]),
        "rwkv_lnx_g": row(a["rwkv_lnx_g"][l]), "rwkv_lnx_b": row(a["rwkv_lnx_b"][l]),
        "attn_sink": a["attn_sink"][l],
        "w_gate": a["w_gate"][l].astype(BF16), "b_gate": row(a["b_gate"][l]),
        "w_branch": a["w_branch"][l].astype(BF16), "w_out": a["w_out"][l].astype(BF16),
        "w_router": w_router, "b_router": b_router,
        "w_e_gate": a["w_e_gate"][l].transpose(1, 0, 2).reshape(D_MODEL, ff).astype(BF16),
        "w_e_up": a["w_e_up"][l].transpose(1, 0, 2).reshape(D_MODEL, ff).astype(BF16),
        "w_e_down": a["w_e_down"][l].reshape(ff, D_MODEL).astype(BF16),
    }


def _rope_tables():
    half = HEAD_DIM // 4
    inv = ROPE_BASE ** (-jnp.arange(half, dtype=F32) / half)
    t = jnp.arange(DEC_SEQ)
    row = (t // GRID_W).astype(F32)[:, None] * inv[None, :]
    col = (t % GRID_W).astype(F32)[:, None] * inv[None, :]
    cos_h = jnp.concatenate([jnp.cos(row), jnp.cos(row), jnp.cos(col), jnp.cos(col)], axis=1)
    sin_h = jnp.concatenate([-jnp.sin(row), jnp.sin(row), -jnp.sin(col), jnp.sin(col)], axis=1)
    reps = V7X_LANES // HEAD_DIM
    return jnp.tile(cos_h, (1, reps)), jnp.tile(sin_h, (1, reps))


def _trunk_layer(x, mod, p, ones, path, final_g, final, attn, scan):
    z = _inproj(x, mod, p["norm1_g"], p["w_in"], path)
    oa, ob = _local_mix(z, p, path)
    od = attn(z)
    g, bonus, y, s_fin = scan(z)
    x1 = _mix(x, mod, p, oa, ob, od, y, g, bonus, ones, path)
    outs = _moe(x1, mod, p, final_g, final, path)
    return outs[0], (outs[1] if final else None), z, s_fin


def kernel(x_prompt, x_sample, cache_k, cache_v, state_rwkv, c, c_ctx, norm1_g, norm2_g, final_norm_g, w_mod, b_mod, w_in, gmlp_ln_g, gmlp_ws, gmlp_bs, conv_w, conv_b, conv_ln_g, conv_ln_b, rwkv_w0, rwkv_w2, rwkv_a0, rwkv_a2, rwkv_g2, rwkv_kk, rwkv_ka, rwkv_rk, rwkv_lnx_g, rwkv_lnx_b, attn_sink, w_gate, b_gate, w_branch, w_out, w_rg, b_rg, w_re, b_re, w_e_gate, w_e_up, w_e_down):
    arrays = dict(norm1_g=norm1_g, norm2_g=norm2_g, w_in=w_in, gmlp_ln_g=gmlp_ln_g, gmlp_ws=gmlp_ws,
                  gmlp_bs=gmlp_bs, conv_w=conv_w, conv_b=conv_b, conv_ln_g=conv_ln_g, conv_ln_b=conv_ln_b,
                  rwkv_w0=rwkv_w0, rwkv_w2=rwkv_w2, rwkv_a0=rwkv_a0, rwkv_a2=rwkv_a2, rwkv_g2=rwkv_g2,
                  rwkv_kk=rwkv_kk, rwkv_ka=rwkv_ka, rwkv_rk=rwkv_rk, rwkv_lnx_g=rwkv_lnx_g,
                  rwkv_lnx_b=rwkv_lnx_b, attn_sink=attn_sink, w_gate=w_gate, b_gate=b_gate,
                  w_branch=w_branch, w_out=w_out, w_rg=w_rg, b_rg=b_rg, w_re=w_re, b_re=b_re,
                  w_e_gate=w_e_gate, w_e_up=w_e_up, w_e_down=w_e_down)
    xc = x_prompt.reshape(CTX.n_tok, D_MODEL)
    xl = x_sample.reshape(LAT.n_tok, D_MODEL)
    cvec = jnp.concatenate([c_ctx[None], c, jnp.zeros((MOD_ROWS - 1 - DEC_BATCH, D_MODEL), F32)], axis=0)
    mod_all = _modulation(cvec, w_mod, b_mod)
    ones = _head_ones()
    cos_t, sin_t = _rope_tables()
    final_g = final_norm_g.reshape(1, D_MODEL)
    kvw = KV_HEADS * HEAD_DIM
    ks_out, vs_out, ss_out = [], [], []
    yc = yl = None
    for l in range(DEPTH):
        p = _layer_params(l, arrays)
        mod = mod_all[l]
        final = l == DEPTH - 1

        def scan_ctx(z):
            g, bonus, r_s, kk_s, v_s, w_s, kd_s, b_s = _prep_ctx(z, p, ones)
            y, s_fin = _scan_ctx(r_s, kk_s, v_s, w_s, kd_s, b_s)
            return g, bonus, y, s_fin

        def scan_lat(z):
            g, bonus, v_s, kq = _prep_lat(z, p, ones)
            s0 = state_rwkv[:, l].reshape(DEC_BATCH, N_DIR, RWKV_HEADS, LAT_VL, V7X_SUBLANES, HEAD_DIM)
            s0 = s0.transpose(1, 5, 4, 3, 0, 2).reshape(N_DIR, HEAD_DIM, V7X_SUBLANES, V7X_LANES)
            y, _ = _scan_lat(kq, v_s, s0)
            return g, bonus, y, None

        ck = cache_k[:, l].reshape(DEC_BATCH, PAST_LEN, kvw)
        cv = cache_v[:, l].reshape(DEC_BATCH, PAST_LEN, kvw)
        xc, yc_l, zc, s_fin = _trunk_layer(
            xc, mod, p, ones, CTX, final_g, final, lambda z: _ctx_attn(z, p["attn_sink"]), scan_ctx)
        xl, yl_l, _, _ = _trunk_layer(
            xl, mod, p, ones, LAT, final_g, final,
            lambda z: _lat_attn(z, p["attn_sink"], ck, cv, cos_t, sin_t), scan_lat)
        if final:
            yc, yl = yc_l, yl_l
        k_off = Z_D + ATT_HEADS * HEAD_DIM
        ks_out.append(zc[:, k_off:k_off + kvw].reshape(BATCH, SEQ, KV_HEADS, HEAD_DIM))
        vs_out.append(zc[:, k_off + kvw:k_off + 2 * kvw].reshape(BATCH, SEQ, KV_HEADS, HEAD_DIM))
        ss_out.append(s_fin.reshape(N_DIR, HEAD_DIM, HEAD_DIM, RWKV_HEADS, BATCH).transpose(4, 0, 3, 2, 1))
    y_prompt = yc.reshape(BATCH, SEQ, D_MODEL)
    y_sample = yl.reshape(DEC_BATCH, DEC_SEQ, D_MODEL)
    return (y_prompt, y_sample, jnp.stack(ks_out, axis=1), jnp.stack(vs_out, axis=1), jnp.stack(ss_out, axis=1))
```

```python
import functools
import math
from typing import NamedTuple

import jax
import jax.numpy as jnp
from jax import lax
from jax.experimental import pallas as pl
from jax.experimental.pallas import tpu as pltpu

D_MODEL = 1024
BATCH = 32
SEQ = 256
DEPTH = 2
DEC_BATCH = 4
DEC_SEQ = 1024
PAST_LEN = 256
GRID_W = 64
HEAD_DIM = 64
BRANCH_W = 256
N_BRANCH = 4
CHUNK = 128
GMLP_GROUPS = 4
CONV_W = 31
RWKV_HEADS = 4
N_DIR = 2
DECAY_LORA = 64
ICL_LORA = 64
GATE_LORA = 128
ATT_HEADS = 4
KV_HEADS = 2
Q_PER_KV = ATT_HEADS // KV_HEADS
WINDOW = 128
BLOCK = 128
ROPE_BASE = 10000.0
ATT_SCALE = HEAD_DIM ** -0.5
N_GROUPS = 4
EXPERTS_PER_GROUP = 4
N_EXPERTS = N_GROUPS * EXPERTS_PER_GROUP
EXPERT_FF = 256
N_MOD = 6
RMS_EPS = 1e-6
LN_EPS = 1e-5
GN_EPS = 64e-5
IN_COLS = 2688

V7X_LANES = 128
V7X_SUBLANES = 8
V7X_VMEM_LIMIT = 56 * 1024 * 1024

MOD_ROWS = 8
TOK_TILE = 512
SCAN_TB = 32
CONV_PAD = 16

Z_AB, Z_C, Z_D, Z_G = 0, 1024, 2048, 2560

LAT_Q_R, LAT_Q_KK, LAT_Q_W, LAT_Q_KD, LAT_Q_B = 0, 1, 2, 4, 6
LAT_NQ = 8
LAT_VL = V7X_LANES // (DEC_BATCH * RWKV_HEADS)

F32 = jnp.float32
BF16 = jnp.bfloat16
HIGHEST = lax.Precision.HIGHEST


class _Path(NamedTuple):
    nb: int
    seq: int
    mod_row0: int
    mod_rows: int
    tt: int

    @property
    def n_tok(self):
        return self.nb * self.seq


CTX = _Path(BATCH, SEQ, 0, 1, TOK_TILE // BATCH)
LAT = _Path(DEC_BATCH, DEC_SEQ, 1, DEC_BATCH, TOK_TILE // DEC_BATCH)


def _cparams(n_axes, vmem=V7X_VMEM_LIMIT):
    return pltpu.CompilerParams(dimension_semantics=("arbitrary",) * n_axes, vmem_limit_bytes=vmem)


def _const_spec(shape):
    nd = len(shape)
    return pl.BlockSpec(shape, lambda *_: (0,) * nd)


def _dot(a, b):
    return jnp.dot(a, b, preferred_element_type=F32)


def _dot_hi(a, b):
    return jnp.dot(a, b, preferred_element_type=F32, precision=HIGHEST)


def _head_ones():
    r = lax.broadcasted_iota(jnp.int32, (BRANCH_W, BRANCH_W), 0) // HEAD_DIM
    c = lax.broadcasted_iota(jnp.int32, (BRANCH_W, BRANCH_W), 1) // HEAD_DIM
    return (r == c).astype(F32)


def _mod_row(tile, path):
    if path.mod_rows == 1:
        return path.mod_row0
    return path.mod_row0 + tile // (path.seq // TOK_TILE)


def _mod_slice(mod_ref, row, j):
    return mod_ref[pl.ds(row, 1), j * D_MODEL:(j + 1) * D_MODEL]


def _rms(x, g):
    return x * lax.rsqrt(jnp.mean(x * x, axis=-1, keepdims=True) + RMS_EPS) * g


def _layernorm(x, g, b=None, eps=LN_EPS):
    mu = jnp.mean(x, axis=-1, keepdims=True)
    d = x - mu
    var = jnp.mean(d * d, axis=-1, keepdims=True)
    y = d * lax.rsqrt(var + eps) * g
    return y if b is None else y + b


def _mod_kernel(c_ref, w_ref, b_ref, o_ref):
    c = c_ref[...]
    a = c * jax.nn.sigmoid(c)
    o_ref[0] = _dot_hi(a, w_ref[0]) + b_ref[0]


def _modulation(cvec, w_mod, b_mod):
    return pl.pallas_call(
        _mod_kernel,
        grid=(DEPTH, N_MOD),
        in_specs=[
            pl.BlockSpec((MOD_ROWS, D_MODEL), lambda l, j: (0, 0)),
            pl.BlockSpec((1, D_MODEL, D_MODEL), lambda l, j: (l, 0, j)),
            pl.BlockSpec((1, 1, D_MODEL), lambda l, j: (l, 0, j)),
        ],
        out_specs=pl.BlockSpec((1, MOD_ROWS, D_MODEL), lambda l, j: (l, 0, j)),
        out_shape=jax.ShapeDtypeStruct((DEPTH, MOD_ROWS, N_MOD * D_MODEL), F32),
        compiler_params=_cparams(2),
        name="modulation",
    )(cvec, w_mod, b_mod.reshape(DEPTH, 1, N_MOD * D_MODEL))


def _inproj_kernel(x_ref, mod_ref, g_ref, w_ref, z_ref, *, path):
    row = _mod_row(pl.program_id(0), path)
    h = _rms(x_ref[...], g_ref[...]) * (1.0 + _mod_slice(mod_ref, row, 1)) + _mod_slice(mod_ref, row, 0)
    z_ref[...] = _dot(h.astype(BF16), w_ref[...])


def _inproj(x, mod, g1, w_in_b, path):
    tm = TOK_TILE
    return pl.pallas_call(
        functools.partial(_inproj_kernel, path=path),
        grid=(path.n_tok // tm,),
        in_specs=[
            pl.BlockSpec((tm, D_MODEL), lambda i: (i, 0)),
            _const_spec((MOD_ROWS, N_MOD * D_MODEL)),
            _const_spec((1, D_MODEL)),
            _const_spec((D_MODEL, IN_COLS)),
        ],
        out_specs=pl.BlockSpec((tm, IN_COLS), lambda i: (i, 0)),
        out_shape=jax.ShapeDtypeStruct((path.n_tok, IN_COLS), F32),
        compiler_params=_cparams(1),
        name=f"inproj_{path.seq}",
    )(x, mod, g1, w_in_b)


def _local_mix_kernel(z_ref, lng_ref, wcat_ref, bsx_ref, cw_ref, cb_ref, clg_ref, clb_ref,
                      oa_ref, ob_ref, ypad, *, seq_len):
    lane_group = lax.broadcasted_iota(jnp.int32, (CHUNK, BRANCH_W), 1) // (BRANCH_W // GMLP_GROUPS)
    for c in range(seq_len // CHUNK):
        rows = pl.ds(c * CHUNK, CHUNK)
        u = jax.nn.gelu(z_ref[rows, 0:BRANCH_W], approximate=True)
        v = jax.nn.gelu(z_ref[rows, BRANCH_W:2 * BRANCH_W], approximate=True)
        vn = _layernorm(v, lng_ref[...])
        vblk = jnp.concatenate(
            [jnp.where(lane_group == g, vn, 0.0) for g in range(GMLP_GROUPS)], axis=0).astype(BF16)
        mixed = _dot(wcat_ref[...], vblk) + bsx_ref[...]
        oa_ref[rows, :] = (u * mixed).astype(BF16)

    zeros = jnp.zeros((CONV_PAD, BRANCH_W), F32)
    ypad[0:CONV_PAD, :] = zeros
    ypad[CONV_PAD + seq_len:2 * CONV_PAD + seq_len, :] = zeros
    ypad[CONV_PAD:CONV_PAD + seq_len, :] = (
        z_ref[:, 2 * BRANCH_W:3 * BRANCH_W] * jax.nn.sigmoid(z_ref[:, 3 * BRANCH_W:4 * BRANCH_W]))
    base = CONV_PAD - CONV_W // 2
    for c in range(seq_len // CHUNK):
        acc = jnp.zeros((CHUNK, BRANCH_W), F32)
        for j in range(CONV_W):
            acc = acc + cw_ref[j:j + 1, :] * ypad[pl.ds(c * CHUNK + base + j, CHUNK), :]
        y = _layernorm(acc + cb_ref[...], clg_ref[...], clb_ref[...])
        ob_ref[pl.ds(c * CHUNK, CHUNK), :] = (y * jax.nn.sigmoid(y)).astype(BF16)


def _local_mix(z, p, path):
    out = jax.ShapeDtypeStruct((path.n_tok, BRANCH_W), BF16)
    return pl.pallas_call(
        functools.partial(_local_mix_kernel, seq_len=path.seq),
        grid=(path.nb,),
        in_specs=[
            pl.BlockSpec((path.seq, 4 * BRANCH_W), lambda i: (i, Z_AB // (4 * BRANCH_W))),
            _const_spec((1, BRANCH_W)),
            _const_spec((CHUNK, GMLP_GROUPS * CHUNK)),
            _const_spec((CHUNK, BRANCH_W)),
            _const_spec((CONV_W, BRANCH_W)),
            _const_spec((1, BRANCH_W)),
            _const_spec((1, BRANCH_W)),
            _const_spec((1, BRANCH_W)),
        ],
        out_specs=[pl.BlockSpec((path.seq, BRANCH_W), lambda i: (i, 0))] * 2,
        out_shape=[out, out],
        scratch_shapes=[pltpu.VMEM((path.seq + 2 * CONV_PAD, BRANCH_W), F32)],
        compiler_params=_cparams(1),
        name=f"local_mix_{path.seq}",
    )(z, p["gmlp_ln_g"], p["gmlp_wcat"], p["gmlp_bsx"], p["conv_w"], p["conv_b"],
      p["conv_ln_g"], p["conv_ln_b"])


def _rwkv_quantities(zc, gd, w0_ref, w2_ref, a0_ref, a2_ref, g2_ref, kkp_ref, ka_ref, rk_ref, ones):
    W = BRANCH_W
    r = zc[:, 0:W]
    k = zc[:, W:2 * W]
    v = zc[:, 2 * W:3 * W]
    wd = zc[:, 3 * W:3 * W + N_DIR * DECAY_LORA]
    ad = zc[:, 3 * W + N_DIR * DECAY_LORA:4 * W]
    w_raw = _dot_hi(jnp.tanh(wd), w2_ref[...]) + w0_ref[...]
    decay = jnp.exp(-math.exp(-0.5) * jax.nn.sigmoid(w_raw))
    a = jax.nn.sigmoid(_dot_hi(ad, a2_ref[...]) + a0_ref[...])
    g = _dot_hi(jax.nn.sigmoid(gd), g2_ref[...])
    kkr = k * kkp_ref[...]
    kk = kkr / jnp.maximum(jnp.sqrt(_dot_hi(kkr * kkr, ones)), 1e-12)
    ka = ka_ref[...]
    w, kd, b = [], [], []
    for d in range(N_DIR):
        a_d = a[:, d * W:(d + 1) * W]
        w.append(decay[:, d * W:(d + 1) * W])
        kd.append(k * (1.0 + (a_d - 1.0) * ka))
        b.append(kk * a_d)
    bonus = _dot_hi(r * (kd[0] + kd[1]) * rk_ref[...], ones) * v
    return g, bonus, r, kk, v, w, kd, b


def _prep_ctx_kernel(zc_ref, zg_ref, w0_ref, w2_ref, a0_ref, a2_ref, g2_ref, kkp_ref, ka_ref, rk_ref,
                     ones_ref, g_o, bonus_o, r_o, kk_o, v_o, w_o, kd_o, b_o, slab):
    nb, tt = CTX.nb, CTX.tt
    rows = nb * tt
    zc = zc_ref[...].reshape(rows, 4 * BRANCH_W)
    gd = zg_ref[...].reshape(rows, GATE_LORA)
    g, bonus, r, kk, v, w, kd, b = _rwkv_quantities(
        zc, gd, w0_ref, w2_ref, a0_ref, a2_ref, g2_ref, kkp_ref, ka_ref, rk_ref, ones_ref[...])
    g_o[...] = g.reshape(nb, tt, BRANCH_W)
    bonus_o[...] = bonus.reshape(nb, tt, BRANCH_W)
    n_slab = BRANCH_W // V7X_LANES
    lanes_per_head = V7X_LANES // RWKV_HEADS

    def to_chains(q, put):
        for s in range(n_slab):
            slab[s] = q[:, s * V7X_LANES:(s + 1) * V7X_LANES]
        by_t = jnp.concatenate(
            [jnp.concatenate([slab[s, pl.ds(t, nb, stride=tt), :] for s in range(n_slab)], axis=1)
             for t in range(tt)], axis=0)
        tr = by_t.T
        for t in range(tt):
            put(t, jnp.concatenate(
                [tr[h * HEAD_DIM:(h + 1) * HEAD_DIM, t * nb:(t + 1) * nb] for h in range(RWKV_HEADS)], axis=1))
    assert lanes_per_head == nb

    def put_into(ref, *lead):
        def put(t, tile):
            ref[(*lead, t)] = tile
        return put

    to_chains(r, put_into(r_o))
    to_chains(kk, put_into(kk_o))
    to_chains(v, put_into(v_o))
    for d in range(N_DIR):
        to_chains(w[d], put_into(w_o, d))
        to_chains(kd[d], put_into(kd_o, d))
        to_chains(b[d], put_into(b_o, d))


def _prep_lat_kernel(zc_ref, zg_ref, w0_ref, w2_ref, a0_ref, a2_ref, g2_ref, kkp_ref, ka_ref, rk_ref,
                     ones_ref, g_o, bonus_o, v_o, kq_o, gk, gv):
    nb, tt = LAT.nb, LAT.tt
    rows = nb * tt
    qi = pl.program_id(1)
    chains = nb * RWKV_HEADS

    @pl.when(qi == 0)
    def _():
        zc = zc_ref[...].reshape(rows, 4 * BRANCH_W)
        gd = zg_ref[...].reshape(rows, GATE_LORA)
        g, bonus, r, kk, v, w, kd, b = _rwkv_quantities(
            zc, gd, w0_ref, w2_ref, a0_ref, a2_ref, g2_ref, kkp_ref, ka_ref, rk_ref, ones_ref[...])
        g_o[...] = g.reshape(nb, tt, BRANCH_W)
        bonus_o[...] = bonus.reshape(nb, tt, BRANCH_W)
        k_indexed = {LAT_Q_R: r, LAT_Q_KK: kk, LAT_Q_W: w[0], LAT_Q_W + 1: w[1],
                     LAT_Q_KD: kd[0], LAT_Q_KD + 1: kd[1], LAT_Q_B: b[0], LAT_Q_B + 1: b[1]}
        for bi in range(nb):
            for q_idx, q in k_indexed.items():
                gk[q_idx, bi * BRANCH_W:(bi + 1) * BRANCH_W, :] = q[bi * tt:(bi + 1) * tt, :].T
            gv[bi * BRANCH_W:(bi + 1) * BRANCH_W, :] = v[bi * tt:(bi + 1) * tt, :].T
        for vp in range(V7X_SUBLANES):
            a = jnp.concatenate(
                [gv[pl.ds(vp + V7X_SUBLANES * vl, chains, stride=HEAD_DIM), :] for vl in range(LAT_VL)], axis=0)
            v_o[pl.ds(vp, tt, stride=V7X_SUBLANES), :] = a.T

    def body(k, carry):
        a = gk[qi, pl.ds(k, chains, stride=HEAD_DIM), :]
        n8 = V7X_SUBLANES
        kq_o[k // n8, pl.ds(k % n8, tt, stride=n8), :] = jnp.concatenate([a] * LAT_VL, axis=0).T
        return carry
    lax.fori_loop(0, HEAD_DIM, body, 0, unroll=8)


def _prep_param_specs():
    return [
        _const_spec((1, N_DIR * BRANCH_W)),
        _const_spec((N_DIR * DECAY_LORA, N_DIR * BRANCH_W)),
        _const_spec((1, N_DIR * BRANCH_W)),
        _const_spec((N_DIR * ICL_LORA, N_DIR * BRANCH_W)),
        _const_spec((GATE_LORA, BRANCH_W)),
        _const_spec((1, BRANCH_W)),
        _const_spec((1, BRANCH_W)),
        _const_spec((1, BRANCH_W)),
        _const_spec((BRANCH_W, BRANCH_W)),
    ]


def _prep_params(p, ones):
    return (p["rwkv_w0"], p["rwkv_w2blk"], p["rwkv_a0"], p["rwkv_a2blk"], p["rwkv_g2"],
            p["rwkv_kk"], p["rwkv_ka"], p["rwkv_rk"], ones)


def _prep_ctx(z, p, ones):
    nb, tt, seq = CTX.nb, CTX.tt, CTX.seq
    z3 = z.reshape(nb, seq, IN_COLS)
    tok = jax.ShapeDtypeStruct((nb, seq, BRANCH_W), F32)
    tok_spec = pl.BlockSpec((nb, tt, BRANCH_W), lambda i: (0, i, 0))
    ch = jax.ShapeDtypeStruct((seq, HEAD_DIM, V7X_LANES), F32)
    ch2 = jax.ShapeDtypeStruct((N_DIR, seq, HEAD_DIM, V7X_LANES), F32)
    ch_spec = pl.BlockSpec((tt, HEAD_DIM, V7X_LANES), lambda i: (i, 0, 0))
    ch2_spec = pl.BlockSpec((N_DIR, tt, HEAD_DIM, V7X_LANES), lambda i: (0, i, 0, 0))
    return pl.pallas_call(
        _prep_ctx_kernel,
        grid=(seq // tt,),
        in_specs=[
            pl.BlockSpec((nb, tt, 4 * BRANCH_W), lambda i: (0, i, Z_C // (4 * BRANCH_W))),
            pl.BlockSpec((nb, tt, GATE_LORA), lambda i: (0, i, Z_G // GATE_LORA)),
        ] + _prep_param_specs(),
        out_specs=[tok_spec, tok_spec, ch_spec, ch_spec, ch_spec, ch2_spec, ch2_spec, ch2_spec],
        out_shape=[tok, tok, ch, ch, ch, ch2, ch2, ch2],
        scratch_shapes=[pltpu.VMEM((BRANCH_W // V7X_LANES, nb * tt, V7X_LANES), F32)],
        compiler_params=_cparams(1),
        name="rwkv_prep_ctx",
    )(z3, z3, *_prep_params(p, ones))


def _prep_lat(z, p, ones):
    nb, tt, seq = LAT.nb, LAT.tt, LAT.seq
    z3 = z.reshape(nb, seq, IN_COLS)
    tok = jax.ShapeDtypeStruct((nb, seq, BRANCH_W), F32)
    tok_spec = pl.BlockSpec((nb, tt, BRANCH_W), lambda i, q: (0, i, 0))
    return pl.pallas_call(
        _prep_lat_kernel,
        grid=(seq // tt, LAT_NQ),
        in_specs=[
            pl.BlockSpec((nb, tt, 4 * BRANCH_W), lambda i, q: (0, i, Z_C // (4 * BRANCH_W))),
            pl.BlockSpec((nb, tt, GATE_LORA), lambda i, q: (0, i, Z_G // GATE_LORA)),
        ] + _prep_param_specs(),
        out_specs=[
            tok_spec, tok_spec,
            pl.BlockSpec((tt * V7X_SUBLANES, V7X_LANES), lambda i, q: (i, 0)),
            pl.BlockSpec((None, HEAD_DIM // V7X_SUBLANES, tt * V7X_SUBLANES, V7X_LANES), lambda i, q: (q, 0, i, 0)),
        ],
        out_shape=[tok, tok,
                   jax.ShapeDtypeStruct((seq * V7X_SUBLANES, V7X_LANES), F32),
                   jax.ShapeDtypeStruct((LAT_NQ, HEAD_DIM // V7X_SUBLANES, seq * V7X_SUBLANES, V7X_LANES), F32)],
        scratch_shapes=[pltpu.VMEM((LAT_NQ, nb * BRANCH_W, V7X_LANES), F32),
                        pltpu.VMEM((nb * BRANCH_W, V7X_LANES), F32)],
        compiler_params=_cparams(2),
        name="rwkv_prep_lat",
    )(z3, z3, *_prep_params(p, ones))


SCAN_ACCS = 4


def _strided_sum(terms):
    acc = [None] * SCAN_ACCS
    for j, x in enumerate(terms):
        a = j % SCAN_ACCS
        acc[a] = x if acc[a] is None else acc[a] + x
    while len(acc) > 1:
        acc = [acc[j] + acc[j + 1] for j in range(0, len(acc), 2)]
    return acc[0]


def _scan_ctx_kernel(r_ref, kk_ref, v_ref, w_ref, kd_ref, b_ref, y_ref, sfin_ref, S, *, tb):
    d = pl.program_id(0)
    i = pl.program_id(1)
    n8 = V7X_SUBLANES
    n_v8 = HEAD_DIM // n8

    @pl.when(i == 0)
    def _():
        S[...] = jnp.zeros_like(S)

    def row8(ref, t, k):
        return jnp.broadcast_to(ref[t, pl.ds(k, 1), :], (n8, V7X_LANES))

    def step(s, carry):
        t = s + d * (tb - 1 - 2 * s)

        def pass1(k, sa):
            kkb = row8(kk_ref, t, k)
            return tuple(sa[vo] + S[k, vo * n8:(vo + 1) * n8, :] * kkb for vo in range(n_v8))

        zero = tuple(jnp.zeros((n8, V7X_LANES), F32) for _ in range(n_v8))
        sa = lax.fori_loop(0, HEAD_DIM, pass1, zero, unroll=16)
        nsa = tuple(-x for x in sa)
        vt = tuple(v_ref[t, vo * n8:(vo + 1) * n8, :] for vo in range(n_v8))

        def pass2(k, y):
            wb = row8(w_ref, t, k)
            bb = row8(b_ref, t, k)
            kb = row8(kd_ref, t, k)
            rb = row8(r_ref, t, k)
            out = []
            for vo in range(n_v8):
                sl = slice(vo * n8, (vo + 1) * n8)
                sn = S[k, sl, :] * wb + (nsa[vo] * bb + vt[vo] * kb)
                S[k, sl, :] = sn
                out.append(y[vo] + sn * rb)
            return tuple(out)

        y = lax.fori_loop(0, HEAD_DIM, pass2, zero, unroll=16)
        for vo in range(n_v8):
            y_ref[t, vo * n8:(vo + 1) * n8, :] = y[vo]
        return carry

    lax.fori_loop(0, tb, step, 0)

    @pl.when(i == pl.num_programs(1) - 1)
    def _():
        sfin_ref[...] = S[...]


def _scan_ctx(r_s, kk_s, v_s, w_s, kd_s, b_s):
    seq, tb = CTX.seq, SCAN_TB
    nt = seq // tb

    def tm(d, i):
        return i + d * (nt - 1 - 2 * i)
    shared = pl.BlockSpec((tb, HEAD_DIM, V7X_LANES), lambda d, i: (tm(d, i), 0, 0))
    per_dir = pl.BlockSpec((None, tb, HEAD_DIM, V7X_LANES), lambda d, i: (d, tm(d, i), 0, 0))
    state = pl.BlockSpec((None, HEAD_DIM, HEAD_DIM, V7X_LANES), lambda d, i: (d, 0, 0, 0))
    return pl.pallas_call(
        functools.partial(_scan_ctx_kernel, tb=tb),
        grid=(N_DIR, nt),
        in_specs=[shared, shared, shared, per_dir, per_dir, per_dir],
        out_specs=[per_dir, state],
        out_shape=[jax.ShapeDtypeStruct((N_DIR, seq, HEAD_DIM, V7X_LANES), F32),
                   jax.ShapeDtypeStruct((N_DIR, HEAD_DIM, HEAD_DIM, V7X_LANES), F32)],
        scratch_shapes=[pltpu.VMEM((HEAD_DIM, HEAD_DIM, V7X_LANES), F32)],
        compiler_params=_cparams(2),
        name="rwkv_scan_ctx",
    )(r_s, kk_s, v_s, w_s, kd_s, b_s)


def _scan_lat_kernel(*refs, tb):
    fwd, bwd = refs[0:6], refs[6:12]
    s0_ref, yf_ref, yb_ref, S = refs[12:]
    n8 = V7X_SUBLANES

    @pl.when(pl.program_id(0) == 0)
    def _():
        S[...] = s0_ref[...]

    def one_dir(d, t, r_ref, kk_ref, v_ref, w_ref, kd_ref, b_ref, y_ref):
        def row8(ref, k):
            return jnp.broadcast_to(ref[k // n8, pl.ds(t * n8 + k % n8, 1), :], (n8, V7X_LANES))
        rows = pl.ds(pl.multiple_of(t * n8, n8), n8)
        nsa = -_strided_sum(S[d, k] * row8(kk_ref, k) for k in range(HEAD_DIM))
        vt = v_ref[rows, :]

        def update(k):
            sn = S[d, k] * row8(w_ref, k) + (nsa * row8(b_ref, k) + vt * row8(kd_ref, k))
            S[d, k] = sn
            return sn * row8(r_ref, k)
        y_ref[rows, :] = _strided_sum(update(k) for k in range(HEAD_DIM))

    def step(s, carry):
        one_dir(0, s, *fwd, yf_ref)
        one_dir(1, tb - 1 - s, *bwd, yb_ref)
        return carry

    lax.fori_loop(0, tb, step, 0)


def _scan_lat(kq, v_s, s0):
    seq, tb = LAT.seq, SCAN_TB
    nt = seq // tb
    n8 = V7X_SUBLANES

    def specs(d):
        tblk = (lambda i: i) if d == 0 else (lambda i: nt - 1 - i)
        plane = lambda q: pl.BlockSpec((None, HEAD_DIM // n8, tb * n8, V7X_LANES), lambda i: (q, 0, tblk(i), 0))
        rows = pl.BlockSpec((tb * n8, V7X_LANES), lambda i: (tblk(i), 0))
        return [plane(LAT_Q_R), plane(LAT_Q_KK), rows,
                plane(LAT_Q_W + d), plane(LAT_Q_KD + d), plane(LAT_Q_B + d)], rows
    in_f, y_f = specs(0)
    in_b, y_b = specs(1)
    y_shape = jax.ShapeDtypeStruct((seq * n8, V7X_LANES), F32)
    one = (kq, kq, v_s, kq, kq, kq)
    return pl.pallas_call(
        functools.partial(_scan_lat_kernel, tb=tb),
        grid=(nt,),
        in_specs=in_f + in_b + [_const_spec((N_DIR, HEAD_DIM, n8, V7X_LANES))],
        out_specs=[y_f, y_b],
        out_shape=[y_shape, y_shape],
        scratch_shapes=[pltpu.VMEM((N_DIR, HEAD_DIM, n8, V7X_LANES), F32)],
        compiler_params=_cparams(1),
        name="rwkv_scan_lat",
    )(*one, *one, s0)


def _softmax_pv(scores, vals, sink_col):
    m = sink_col
    for s in scores:
        m = jnp.maximum(m, jnp.max(s, axis=-1, keepdims=True))
    den = jnp.exp(sink_col - m)
    out = None
    for s, vx in zip(scores, vals):
        p = jnp.exp(s - m)
        den = den + jnp.sum(p, axis=-1, keepdims=True)
        o = _dot(p.astype(BF16), vx)
        out = o if out is None else out + o
    return out / den


def _sink_col(sink_ref, kv, n_rows):
    row = lax.broadcasted_iota(jnp.int32, (Q_PER_KV * n_rows, 1), 0)
    col = jnp.full((Q_PER_KV * n_rows, 1), sink_ref[kv * Q_PER_KV], F32)
    for g in range(1, Q_PER_KV):
        col = jnp.where(row >= g * n_rows, sink_ref[kv * Q_PER_KV + g], col)
    return col


def _qk(q2, kh):
    return lax.dot_general(q2, kh, (((1,), (1,)), ((), ())), preferred_element_type=F32) * ATT_SCALE


def _ctx_attn_kernel(sink_ref, z_ref, o_ref):
    hd = HEAD_DIM
    q = z_ref[:, 0:ATT_HEADS * hd]
    outs = []
    for kv in range(KV_HEADS):
        kh = z_ref[:, ATT_HEADS * hd + kv * hd:ATT_HEADS * hd + (kv + 1) * hd].astype(BF16)
        vh = z_ref[:, (ATT_HEADS + KV_HEADS) * hd + kv * hd:(ATT_HEADS + KV_HEADS) * hd + (kv + 1) * hd].astype(BF16)
        q2 = jnp.concatenate(
            [q[:, (kv * Q_PER_KV + g) * hd:(kv * Q_PER_KV + g + 1) * hd] for g in range(Q_PER_KV)],
            axis=0).astype(BF16)
        o2 = _softmax_pv([_qk(q2, kh)], [vh], _sink_col(sink_ref, kv, SEQ))
        outs += [o2[g * SEQ:(g + 1) * SEQ] for g in range(Q_PER_KV)]
    o_ref[...] = jnp.concatenate(outs, axis=1).astype(BF16)


def _rope(x, cos, sin_signed):
    lane = lax.broadcasted_iota(jnp.int32, x.shape, 1)
    first = (lane % (HEAD_DIM // 2)) < (HEAD_DIM // 4)
    partner = jnp.where(first, pltpu.roll(x, V7X_LANES - HEAD_DIM // 4, 1), pltpu.roll(x, HEAD_DIM // 4, 1))
    return x * cos + partner * sin_signed


def _lat_attn_kernel(sink_ref, z_ref, ck_ref, cv_ref, cos_ref, sin_ref, o_ref, q_s, k_s):
    hd = HEAD_DIM
    cos = cos_ref[...]
    sin = sin_ref[...]
    for j in range(ATT_HEADS * hd // V7X_LANES):
        q_s[:, j * V7X_LANES:(j + 1) * V7X_LANES] = _rope(
            z_ref[:, j * V7X_LANES:(j + 1) * V7X_LANES], cos, sin).astype(BF16)
    k_s[...] = _rope(z_ref[:, ATT_HEADS * hd:(ATT_HEADS + KV_HEADS) * hd], cos, sin).astype(BF16)
    voff = (ATT_HEADS + KV_HEADS) * hd
    nb = DEC_SEQ // BLOCK
    for n in range(nb):
        lo = max(n - 1, 0) * BLOCK
        hi = min(n + 2, nb) * BLOCK
        i_abs = n * BLOCK + lax.broadcasted_iota(jnp.int32, (Q_PER_KV * BLOCK, hi - lo), 0) % BLOCK
        j_abs = lo + lax.broadcasted_iota(jnp.int32, (Q_PER_KV * BLOCK, hi - lo), 1)
        band = jnp.abs(i_abs - j_abs) <= WINDOW
        outs = []
        for kv in range(KV_HEADS):
            q2 = jnp.concatenate(
                [q_s[n * BLOCK:(n + 1) * BLOCK, (kv * Q_PER_KV + g) * hd:(kv * Q_PER_KV + g + 1) * hd]
                 for g in range(Q_PER_KV)], axis=0)
            kw = k_s[lo:hi, kv * hd:(kv + 1) * hd]
            vw = z_ref[lo:hi, voff + kv * hd:voff + (kv + 1) * hd].astype(BF16)
            kc = ck_ref[:, kv * hd:(kv + 1) * hd].astype(BF16)
            vc = cv_ref[:, kv * hd:(kv + 1) * hd].astype(BF16)
            s_lat = jnp.where(band, _qk(q2, kw), -1e30)
            o2 = _softmax_pv([s_lat, _qk(q2, kc)], [vw, vc], _sink_col(sink_ref, kv, BLOCK))
            outs += [o2[g * BLOCK:(g + 1) * BLOCK] for g in range(Q_PER_KV)]
        o_ref[n * BLOCK:(n + 1) * BLOCK, :] = jnp.concatenate(outs, axis=1).astype(BF16)


def _ctx_attn(z, sink):
    wd = (ATT_HEADS + 2 * KV_HEADS) * HEAD_DIM
    return pl.pallas_call(
        _ctx_attn_kernel,
        grid=(BATCH,),
        in_specs=[
            pl.BlockSpec(memory_space=pltpu.SMEM),
            pl.BlockSpec((SEQ, wd), lambda i: (i, Z_D // wd)),
        ],
        out_specs=pl.BlockSpec((SEQ, BRANCH_W), lambda i: (i, 0)),
        out_shape=jax.ShapeDtypeStruct((CTX.n_tok, BRANCH_W), BF16),
        compiler_params=_cparams(1),
        name="ctx_attn",
    )(sink, z)


def _lat_attn(z, sink, ck, cv, cos_t, sin_t):
    wd = (ATT_HEADS + 2 * KV_HEADS) * HEAD_DIM
    kvw = KV_HEADS * HEAD_DIM
    return pl.pallas_call(
        _lat_attn_kernel,
        grid=(DEC_BATCH,),
        in_specs=[
            pl.BlockSpec(memory_space=pltpu.SMEM),
            pl.BlockSpec((DEC_SEQ, wd), lambda i: (i, Z_D // wd)),
            pl.BlockSpec((None, PAST_LEN, kvw), lambda i: (i, 0, 0)),
            pl.BlockSpec((None, PAST_LEN, kvw), lambda i: (i, 0, 0)),
            _const_spec((DEC_SEQ, V7X_LANES)),
            _const_spec((DEC_SEQ, V7X_LANES)),
        ],
        out_specs=pl.BlockSpec((DEC_SEQ, BRANCH_W), lambda i: (i, 0)),
        out_shape=jax.ShapeDtypeStruct((LAT.n_tok, BRANCH_W), BF16),
        scratch_shapes=[pltpu.VMEM((DEC_SEQ, ATT_HEADS * HEAD_DIM), BF16),
                        pltpu.VMEM((DEC_SEQ, kvw), BF16)],
        compiler_params=_cparams(1),
        name="lat_attn",
    )(sink, z, ck, cv, cos_t, sin_t)


def _scan_out_ctx(yf_ref, yb_ref, slab):
    nb, tt = CTX.nb, CTX.tt
    cols = []
    for t in range(tt):
        ys = yf_ref[t] + yb_ref[t]
        cols.append(jnp.concatenate([ys[:, h * nb:(h + 1) * nb] for h in range(RWKV_HEADS)], axis=0))
    by_t = jnp.concatenate(cols, axis=1).T
    n_slab = BRANCH_W // V7X_LANES
    for t in range(tt):
        for s in range(n_slab):
            slab[s, pl.ds(t, nb, stride=tt), :] = by_t[t * nb:(t + 1) * nb, s * V7X_LANES:(s + 1) * V7X_LANES]
    return jnp.concatenate([slab[s] for s in range(n_slab)], axis=1)


def _scan_out_lat(yf_ref, yb_ref, ysum, gy):
    nb, tt = LAT.nb, LAT.tt
    chains = nb * RWKV_HEADS
    ysum[...] = yf_ref[...] + yb_ref[...]
    for vp in range(V7X_SUBLANES):
        a = ysum[pl.ds(vp, tt, stride=V7X_SUBLANES), :].T
        for vl in range(LAT_VL):
            gy[pl.ds(vp + V7X_SUBLANES * vl, chains, stride=HEAD_DIM), :] = a[vl * chains:(vl + 1) * chains, :]
    return jnp.concatenate([gy[b * BRANCH_W:(b + 1) * BRANCH_W, :].T for b in range(nb)], axis=0)


def _mix_kernel(x_ref, mod_ref, g1_ref, oa_ref, ob_ref, od_ref, yf_ref, yb_ref, g_ref, bonus_ref,
                lnxg_ref, lnxb_ref, ones_ref, wg_ref, bg_ref, wb_ref, wo_ref, x1_ref, *scratch, path):
    nb, tt = path.nb, path.tt
    rows = nb * tt
    r0, nr = path.mod_row0, path.mod_rows

    def mod3(j):
        return mod_ref[r0:r0 + nr, j * D_MODEL:(j + 1) * D_MODEL][:, None, :]

    x3 = x_ref[...]
    h3 = _rms(x3, g1_ref[...]) * (1.0 + mod3(1)) + mod3(0)
    hb = h3.reshape(rows, D_MODEL).astype(BF16)
    y = (_scan_out_ctx if path is CTX else _scan_out_lat)(yf_ref, yb_ref, *scratch)
    ones = ones_ref[...]
    mu = _dot_hi(y, ones) * (1.0 / HEAD_DIM)
    dl = y - mu
    var = _dot_hi(dl * dl, ones) * (1.0 / HEAD_DIM)
    yn = dl * lax.rsqrt(var + GN_EPS) * lnxg_ref[...] + lnxb_ref[...]
    bonus = bonus_ref[...].reshape(rows, BRANCH_W)
    gate_c = g_ref[...].reshape(rows, BRANCH_W)
    oc = ((yn + bonus) * gate_c).astype(BF16)
    branches = (oa_ref[...].reshape(rows, BRANCH_W), ob_ref[...].reshape(rows, BRANCH_W), oc,
                od_ref[...].reshape(rows, BRANCH_W))
    mixed = None
    for n, br in enumerate(branches):
        cols = slice(n * D_MODEL, (n + 1) * D_MODEL)
        gate = jax.nn.sigmoid(_dot(hb, wg_ref[:, cols]) + bg_ref[:, cols])
        term = gate * _dot(br, wb_ref[n])
        mixed = term if mixed is None else mixed + term
    mix = _dot(mixed.astype(BF16), wo_ref[...])
    x1_ref[...] = x3 + mod3(2) * mix.reshape(nb, tt, D_MODEL)


def _mix(x, mod, p, oa, ob, od, yf, yb, g, bonus, ones, path):
    nb, tt, seq = path.nb, path.tt, path.seq
    tokw = pl.BlockSpec((nb, tt, BRANCH_W), lambda i: (0, i, 0))
    tokd = pl.BlockSpec((nb, tt, D_MODEL), lambda i: (0, i, 0))
    n_slab = BRANCH_W // V7X_LANES
    if path is CTX:
        y_specs = [pl.BlockSpec((None, tt, HEAD_DIM, V7X_LANES), lambda i, d=d: (d, i, 0, 0)) for d in range(N_DIR)]
        scratch = [pltpu.VMEM((n_slab, nb * tt, V7X_LANES), F32)]
    else:
        y_specs = [pl.BlockSpec((tt * V7X_SUBLANES, V7X_LANES), lambda i: (i, 0))] * N_DIR
        scratch = [pltpu.VMEM((tt * V7X_SUBLANES, V7X_LANES), F32),
                   pltpu.VMEM((nb * BRANCH_W, V7X_LANES), F32)]
    as3 = lambda a: a.reshape(nb, seq, a.shape[-1])
    out = pl.pallas_call(
        functools.partial(_mix_kernel, path=path),
        grid=(seq // tt,),
        in_specs=[
            tokd,
            _const_spec((MOD_ROWS, N_MOD * D_MODEL)),
            _const_spec((1, D_MODEL)),
            tokw, tokw, tokw, *y_specs, tokw, tokw,
            _const_spec((1, BRANCH_W)),
            _const_spec((1, BRANCH_W)),
            _const_spec((BRANCH_W, BRANCH_W)),
            _const_spec((D_MODEL, N_BRANCH * D_MODEL)),
            _const_spec((1, N_BRANCH * D_MODEL)),
            _const_spec((N_BRANCH, BRANCH_W, D_MODEL)),
            _const_spec((D_MODEL, D_MODEL)),
        ],
        out_specs=tokd,
        out_shape=jax.ShapeDtypeStruct((nb, seq, D_MODEL), F32),
        scratch_shapes=scratch,
        compiler_params=_cparams(1),
        name=f"branch_mix_{seq}",
    )(as3(x), mod, p["norm1_g"], as3(oa), as3(ob), as3(od), yf, yb, g, bonus, p["rwkv_lnx_g"], p["rwkv_lnx_b"],
      ones, p["w_gate"], p["b_gate"], p["w_branch"], p["w_out"])
    return out.reshape(path.n_tok, D_MODEL)


ROUTER_LANES = V7X_LANES


def _route(logits):
    lane = lax.broadcasted_iota(jnp.int32, logits.shape, 1).astype(F32)
    ninf = -jnp.inf
    big = float(ROUTER_LANES)
    gmask = lane < N_GROUPS
    gl = jnp.where(gmask, logits, ninf)
    gmax = jnp.max(gl, axis=-1, keepdims=True)
    gidx = jnp.min(jnp.where(gl == gmax, lane, big), axis=-1, keepdims=True)
    g_w = 1.0 / jnp.sum(jnp.where(gmask, jnp.exp(gl - gmax), 0.0), axis=-1, keepdims=True)
    egroup = jnp.floor((lane - N_GROUPS) * (1.0 / EXPERTS_PER_GROUP))
    emask = (lane >= N_GROUPS) & (lane < N_GROUPS + N_EXPERTS) & (egroup == gidx)
    el = jnp.where(emask, logits, ninf)
    e1 = jnp.max(el, axis=-1, keepdims=True)
    i1 = jnp.min(jnp.where(emask & (el == e1), lane, big), axis=-1, keepdims=True)
    el2 = jnp.where(lane == i1, ninf, el)
    e2 = jnp.max(el2, axis=-1, keepdims=True)
    i2 = jnp.min(jnp.where(emask & (lane != i1) & (el2 == e2), lane, big), axis=-1, keepdims=True)
    t = jnp.exp(e2 - e1)
    den = 1.0 + t
    return jnp.where(lane == i1, g_w * (1.0 / den), 0.0) + jnp.where(lane == i2, g_w * (t / den), 0.0)


def _moe_kernel(x1_ref, mod_ref, g2_ref, wr_ref, br_ref, weg_ref, weu_ref, wed_ref, fg_ref,
                *out_refs, path, final):
    row = _mod_row(pl.program_id(0), path)
    x1 = x1_ref[...]
    h2 = _rms(x1, g2_ref[...]) * (1.0 + _mod_slice(mod_ref, row, 4)) + _mod_slice(mod_ref, row, 3)
    comb = _route(_dot_hi(h2, wr_ref[...]) + br_ref[...])
    hb = h2.astype(BF16)
    gw = EXPERTS_PER_GROUP * EXPERT_FF
    acc = None
    for grp in range(N_GROUPS):
        cols = slice(grp * gw, (grp + 1) * gw)
        hg = _dot(hb, weg_ref[:, cols])
        hu = _dot(hb, weu_ref[:, cols])
        hid = hg * jax.nn.sigmoid(hg) * hu
        parts = []
        for e in range(EXPERTS_PER_GROUP):
            lane = N_GROUPS + grp * EXPERTS_PER_GROUP + e
            parts.append(hid[:, e * EXPERT_FF:(e + 1) * EXPERT_FF] * comb[:, lane:lane + 1])
        term = _dot(jnp.concatenate(parts, axis=1).astype(BF16), wed_ref[cols, :])
        acc = term if acc is None else acc + term
    x2 = x1 + _mod_slice(mod_ref, row, 5) * acc
    out_refs[0][...] = x2
    if final:
        out_refs[1][...] = _rms(x2, fg_ref[...])


def _moe(x1, mod, p, final_g, final, path):
    tm = TOK_TILE
    tokd = pl.BlockSpec((tm, D_MODEL), lambda i: (i, 0))
    ff = N_EXPERTS * EXPERT_FF
    single = pl.Buffered(1)
    out = jax.ShapeDtypeStruct((path.n_tok, D_MODEL), F32)
    return pl.pallas_call(
        functools.partial(_moe_kernel, path=path, final=final),
        grid=(path.n_tok // tm,),
        in_specs=[
            tokd,
            _const_spec((MOD_ROWS, N_MOD * D_MODEL)),
            _const_spec((1, D_MODEL)),
            _const_spec((D_MODEL, ROUTER_LANES)),
            _const_spec((1, ROUTER_LANES)),
            pl.BlockSpec((D_MODEL, ff), lambda i: (0, 0), pipeline_mode=single),
            pl.BlockSpec((D_MODEL, ff), lambda i: (0, 0), pipeline_mode=single),
            pl.BlockSpec((ff, D_MODEL), lambda i: (0, 0), pipeline_mode=single),
            _const_spec((1, D_MODEL)),
        ],
        out_specs=[tokd, tokd] if final else [tokd],
        out_shape=[out, out] if final else [out],
        compiler_params=_cparams(1),
        name=f"moe_{path.seq}" + ("_final" if final else ""),
    )(x1, mod, p["norm2_g"], p["w_router"], p["b_router"], p["w_e_gate"], p["w_e_up"], p["w_e_down"], final_g)


def _block_diag2(w):
    z = jnp.zeros_like(w[0])
    return jnp.concatenate([jnp.concatenate([w[0], z], axis=1), jnp.concatenate([z, w[1]], axis=1)], axis=0)


def _layer_params(l, a):
    row = lambda v: v.reshape(1, -1)
    perm = jnp.concatenate([a["w_in"][l][:, 0:2048], a["w_in"][l][:, 2176:2688], a["w_in"][l][:, 2048:2176]], axis=1)
    w_router = jnp.zeros((D_MODEL, ROUTER_LANES), F32)
    w_router = w_router.at[:, 0:N_GROUPS].set(a["w_rg"][l]).at[:, N_GROUPS:N_GROUPS + N_EXPERTS].set(a["w_re"][l])
    b_router = jnp.zeros((1, ROUTER_LANES), F32)
    b_router = b_router.at[0, 0:N_GROUPS].set(a["b_rg"][l]).at[0, N_GROUPS:N_GROUPS + N_EXPERTS].set(a["b_re"][l])
    ff = N_EXPERTS * EXPERT_FF
    return {
        "norm1_g": row(a["norm1_g"][l]), "norm2_g": row(a["norm2_g"][l]),
        "w_in": perm.astype(BF16),
        "gmlp_ln_g": row(a["gmlp_ln_g"][l]),
        "gmlp_wcat": a["gmlp_ws"][l].transpose(1, 0, 2).reshape(CHUNK, GMLP_GROUPS * CHUNK).astype(BF16),
        "gmlp_bsx": jnp.repeat(a["gmlp_bs"][l].T, BRANCH_W // GMLP_GROUPS, axis=1),
        "conv_w": a["conv_w"][l], "conv_b": row(a["conv_b"][l]),
        "conv_ln_g": row(a["conv_ln_g"][l]), "conv_ln_b": row(a["conv_ln_b"][l]),
        "rwkv_w0": row(a["rwkv_w0"][l]), "rwkv_w2blk": _block_diag2(a["rwkv_w2"][l]),
        "rwkv_a0": row(a["rwkv_a0"][l]), "rwkv_a2blk": _block_diag2(a["rwkv_a2"][l]),
        "rwkv_g2": a["rwkv_g2"][l], "rwkv_kk": row(a["rwkv_kk"][l]), "rwkv_ka": row(a["rwkv_ka"][l]),
        "rwkv_rk": row(a["rwkv_rk"][l]),
        "rwkv_lnx_g": row(a["rwkv_lnx_g"][l]), "rwkv_lnx_b": row(a["rwkv_lnx_b"][l]),
        "attn_sink": a["attn_sink"][l],
        "w_gate": a["w_gate"][l].astype(BF16), "b_gate": row(a["b_gate"][l]),
        "w_branch": a["w_branch"][l].astype(BF16), "w_out": a["w_out"][l].astype(BF16),
        "w_router": w_router, "b_router": b_router,
        "w_e_gate": a["w_e_gate"][l].transpose(1, 0, 2).reshape(D_MODEL, ff).astype(BF16),
        "w_e_up": a["w_e_up"][l].transpose(1, 0, 2).reshape(D_MODEL, ff).astype(BF16),
        "w_e_down": a["w_e_down"][l].reshape(ff, D_MODEL).astype(BF16),
    }


def _rope_tables():
    half = HEAD_DIM // 4
    inv = ROPE_BASE ** (-jnp.arange(half, dtype=F32) / half)
    t = jnp.arange(DEC_SEQ)
    row = (t // GRID_W).astype(F32)[:, None] * inv[None, :]
    col = (t % GRID_W).astype(F32)[:, None] * inv[None, :]
    cos_h = jnp.concatenate([jnp.cos(row), jnp.cos(row), jnp.cos(col), jnp.cos(col)], axis=1)
    sin_h = jnp.concatenate([-jnp.sin(row), jnp.sin(row), -jnp.sin(col), jnp.sin(col)], axis=1)
    reps = V7X_LANES // HEAD_DIM
    return jnp.tile(cos_h, (1, reps)), jnp.tile(sin_h, (1, reps))


def _trunk_layer(x, mod, p, ones, path, final_g, final, attn, scan):
    z = _inproj(x, mod, p["norm1_g"], p["w_in"], path)
    oa, ob = _local_mix(z, p, path)
    od = attn(z)
    g, bonus, yf, yb, s_fin = scan(z)
    x1 = _mix(x, mod, p, oa, ob, od, yf, yb, g, bonus, ones, path)
    outs = _moe(x1, mod, p, final_g, final, path)
    return outs[0], (outs[1] if final else None), z, s_fin


def kernel(x_prompt, x_sample, cache_k, cache_v, state_rwkv, c, c_ctx, norm1_g, norm2_g, final_norm_g, w_mod, b_mod, w_in, gmlp_ln_g, gmlp_ws, gmlp_bs, conv_w, conv_b, conv_ln_g, conv_ln_b, rwkv_w0, rwkv_w2, rwkv_a0, rwkv_a2, rwkv_g2, rwkv_kk, rwkv_ka, rwkv_rk, rwkv_lnx_g, rwkv_lnx_b, attn_sink, w_gate, b_gate, w_branch, w_out, w_rg, b_rg, w_re, b_re, w_e_gate, w_e_up, w_e_down):
    arrays = dict(norm1_g=norm1_g, norm2_g=norm2_g, w_in=w_in, gmlp_ln_g=gmlp_ln_g, gmlp_ws=gmlp_ws,
                  gmlp_bs=gmlp_bs, conv_w=conv_w, conv_b=conv_b, conv_ln_g=conv_ln_g, conv_ln_b=conv_ln_b,
                  rwkv_w0=rwkv_w0, rwkv_w2=rwkv_w2, rwkv_a0=rwkv_a0, rwkv_a2=rwkv_a2, rwkv_g2=rwkv_g2,
                  rwkv_kk=rwkv_kk, rwkv_ka=rwkv_ka, rwkv_rk=rwkv_rk, rwkv_lnx_g=rwkv_lnx_g,
                  rwkv_lnx_b=rwkv_lnx_b, attn_sink=attn_sink, w_gate=w_gate, b_gate=b_gate,
                  w_branch=w_branch, w_out=w_out, w_rg=w_rg, b_rg=b_rg, w_re=w_re, b_re=b_re,
                  w_e_gate=w_e_gate, w_e_up=w_e_up, w_e_down=w_e_down)
    xc = x_prompt.reshape(CTX.n_tok, D_MODEL)
    xl = x_sample.reshape(LAT.n_tok, D_MODEL)
    cvec = jnp.concatenate([c_ctx[None], c, jnp.zeros((MOD_ROWS - 1 - DEC_BATCH, D_MODEL), F32)], axis=0)
    mod_all = _modulation(cvec, w_mod, b_mod)
    ones = _head_ones()
    cos_t, sin_t = _rope_tables()
    final_g = final_norm_g.reshape(1, D_MODEL)
    kvw = KV_HEADS * HEAD_DIM
    ks_out, vs_out, ss_out = [], [], []
    yc = yl = None
    for l in range(DEPTH):
        p = _layer_params(l, arrays)
        mod = mod_all[l]
        final = l == DEPTH - 1

        def scan_ctx(z):
            g, bonus, r_s, kk_s, v_s, w_s, kd_s, b_s = _prep_ctx(z, p, ones)
            y, s_fin = _scan_ctx(r_s, kk_s, v_s, w_s, kd_s, b_s)
            return g, bonus, y, y, s_fin

        def scan_lat(z):
            g, bonus, v_s, kq = _prep_lat(z, p, ones)
            s0 = state_rwkv[:, l].reshape(DEC_BATCH, N_DIR, RWKV_HEADS, LAT_VL, V7X_SUBLANES, HEAD_DIM)
            s0 = s0.transpose(1, 5, 4, 3, 0, 2).reshape(N_DIR, HEAD_DIM, V7X_SUBLANES, V7X_LANES)
            yf, yb = _scan_lat(kq, v_s, s0)
            return g, bonus, yf, yb, None

        ck = cache_k[:, l].reshape(DEC_BATCH, PAST_LEN, kvw)
        cv = cache_v[:, l].reshape(DEC_BATCH, PAST_LEN, kvw)
        xc, yc_l, zc, s_fin = _trunk_layer(
            xc, mod, p, ones, CTX, final_g, final, lambda z: _ctx_attn(z, p["attn_sink"]), scan_ctx)
        xl, yl_l, _, _ = _trunk_layer(
            xl, mod, p, ones, LAT, final_g, final,
            lambda z: _lat_attn(z, p["attn_sink"], ck, cv, cos_t, sin_t), scan_lat)
        if final:
            yc, yl = yc_l, yl_l
        k_off = Z_D + ATT_HEADS * HEAD_DIM
        ks_out.append(zc[:, k_off:k_off + kvw].reshape(BATCH, SEQ, KV_HEADS, HEAD_DIM))
        vs_out.append(zc[:, k_off + kvw:k_off + 2 * kvw].reshape(BATCH, SEQ, KV_HEADS, HEAD_DIM))
        ss_out.append(s_fin.reshape(N_DIR, HEAD_DIM, HEAD_DIM, RWKV_HEADS, BATCH).transpose(4, 0, 3, 2, 1))
    y_prompt = yc.reshape(BATCH, SEQ, D_MODEL)
    y_sample = yl.reshape(DEC_BATCH, DEC_SEQ, D_MODEL)
    return (y_prompt, y_sample, jnp.stack(ks_out, axis=1), jnp.stack(vs_out, axis=1), jnp.stack(ss_out, axis=1))
```

```python
import functools
import math
from typing import NamedTuple

import jax
import jax.numpy as jnp
from jax import lax
from jax.experimental import pallas as pl
from jax.experimental.pallas import tpu as pltpu

D_MODEL = 1024
BATCH = 32
SEQ = 256
DEPTH = 2
DEC_BATCH = 4
DEC_SEQ = 1024
PAST_LEN = 256
GRID_W = 64
HEAD_DIM = 64
BRANCH_W = 256
N_BRANCH = 4
CHUNK = 128
GMLP_GROUPS = 4
CONV_W = 31
RWKV_HEADS = 4
N_DIR = 2
DECAY_LORA = 64
ICL_LORA = 64
GATE_LORA = 128
ATT_HEADS = 4
KV_HEADS = 2
Q_PER_KV = ATT_HEADS // KV_HEADS
WINDOW = 128
BLOCK = 128
ROPE_BASE = 10000.0
ATT_SCALE = HEAD_DIM ** -0.5
N_GROUPS = 4
EXPERTS_PER_GROUP = 4
N_EXPERTS = N_GROUPS * EXPERTS_PER_GROUP
EXPERT_FF = 256
N_MOD = 6
RMS_EPS = 1e-6
LN_EPS = 1e-5
GN_EPS = 64e-5
IN_COLS = 2688

V7X_LANES = 128
V7X_SUBLANES = 8
V7X_VMEM_LIMIT = 56 * 1024 * 1024

MOD_ROWS = 8
TOK_TILE = 512
SCAN_TB = 32
CONV_PAD = 16

Z_AB, Z_C, Z_D, Z_G = 0, 1024, 2048, 2560

LAT_Q_R, LAT_Q_KK, LAT_Q_W, LAT_Q_KD, LAT_Q_B = 0, 1, 2, 4, 6
LAT_NQ = 8
LAT_VL = V7X_LANES // (DEC_BATCH * RWKV_HEADS)

F32 = jnp.float32
BF16 = jnp.bfloat16
HIGHEST = lax.Precision.HIGHEST


class _Path(NamedTuple):
    nb: int
    seq: int
    mod_row0: int
    mod_rows: int
    tt: int

    @property
    def n_tok(self):
        return self.nb * self.seq


CTX = _Path(BATCH, SEQ, 0, 1, TOK_TILE // BATCH)
LAT = _Path(DEC_BATCH, DEC_SEQ, 1, DEC_BATCH, TOK_TILE // DEC_BATCH)


def _cparams(n_axes, vmem=V7X_VMEM_LIMIT):
    return pltpu.CompilerParams(dimension_semantics=("arbitrary",) * n_axes, vmem_limit_bytes=vmem)


def _const_spec(shape):
    nd = len(shape)
    return pl.BlockSpec(shape, lambda *_: (0,) * nd)


def _dot(a, b):
    return jnp.dot(a, b, preferred_element_type=F32)


def _dot_hi(a, b):
    return jnp.dot(a, b, preferred_element_type=F32, precision=HIGHEST)


def _split(x):
    hi = x.astype(BF16)
    return hi, (x - hi.astype(F32)).astype(BF16)


def _dot_split(a, b_hi, b_lo):
    a_hi, a_lo = _split(a)
    return _dot(a_hi, b_hi) + (_dot(a_lo, b_hi) + _dot(a_hi, b_lo))


def _head_sum(x, ones_b):
    hi, lo = _split(x)
    return _dot(hi, ones_b) + _dot(lo, ones_b)


def _head_ones():
    r = lax.broadcasted_iota(jnp.int32, (BRANCH_W, BRANCH_W), 0) // HEAD_DIM
    c = lax.broadcasted_iota(jnp.int32, (BRANCH_W, BRANCH_W), 1) // HEAD_DIM
    return (r == c).astype(BF16)


def _mod_row(tile, path):
    if path.mod_rows == 1:
        return path.mod_row0
    return path.mod_row0 + tile // (path.seq // TOK_TILE)


def _mod_slice(mod_ref, row, j):
    return mod_ref[pl.ds(row, 1), j * D_MODEL:(j + 1) * D_MODEL]


def _rms(x, g):
    return x * lax.rsqrt(jnp.mean(x * x, axis=-1, keepdims=True) + RMS_EPS) * g


def _layernorm(x, g, b=None, eps=LN_EPS):
    mu = jnp.mean(x, axis=-1, keepdims=True)
    d = x - mu
    var = jnp.mean(d * d, axis=-1, keepdims=True)
    y = d * lax.rsqrt(var + eps) * g
    return y if b is None else y + b


def _mod_kernel(c_ref, w_ref, b_ref, o_ref):
    c = c_ref[...]
    a = c * jax.nn.sigmoid(c)
    o_ref[0] = _dot_hi(a, w_ref[0]) + b_ref[0]


def _modulation(cvec, w_mod, b_mod):
    return pl.pallas_call(
        _mod_kernel,
        grid=(DEPTH, N_MOD),
        in_specs=[
            pl.BlockSpec((MOD_ROWS, D_MODEL), lambda l, j: (0, 0)),
            pl.BlockSpec((1, D_MODEL, D_MODEL), lambda l, j: (l, 0, j)),
            pl.BlockSpec((1, 1, D_MODEL), lambda l, j: (l, 0, j)),
        ],
        out_specs=pl.BlockSpec((1, MOD_ROWS, D_MODEL), lambda l, j: (l, 0, j)),
        out_shape=jax.ShapeDtypeStruct((DEPTH, MOD_ROWS, N_MOD * D_MODEL), F32),
        compiler_params=_cparams(2),
        name="modulation",
    )(cvec, w_mod, b_mod.reshape(DEPTH, 1, N_MOD * D_MODEL))


def _inproj_kernel(x_ref, mod_ref, g_ref, w_ref, z_ref, *kv_refs, path):
    row = _mod_row(pl.program_id(0), path)
    h = _rms(x_ref[...], g_ref[...]) * (1.0 + _mod_slice(mod_ref, row, 1)) + _mod_slice(mod_ref, row, 0)
    z = _dot(h.astype(BF16), w_ref[...])
    z_ref[...] = z
    kvw = KV_HEADS * HEAD_DIM
    for j, ref in enumerate(kv_refs):
        off = Z_D + ATT_HEADS * HEAD_DIM + j * kvw
        ref[...] = z[:, off:off + kvw]


def _inproj(x, mod, g1, w_in_b, path):
    tm = TOK_TILE
    kvw = KV_HEADS * HEAD_DIM
    n_kv = 2 if path is CTX else 0
    return pl.pallas_call(
        functools.partial(_inproj_kernel, path=path),
        grid=(path.n_tok // tm,),
        in_specs=[
            pl.BlockSpec((tm, D_MODEL), lambda i: (i, 0)),
            _const_spec((MOD_ROWS, N_MOD * D_MODEL)),
            _const_spec((1, D_MODEL)),
            _const_spec((D_MODEL, IN_COLS)),
        ],
        out_specs=[pl.BlockSpec((tm, IN_COLS), lambda i: (i, 0))] + [pl.BlockSpec((tm, kvw), lambda i: (i, 0))] * n_kv,
        out_shape=[jax.ShapeDtypeStruct((path.n_tok, IN_COLS), F32)]
        + [jax.ShapeDtypeStruct((path.n_tok, kvw), F32)] * n_kv,
        compiler_params=_cparams(1),
        name=f"inproj_{path.seq}",
    )(x, mod, g1, w_in_b)


def _local_mix_kernel(z_ref, lng_ref, wcat_ref, bsx_ref, cw_ref, cb_ref, clg_ref, clb_ref,
                      oa_ref, ob_ref, ypad, *, seq_len):
    lane_group = lax.broadcasted_iota(jnp.int32, (CHUNK, BRANCH_W), 1) // (BRANCH_W // GMLP_GROUPS)
    for c in range(seq_len // CHUNK):
        rows = pl.ds(c * CHUNK, CHUNK)
        u = jax.nn.gelu(z_ref[rows, 0:BRANCH_W], approximate=True)
        v = jax.nn.gelu(z_ref[rows, BRANCH_W:2 * BRANCH_W], approximate=True)
        vn = _layernorm(v, lng_ref[...])
        vblk = jnp.concatenate(
            [jnp.where(lane_group == g, vn, 0.0) for g in range(GMLP_GROUPS)], axis=0).astype(BF16)
        mixed = _dot(wcat_ref[...], vblk) + bsx_ref[...]
        oa_ref[rows, :] = (u * mixed).astype(BF16)

    zeros = jnp.zeros((CONV_PAD, BRANCH_W), F32)
    ypad[0:CONV_PAD, :] = zeros
    ypad[CONV_PAD + seq_len:2 * CONV_PAD + seq_len, :] = zeros
    ypad[CONV_PAD:CONV_PAD + seq_len, :] = (
        z_ref[:, 2 * BRANCH_W:3 * BRANCH_W] * jax.nn.sigmoid(z_ref[:, 3 * BRANCH_W:4 * BRANCH_W]))
    base = CONV_PAD - CONV_W // 2
    for c in range(seq_len // CHUNK):
        acc = jnp.zeros((CHUNK, BRANCH_W), F32)
        for j in range(CONV_W):
            acc = acc + cw_ref[j:j + 1, :] * ypad[pl.ds(c * CHUNK + base + j, CHUNK), :]
        y = _layernorm(acc + cb_ref[...], clg_ref[...], clb_ref[...])
        ob_ref[pl.ds(c * CHUNK, CHUNK), :] = (y * jax.nn.sigmoid(y)).astype(BF16)


def _local_mix(z, p, path):
    out = jax.ShapeDtypeStruct((path.n_tok, BRANCH_W), BF16)
    return pl.pallas_call(
        functools.partial(_local_mix_kernel, seq_len=path.seq),
        grid=(path.nb,),
        in_specs=[
            pl.BlockSpec((path.seq, 4 * BRANCH_W), lambda i: (i, Z_AB // (4 * BRANCH_W))),
            _const_spec((1, BRANCH_W)),
            _const_spec((CHUNK, GMLP_GROUPS * CHUNK)),
            _const_spec((CHUNK, BRANCH_W)),
            _const_spec((CONV_W, BRANCH_W)),
            _const_spec((1, BRANCH_W)),
            _const_spec((1, BRANCH_W)),
            _const_spec((1, BRANCH_W)),
        ],
        out_specs=[pl.BlockSpec((path.seq, BRANCH_W), lambda i: (i, 0))] * 2,
        out_shape=[out, out],
        scratch_shapes=[pltpu.VMEM((path.seq + 2 * CONV_PAD, BRANCH_W), F32)],
        compiler_params=_cparams(1),
        name=f"local_mix_{path.seq}",
    )(z, p["gmlp_ln_g"], p["gmlp_wcat"], p["gmlp_bsx"], p["conv_w"], p["conv_b"],
      p["conv_ln_g"], p["conv_ln_b"])


def _rwkv_quantities(zc, gd, w0_ref, w2_ref, a0_ref, a2_ref, g2_ref, kkp_ref, ka_ref, rk_ref, ones):
    W = BRANCH_W
    r = zc[:, 0:W]
    k = zc[:, W:2 * W]
    v = zc[:, 2 * W:3 * W]
    wd = zc[:, 3 * W:3 * W + N_DIR * DECAY_LORA]
    ad = zc[:, 3 * W + N_DIR * DECAY_LORA:4 * W]
    w_raw = _dot_split(jnp.tanh(wd), w2_ref[0], w2_ref[1]) + w0_ref[...]
    decay = jnp.exp(-math.exp(-0.5) * jax.nn.sigmoid(w_raw))
    a = jax.nn.sigmoid(_dot_split(ad, a2_ref[0], a2_ref[1]) + a0_ref[...])
    g = _dot_split(jax.nn.sigmoid(gd), g2_ref[0], g2_ref[1])
    kkr = k * kkp_ref[...]
    kk = kkr / jnp.maximum(jnp.sqrt(_head_sum(kkr * kkr, ones)), 1e-12)
    ka = ka_ref[...]
    w, kd, b = [], [], []
    for d in range(N_DIR):
        a_d = a[:, d * W:(d + 1) * W]
        w.append(decay[:, d * W:(d + 1) * W])
        kd.append(k * (1.0 + (a_d - 1.0) * ka))
        b.append(kk * a_d)
    bonus = _head_sum(r * (kd[0] + kd[1]) * rk_ref[...], ones) * v
    return g, bonus, r, kk, v, w, kd, b


def _prep_ctx_kernel(zc_ref, zg_ref, w0_ref, w2_ref, a0_ref, a2_ref, g2_ref, kkp_ref, ka_ref, rk_ref,
                     ones_ref, g_o, bonus_o, r_o, kk_o, v_o, w_o, kd_o, b_o, slab):
    nb, tt = CTX.nb, CTX.tt
    rows = nb * tt
    zc = zc_ref[...].reshape(rows, 4 * BRANCH_W)
    gd = zg_ref[...].reshape(rows, GATE_LORA)
    g, bonus, r, kk, v, w, kd, b = _rwkv_quantities(
        zc, gd, w0_ref, w2_ref, a0_ref, a2_ref, g2_ref, kkp_ref, ka_ref, rk_ref, ones_ref[...])
    g_o[...] = g.reshape(nb, tt, BRANCH_W)
    bonus_o[...] = bonus.reshape(nb, tt, BRANCH_W)
    n_slab = BRANCH_W // V7X_LANES
    lanes_per_head = V7X_LANES // RWKV_HEADS

    def to_chains(q, put):
        for s in range(n_slab):
            slab[s] = q[:, s * V7X_LANES:(s + 1) * V7X_LANES]
        by_t = jnp.concatenate(
            [jnp.concatenate([slab[s, pl.ds(t, nb, stride=tt), :] for s in range(n_slab)], axis=1)
             for t in range(tt)], axis=0)
        tr = by_t.T
        lane = lax.broadcasted_iota(jnp.int32, (HEAD_DIM, V7X_LANES), 1)
        low_half = lane < V7X_LANES // 2
        even_quarter = (lane // nb) % 2 == 0
        roll = lambda a, s: pltpu.roll(a, s, 1)
        for c in range(tt // RWKV_HEADS):
            x0, x1, x2, x3 = (tr[h * HEAD_DIM:(h + 1) * HEAD_DIM, c * V7X_LANES:(c + 1) * V7X_LANES]
                              for h in range(RWKV_HEADS))
            y0 = jnp.where(low_half, x0, roll(x2, 2 * nb))
            y2 = jnp.where(low_half, roll(x0, 2 * nb), x2)
            y1 = jnp.where(low_half, x1, roll(x3, 2 * nb))
            y3 = jnp.where(low_half, roll(x1, 2 * nb), x3)
            put(4 * c + 0, jnp.where(even_quarter, y0, roll(y1, nb)))
            put(4 * c + 1, jnp.where(even_quarter, roll(y0, 3 * nb), y1))
            put(4 * c + 2, jnp.where(even_quarter, y2, roll(y3, nb)))
            put(4 * c + 3, jnp.where(even_quarter, roll(y2, 3 * nb), y3))
    assert lanes_per_head == nb and RWKV_HEADS == 4

    def put_into(ref, *lead):
        def put(t, tile):
            ref[(*lead, t)] = tile
        return put

    to_chains(r, put_into(r_o))
    to_chains(kk, put_into(kk_o))
    to_chains(v, put_into(v_o))
    for d in range(N_DIR):
        to_chains(w[d], put_into(w_o, d))
        to_chains(kd[d], put_into(kd_o, d))
        to_chains(b[d], put_into(b_o, d))


def _prep_lat_kernel(zc_ref, zg_ref, w0_ref, w2_ref, a0_ref, a2_ref, g2_ref, kkp_ref, ka_ref, rk_ref,
                     ones_ref, g_o, bonus_o, v_o, kq_o, gk, gv):
    nb, tt = LAT.nb, LAT.tt
    rows = nb * tt
    qi = pl.program_id(1)
    chains = nb * RWKV_HEADS

    @pl.when(qi == 0)
    def _():
        zc = zc_ref[...].reshape(rows, 4 * BRANCH_W)
        gd = zg_ref[...].reshape(rows, GATE_LORA)
        g, bonus, r, kk, v, w, kd, b = _rwkv_quantities(
            zc, gd, w0_ref, w2_ref, a0_ref, a2_ref, g2_ref, kkp_ref, ka_ref, rk_ref, ones_ref[...])
        g_o[...] = g.reshape(nb, tt, BRANCH_W)
        bonus_o[...] = bonus.reshape(nb, tt, BRANCH_W)
        k_indexed = {LAT_Q_R: r, LAT_Q_KK: kk, LAT_Q_W: w[0], LAT_Q_W + 1: w[1],
                     LAT_Q_KD: kd[0], LAT_Q_KD + 1: kd[1], LAT_Q_B: b[0], LAT_Q_B + 1: b[1]}
        for bi in range(nb):
            for q_idx, q in k_indexed.items():
                gk[q_idx, bi * BRANCH_W:(bi + 1) * BRANCH_W, :] = q[bi * tt:(bi + 1) * tt, :].T
            gv[bi * BRANCH_W:(bi + 1) * BRANCH_W, :] = v[bi * tt:(bi + 1) * tt, :].T
        for vp in range(V7X_SUBLANES):
            a = jnp.concatenate(
                [gv[pl.ds(vp + V7X_SUBLANES * vl, chains, stride=HEAD_DIM), :] for vl in range(LAT_VL)], axis=0)
            v_o[pl.ds(vp, tt, stride=V7X_SUBLANES), :] = a.T

    def body(k, carry):
        a = gk[qi, pl.ds(k, chains, stride=HEAD_DIM), :]
        n8 = V7X_SUBLANES
        kq_o[k // n8, pl.ds(k % n8, tt, stride=n8), :] = jnp.concatenate([a] * LAT_VL, axis=0).T
        return carry
    lax.fori_loop(0, HEAD_DIM, body, 0, unroll=8)


def _prep_param_specs():
    return [
        _const_spec((1, N_DIR * BRANCH_W)),
        _const_spec((2, N_DIR * DECAY_LORA, N_DIR * BRANCH_W)),
        _const_spec((1, N_DIR * BRANCH_W)),
        _const_spec((2, N_DIR * ICL_LORA, N_DIR * BRANCH_W)),
        _const_spec((2, GATE_LORA, BRANCH_W)),
        _const_spec((1, BRANCH_W)),
        _const_spec((1, BRANCH_W)),
        _const_spec((1, BRANCH_W)),
        _const_spec((BRANCH_W, BRANCH_W)),
    ]


def _prep_params(p, ones):
    return (p["rwkv_w0"], p["rwkv_w2blk"], p["rwkv_a0"], p["rwkv_a2blk"], p["rwkv_g2"],
            p["rwkv_kk"], p["rwkv_ka"], p["rwkv_rk"], ones)


def _prep_ctx(z, p, ones):
    nb, tt, seq = CTX.nb, CTX.tt, CTX.seq
    z3 = z.reshape(nb, seq, IN_COLS)
    tok = jax.ShapeDtypeStruct((nb, seq, BRANCH_W), F32)
    tok_spec = pl.BlockSpec((nb, tt, BRANCH_W), lambda i: (0, i, 0))
    ch = jax.ShapeDtypeStruct((seq, HEAD_DIM, V7X_LANES), F32)
    ch2 = jax.ShapeDtypeStruct((N_DIR, seq, HEAD_DIM, V7X_LANES), F32)
    ch_spec = pl.BlockSpec((tt, HEAD_DIM, V7X_LANES), lambda i: (i, 0, 0))
    ch2_spec = pl.BlockSpec((N_DIR, tt, HEAD_DIM, V7X_LANES), lambda i: (0, i, 0, 0))
    return pl.pallas_call(
        _prep_ctx_kernel,
        grid=(seq // tt,),
        in_specs=[
            pl.BlockSpec((nb, tt, 4 * BRANCH_W), lambda i: (0, i, Z_C // (4 * BRANCH_W))),
            pl.BlockSpec((nb, tt, GATE_LORA), lambda i: (0, i, Z_G // GATE_LORA)),
        ] + _prep_param_specs(),
        out_specs=[tok_spec, tok_spec, ch_spec, ch_spec, ch_spec, ch2_spec, ch2_spec, ch2_spec],
        out_shape=[tok, tok, ch, ch, ch, ch2, ch2, ch2],
        scratch_shapes=[pltpu.VMEM((BRANCH_W // V7X_LANES, nb * tt, V7X_LANES), F32)],
        compiler_params=_cparams(1),
        name="rwkv_prep_ctx",
    )(z3, z3, *_prep_params(p, ones))


def _prep_lat(z, p, ones):
    nb, tt, seq = LAT.nb, LAT.tt, LAT.seq
    z3 = z.reshape(nb, seq, IN_COLS)
    tok = jax.ShapeDtypeStruct((nb, seq, BRANCH_W), F32)
    tok_spec = pl.BlockSpec((nb, tt, BRANCH_W), lambda i, q: (0, i, 0))
    return pl.pallas_call(
        _prep_lat_kernel,
        grid=(seq // tt, LAT_NQ),
        in_specs=[
            pl.BlockSpec((nb, tt, 4 * BRANCH_W), lambda i, q: (0, i, Z_C // (4 * BRANCH_W))),
            pl.BlockSpec((nb, tt, GATE_LORA), lambda i, q: (0, i, Z_G // GATE_LORA)),
        ] + _prep_param_specs(),
        out_specs=[
            tok_spec, tok_spec,
            pl.BlockSpec((tt * V7X_SUBLANES, V7X_LANES), lambda i, q: (i, 0)),
            pl.BlockSpec((None, HEAD_DIM // V7X_SUBLANES, tt * V7X_SUBLANES, V7X_LANES), lambda i, q: (q, 0, i, 0)),
        ],
        out_shape=[tok, tok,
                   jax.ShapeDtypeStruct((seq * V7X_SUBLANES, V7X_LANES), F32),
                   jax.ShapeDtypeStruct((LAT_NQ, HEAD_DIM // V7X_SUBLANES, seq * V7X_SUBLANES, V7X_LANES), F32)],
        scratch_shapes=[pltpu.VMEM((LAT_NQ, nb * BRANCH_W, V7X_LANES), F32),
                        pltpu.VMEM((nb * BRANCH_W, V7X_LANES), F32)],
        compiler_params=_cparams(2),
        name="rwkv_prep_lat",
    )(z3, z3, *_prep_params(p, ones))


SCAN_ACCS = 4


def _strided_sum(terms):
    acc = [None] * SCAN_ACCS
    for j, x in enumerate(terms):
        a = j % SCAN_ACCS
        acc[a] = x if acc[a] is None else acc[a] + x
    while len(acc) > 1:
        acc = [acc[j] + acc[j + 1] for j in range(0, len(acc), 2)]
    return acc[0]


def _scan_ctx_kernel(r_ref, kk_ref, v_ref, w_ref, kd_ref, b_ref, y_ref, sfin_ref, S, *, tb):
    d = pl.program_id(0)
    i = pl.program_id(1)
    n8 = V7X_SUBLANES
    n_v8 = HEAD_DIM // n8

    @pl.when(i == 0)
    def _():
        S[...] = jnp.zeros_like(S)

    def row8(ref, t, k):
        return jnp.broadcast_to(ref[t, pl.ds(k, 1), :], (n8, V7X_LANES))

    def step(s, carry):
        t = s + d * (tb - 1 - 2 * s)

        def pass1(k, sa):
            kkb = row8(kk_ref, t, k)
            return tuple(sa[vo] + S[k, vo * n8:(vo + 1) * n8, :] * kkb for vo in range(n_v8))

        zero = tuple(jnp.zeros((n8, V7X_LANES), F32) for _ in range(n_v8))
        sa = lax.fori_loop(0, HEAD_DIM, pass1, zero, unroll=16)
        nsa = tuple(-x for x in sa)
        vt = tuple(v_ref[t, vo * n8:(vo + 1) * n8, :] for vo in range(n_v8))

        def pass2(k, y):
            wb = row8(w_ref, t, k)
            bb = row8(b_ref, t, k)
            kb = row8(kd_ref, t, k)
            rb = row8(r_ref, t, k)
            out = []
            for vo in range(n_v8):
                sl = slice(vo * n8, (vo + 1) * n8)
                sn = S[k, sl, :] * wb + (nsa[vo] * bb + vt[vo] * kb)
                S[k, sl, :] = sn
                out.append(y[vo] + sn * rb)
            return tuple(out)

        y = lax.fori_loop(0, HEAD_DIM, pass2, zero, unroll=16)
        for vo in range(n_v8):
            y_ref[t, vo * n8:(vo + 1) * n8, :] = y[vo]
        return carry

    lax.fori_loop(0, tb, step, 0)

    @pl.when(i == pl.num_programs(1) - 1)
    def _():
        sfin_ref[...] = S[...]


def _scan_ctx(r_s, kk_s, v_s, w_s, kd_s, b_s):
    seq, tb = CTX.seq, SCAN_TB
    nt = seq // tb

    def tm(d, i):
        return i + d * (nt - 1 - 2 * i)
    shared = pl.BlockSpec((tb, HEAD_DIM, V7X_LANES), lambda d, i: (tm(d, i), 0, 0))
    per_dir = pl.BlockSpec((None, tb, HEAD_DIM, V7X_LANES), lambda d, i: (d, tm(d, i), 0, 0))
    state = pl.BlockSpec((None, HEAD_DIM, HEAD_DIM, V7X_LANES), lambda d, i: (d, 0, 0, 0))
    return pl.pallas_call(
        functools.partial(_scan_ctx_kernel, tb=tb),
        grid=(N_DIR, nt),
        in_specs=[shared, shared, shared, per_dir, per_dir, per_dir],
        out_specs=[per_dir, state],
        out_shape=[jax.ShapeDtypeStruct((N_DIR, seq, HEAD_DIM, V7X_LANES), F32),
                   jax.ShapeDtypeStruct((N_DIR, HEAD_DIM, HEAD_DIM, V7X_LANES), F32)],
        scratch_shapes=[pltpu.VMEM((HEAD_DIM, HEAD_DIM, V7X_LANES), F32)],
        compiler_params=_cparams(2),
        name="rwkv_scan_ctx",
    )(r_s, kk_s, v_s, w_s, kd_s, b_s)


def _scan_lat_kernel(*refs, tb):
    fwd, bwd = refs[0:6], refs[6:12]
    s0_ref, yf_ref, yb_ref, S = refs[12:]
    n8 = V7X_SUBLANES

    @pl.when(pl.program_id(0) == 0)
    def _():
        S[...] = s0_ref[...]

    def one_dir(d, t, r_ref, kk_ref, v_ref, w_ref, kd_ref, b_ref, y_ref):
        def row8(ref, k):
            return jnp.broadcast_to(ref[k // n8, pl.ds(t * n8 + k % n8, 1), :], (n8, V7X_LANES))
        rows = pl.ds(pl.multiple_of(t * n8, n8), n8)
        nsa = -_strided_sum(S[d, k] * row8(kk_ref, k) for k in range(HEAD_DIM))
        vt = v_ref[rows, :]

        def update(k):
            sn = S[d, k] * row8(w_ref, k) + (nsa * row8(b_ref, k) + vt * row8(kd_ref, k))
            S[d, k] = sn
            return sn * row8(r_ref, k)
        y_ref[rows, :] = _strided_sum(update(k) for k in range(HEAD_DIM))

    def step(s, carry):
        one_dir(0, s, *fwd, yf_ref)
        one_dir(1, tb - 1 - s, *bwd, yb_ref)
        return carry

    lax.fori_loop(0, tb, step, 0)


def _scan_lat(kq, v_s, s0):
    seq, tb = LAT.seq, SCAN_TB
    nt = seq // tb
    n8 = V7X_SUBLANES

    def specs(d):
        tblk = (lambda i: i) if d == 0 else (lambda i: nt - 1 - i)
        plane = lambda q: pl.BlockSpec((None, HEAD_DIM // n8, tb * n8, V7X_LANES), lambda i: (q, 0, tblk(i), 0))
        rows = pl.BlockSpec((tb * n8, V7X_LANES), lambda i: (tblk(i), 0))
        return [plane(LAT_Q_R), plane(LAT_Q_KK), rows,
                plane(LAT_Q_W + d), plane(LAT_Q_KD + d), plane(LAT_Q_B + d)], rows
    in_f, y_f = specs(0)
    in_b, y_b = specs(1)
    y_shape = jax.ShapeDtypeStruct((seq * n8, V7X_LANES), F32)
    one = (kq, kq, v_s, kq, kq, kq)
    return pl.pallas_call(
        functools.partial(_scan_lat_kernel, tb=tb),
        grid=(nt,),
        in_specs=in_f + in_b + [_const_spec((N_DIR, HEAD_DIM, n8, V7X_LANES))],
        out_specs=[y_f, y_b],
        out_shape=[y_shape, y_shape],
        scratch_shapes=[pltpu.VMEM((N_DIR, HEAD_DIM, n8, V7X_LANES), F32)],
        compiler_params=_cparams(1),
        name="rwkv_scan_lat",
    )(*one, *one, s0)


def _softmax_pv(scores, vals, sink_col):
    m = sink_col
    for s in scores:
        m = jnp.maximum(m, jnp.max(s, axis=-1, keepdims=True))
    den = jnp.exp(sink_col - m)
    out = None
    for s, vx in zip(scores, vals):
        p = jnp.exp(s - m)
        den = den + jnp.sum(p, axis=-1, keepdims=True)
        o = _dot(p.astype(BF16), vx)
        out = o if out is None else out + o
    return out / den


def _sink_col(sink_ref, kv, n_rows):
    row = lax.broadcasted_iota(jnp.int32, (Q_PER_KV * n_rows, 1), 0)
    col = jnp.full((Q_PER_KV * n_rows, 1), sink_ref[kv * Q_PER_KV], F32)
    for g in range(1, Q_PER_KV):
        col = jnp.where(row >= g * n_rows, sink_ref[kv * Q_PER_KV + g], col)
    return col


def _qk(q2, kh):
    return lax.dot_general(q2, kh, (((1,), (1,)), ((), ())), preferred_element_type=F32) * ATT_SCALE


def _ctx_attn_kernel(sink_ref, z_ref, o_ref):
    hd = HEAD_DIM
    q = z_ref[:, 0:ATT_HEADS * hd]
    outs = []
    for kv in range(KV_HEADS):
        kh = z_ref[:, ATT_HEADS * hd + kv * hd:ATT_HEADS * hd + (kv + 1) * hd].astype(BF16)
        vh = z_ref[:, (ATT_HEADS + KV_HEADS) * hd + kv * hd:(ATT_HEADS + KV_HEADS) * hd + (kv + 1) * hd].astype(BF16)
        q2 = jnp.concatenate(
            [q[:, (kv * Q_PER_KV + g) * hd:(kv * Q_PER_KV + g + 1) * hd] for g in range(Q_PER_KV)],
            axis=0).astype(BF16)
        o2 = _softmax_pv([_qk(q2, kh)], [vh], _sink_col(sink_ref, kv, SEQ))
        outs += [o2[g * SEQ:(g + 1) * SEQ] for g in range(Q_PER_KV)]
    o_ref[...] = jnp.concatenate(outs, axis=1).astype(BF16)


def _rope(x, cos, sin_signed):
    lane = lax.broadcasted_iota(jnp.int32, x.shape, 1)
    first = (lane % (HEAD_DIM // 2)) < (HEAD_DIM // 4)
    partner = jnp.where(first, pltpu.roll(x, V7X_LANES - HEAD_DIM // 4, 1), pltpu.roll(x, HEAD_DIM // 4, 1))
    return x * cos + partner * sin_signed


def _lat_attn_kernel(sink_ref, z_ref, ck_ref, cv_ref, cos_ref, sin_ref, o_ref, q_s, k_s):
    hd = HEAD_DIM
    cos = cos_ref[...]
    sin = sin_ref[...]
    for j in range(ATT_HEADS * hd // V7X_LANES):
        q_s[:, j * V7X_LANES:(j + 1) * V7X_LANES] = _rope(
            z_ref[:, j * V7X_LANES:(j + 1) * V7X_LANES], cos, sin).astype(BF16)
    k_s[...] = _rope(z_ref[:, ATT_HEADS * hd:(ATT_HEADS + KV_HEADS) * hd], cos, sin).astype(BF16)
    voff = (ATT_HEADS + KV_HEADS) * hd
    nb = DEC_SEQ // BLOCK
    for n in range(nb):
        lo = max(n - 1, 0) * BLOCK
        hi = min(n + 2, nb) * BLOCK
        i_abs = n * BLOCK + lax.broadcasted_iota(jnp.int32, (Q_PER_KV * BLOCK, hi - lo), 0) % BLOCK
        j_abs = lo + lax.broadcasted_iota(jnp.int32, (Q_PER_KV * BLOCK, hi - lo), 1)
        band = jnp.abs(i_abs - j_abs) <= WINDOW
        outs = []
        for kv in range(KV_HEADS):
            q2 = jnp.concatenate(
                [q_s[n * BLOCK:(n + 1) * BLOCK, (kv * Q_PER_KV + g) * hd:(kv * Q_PER_KV + g + 1) * hd]
                 for g in range(Q_PER_KV)], axis=0)
            kw = k_s[lo:hi, kv * hd:(kv + 1) * hd]
            vw = z_ref[lo:hi, voff + kv * hd:voff + (kv + 1) * hd].astype(BF16)
            kc = ck_ref[:, kv * hd:(kv + 1) * hd].astype(BF16)
            vc = cv_ref[:, kv * hd:(kv + 1) * hd].astype(BF16)
            s_lat = jnp.where(band, _qk(q2, kw), -1e30)
            o2 = _softmax_pv([s_lat, _qk(q2, kc)], [vw, vc], _sink_col(sink_ref, kv, BLOCK))
            outs += [o2[g * BLOCK:(g + 1) * BLOCK] for g in range(Q_PER_KV)]
        o_ref[n * BLOCK:(n + 1) * BLOCK, :] = jnp.concatenate(outs, axis=1).astype(BF16)


def _ctx_attn(z, sink):
    wd = (ATT_HEADS + 2 * KV_HEADS) * HEAD_DIM
    return pl.pallas_call(
        _ctx_attn_kernel,
        grid=(BATCH,),
        in_specs=[
            pl.BlockSpec(memory_space=pltpu.SMEM),
            pl.BlockSpec((SEQ, wd), lambda i: (i, Z_D // wd)),
        ],
        out_specs=pl.BlockSpec((SEQ, BRANCH_W), lambda i: (i, 0)),
        out_shape=jax.ShapeDtypeStruct((CTX.n_tok, BRANCH_W), BF16),
        compiler_params=_cparams(1),
        name="ctx_attn",
    )(sink, z)


def _lat_attn(z, sink, ck, cv, cos_t, sin_t):
    wd = (ATT_HEADS + 2 * KV_HEADS) * HEAD_DIM
    kvw = KV_HEADS * HEAD_DIM
    return pl.pallas_call(
        _lat_attn_kernel,
        grid=(DEC_BATCH,),
        in_specs=[
            pl.BlockSpec(memory_space=pltpu.SMEM),
            pl.BlockSpec((DEC_SEQ, wd), lambda i: (i, Z_D // wd)),
            pl.BlockSpec((None, PAST_LEN, kvw), lambda i: (i, 0, 0)),
            pl.BlockSpec((None, PAST_LEN, kvw), lambda i: (i, 0, 0)),
            _const_spec((DEC_SEQ, V7X_LANES)),
            _const_spec((DEC_SEQ, V7X_LANES)),
        ],
        out_specs=pl.BlockSpec((DEC_SEQ, BRANCH_W), lambda i: (i, 0)),
        out_shape=jax.ShapeDtypeStruct((LAT.n_tok, BRANCH_W), BF16),
        scratch_shapes=[pltpu.VMEM((DEC_SEQ, ATT_HEADS * HEAD_DIM), BF16),
                        pltpu.VMEM((DEC_SEQ, kvw), BF16)],
        compiler_params=_cparams(1),
        name="lat_attn",
    )(sink, z, ck, cv, cos_t, sin_t)


def _scan_out_ctx(yf_ref, yb_ref, slab):
    nb, tt = CTX.nb, CTX.tt
    cols = []
    for t in range(tt):
        ys = yf_ref[t] + yb_ref[t]
        cols.append(jnp.concatenate([ys[:, h * nb:(h + 1) * nb] for h in range(RWKV_HEADS)], axis=0))
    by_t = jnp.concatenate(cols, axis=1).T
    n_slab = BRANCH_W // V7X_LANES
    for t in range(tt):
        for s in range(n_slab):
            slab[s, pl.ds(t, nb, stride=tt), :] = by_t[t * nb:(t + 1) * nb, s * V7X_LANES:(s + 1) * V7X_LANES]
    return jnp.concatenate([slab[s] for s in range(n_slab)], axis=1)


def _scan_out_lat(yf_ref, yb_ref, ysum, gy):
    nb, tt = LAT.nb, LAT.tt
    chains = nb * RWKV_HEADS
    ysum[...] = yf_ref[...] + yb_ref[...]
    for vp in range(V7X_SUBLANES):
        a = ysum[pl.ds(vp, tt, stride=V7X_SUBLANES), :].T
        for vl in range(LAT_VL):
            gy[pl.ds(vp + V7X_SUBLANES * vl, chains, stride=HEAD_DIM), :] = a[vl * chains:(vl + 1) * chains, :]
    return jnp.concatenate([gy[b * BRANCH_W:(b + 1) * BRANCH_W, :].T for b in range(nb)], axis=0)


def _mix_kernel(x_ref, mod_ref, g1_ref, oa_ref, ob_ref, od_ref, yf_ref, yb_ref, g_ref, bonus_ref,
                lnxg_ref, lnxb_ref, ones_ref, wg_ref, bg_ref, wb_ref, wo_ref, x1_ref, *scratch, path):
    nb, tt = path.nb, path.tt
    rows = nb * tt
    r0, nr = path.mod_row0, path.mod_rows

    def mod3(j):
        return mod_ref[r0:r0 + nr, j * D_MODEL:(j + 1) * D_MODEL][:, None, :]

    x3 = x_ref[...]
    h3 = _rms(x3, g1_ref[...]) * (1.0 + mod3(1)) + mod3(0)
    hb = h3.reshape(rows, D_MODEL).astype(BF16)
    y = (_scan_out_ctx if path is CTX else _scan_out_lat)(yf_ref, yb_ref, *scratch)
    ones = ones_ref[...]
    mu = _head_sum(y, ones) * (1.0 / HEAD_DIM)
    dl = y - mu
    var = _head_sum(dl * dl, ones) * (1.0 / HEAD_DIM)
    yn = dl * lax.rsqrt(var + GN_EPS) * lnxg_ref[...] + lnxb_ref[...]
    bonus = bonus_ref[...].reshape(rows, BRANCH_W)
    gate_c = g_ref[...].reshape(rows, BRANCH_W)
    oc = ((yn + bonus) * gate_c).astype(BF16)
    branches = (oa_ref[...].reshape(rows, BRANCH_W), ob_ref[...].reshape(rows, BRANCH_W), oc,
                od_ref[...].reshape(rows, BRANCH_W))
    mixed = None
    for n, br in enumerate(branches):
        cols = slice(n * D_MODEL, (n + 1) * D_MODEL)
        gate = jax.nn.sigmoid(_dot(hb, wg_ref[:, cols]) + bg_ref[:, cols])
        term = gate * _dot(br, wb_ref[n])
        mixed = term if mixed is None else mixed + term
    mix = _dot(mixed.astype(BF16), wo_ref[...])
    x1_ref[...] = x3 + mod3(2) * mix.reshape(nb, tt, D_MODEL)


def _mix(x, mod, p, oa, ob, od, yf, yb, g, bonus, ones, path):
    nb, tt, seq = path.nb, path.tt, path.seq
    tokw = pl.BlockSpec((nb, tt, BRANCH_W), lambda i: (0, i, 0))
    tokd = pl.BlockSpec((nb, tt, D_MODEL), lambda i: (0, i, 0))
    n_slab = BRANCH_W // V7X_LANES
    if path is CTX:
        y_specs = [pl.BlockSpec((None, tt, HEAD_DIM, V7X_LANES), lambda i, d=d: (d, i, 0, 0)) for d in range(N_DIR)]
        scratch = [pltpu.VMEM((n_slab, nb * tt, V7X_LANES), F32)]
    else:
        y_specs = [pl.BlockSpec((tt * V7X_SUBLANES, V7X_LANES), lambda i: (i, 0))] * N_DIR
        scratch = [pltpu.VMEM((tt * V7X_SUBLANES, V7X_LANES), F32),
                   pltpu.VMEM((nb * BRANCH_W, V7X_LANES), F32)]
    as3 = lambda a: a.reshape(nb, seq, a.shape[-1])
    out = pl.pallas_call(
        functools.partial(_mix_kernel, path=path),
        grid=(seq // tt,),
        in_specs=[
            tokd,
            _const_spec((MOD_ROWS, N_MOD * D_MODEL)),
            _const_spec((1, D_MODEL)),
            tokw, tokw, tokw, *y_specs, tokw, tokw,
            _const_spec((1, BRANCH_W)),
            _const_spec((1, BRANCH_W)),
            _const_spec((BRANCH_W, BRANCH_W)),
            _const_spec((D_MODEL, N_BRANCH * D_MODEL)),
            _const_spec((1, N_BRANCH * D_MODEL)),
            _const_spec((N_BRANCH, BRANCH_W, D_MODEL)),
            _const_spec((D_MODEL, D_MODEL)),
        ],
        out_specs=tokd,
        out_shape=jax.ShapeDtypeStruct((nb, seq, D_MODEL), F32),
        scratch_shapes=scratch,
        compiler_params=_cparams(1),
        name=f"branch_mix_{seq}",
    )(as3(x), mod, p["norm1_g"], as3(oa), as3(ob), as3(od), yf, yb, g, bonus, p["rwkv_lnx_g"], p["rwkv_lnx_b"],
      ones, p["w_gate"], p["b_gate"], p["w_branch"], p["w_out"])
    return out.reshape(path.n_tok, D_MODEL)


ROUTER_LANES = V7X_LANES


def _route(logits):
    lane = lax.broadcasted_iota(jnp.int32, logits.shape, 1).astype(F32)
    ninf = -jnp.inf
    big = float(ROUTER_LANES)
    gmask = lane < N_GROUPS
    gl = jnp.where(gmask, logits, ninf)
    gmax = jnp.max(gl, axis=-1, keepdims=True)
    gidx = jnp.min(jnp.where(gl == gmax, lane, big), axis=-1, keepdims=True)
    g_w = 1.0 / jnp.sum(jnp.where(gmask, jnp.exp(gl - gmax), 0.0), axis=-1, keepdims=True)
    egroup = jnp.floor((lane - N_GROUPS) * (1.0 / EXPERTS_PER_GROUP))
    emask = (lane >= N_GROUPS) & (lane < N_GROUPS + N_EXPERTS) & (egroup == gidx)
    el = jnp.where(emask, logits, ninf)
    e1 = jnp.max(el, axis=-1, keepdims=True)
    i1 = jnp.min(jnp.where(emask & (el == e1), lane, big), axis=-1, keepdims=True)
    el2 = jnp.where(lane == i1, ninf, el)
    e2 = jnp.max(el2, axis=-1, keepdims=True)
    i2 = jnp.min(jnp.where(emask & (lane != i1) & (el2 == e2), lane, big), axis=-1, keepdims=True)
    t = jnp.exp(e2 - e1)
    den = 1.0 + t
    return jnp.where(lane == i1, g_w * (1.0 / den), 0.0) + jnp.where(lane == i2, g_w * (t / den), 0.0)


def _moe_kernel(x1_ref, mod_ref, g2_ref, wrh_ref, wrl_ref, br_ref, weg_ref, weu_ref, wed_ref, fg_ref,
                *out_refs, path, final):
    row = _mod_row(pl.program_id(0), path)
    x1 = x1_ref[...]
    h2 = _rms(x1, g2_ref[...]) * (1.0 + _mod_slice(mod_ref, row, 4)) + _mod_slice(mod_ref, row, 3)
    comb = _route(_dot_split(h2, wrh_ref[...], wrl_ref[...]) + br_ref[...])
    hb = h2.astype(BF16)
    gw = EXPERTS_PER_GROUP * EXPERT_FF
    acc = None
    for grp in range(N_GROUPS):
        cols = slice(grp * gw, (grp + 1) * gw)
        hg = _dot(hb, weg_ref[:, cols])
        hu = _dot(hb, weu_ref[:, cols])
        hid = hg * jax.nn.sigmoid(hg) * hu
        parts = []
        for e in range(EXPERTS_PER_GROUP):
            lane = N_GROUPS + grp * EXPERTS_PER_GROUP + e
            parts.append(hid[:, e * EXPERT_FF:(e + 1) * EXPERT_FF] * comb[:, lane:lane + 1])
        term = _dot(jnp.concatenate(parts, axis=1).astype(BF16), wed_ref[cols, :])
        acc = term if acc is None else acc + term
    x2 = x1 + _mod_slice(mod_ref, row, 5) * acc
    out_refs[0][...] = x2
    if final:
        out_refs[1][...] = _rms(x2, fg_ref[...])


def _moe(x1, mod, p, final_g, final, path):
    tm = TOK_TILE
    tokd = pl.BlockSpec((tm, D_MODEL), lambda i: (i, 0))
    ff = N_EXPERTS * EXPERT_FF
    single = pl.Buffered(1)
    out = jax.ShapeDtypeStruct((path.n_tok, D_MODEL), F32)
    return pl.pallas_call(
        functools.partial(_moe_kernel, path=path, final=final),
        grid=(path.n_tok // tm,),
        in_specs=[
            tokd,
            _const_spec((MOD_ROWS, N_MOD * D_MODEL)),
            _const_spec((1, D_MODEL)),
            _const_spec((D_MODEL, ROUTER_LANES)),
            _const_spec((D_MODEL, ROUTER_LANES)),
            _const_spec((1, ROUTER_LANES)),
            pl.BlockSpec((D_MODEL, ff), lambda i: (0, 0), pipeline_mode=single),
            pl.BlockSpec((D_MODEL, ff), lambda i: (0, 0), pipeline_mode=single),
            pl.BlockSpec((ff, D_MODEL), lambda i: (0, 0), pipeline_mode=single),
            _const_spec((1, D_MODEL)),
        ],
        out_specs=[tokd, tokd] if final else [tokd],
        out_shape=[out, out] if final else [out],
        compiler_params=_cparams(1),
        name=f"moe_{path.seq}" + ("_final" if final else ""),
    )(x1, mod, p["norm2_g"], *p["w_router"], p["b_router"], p["w_e_gate"], p["w_e_up"], p["w_e_down"], final_g)


def _block_diag2(w):
    z = jnp.zeros_like(w[0])
    return jnp.concatenate([jnp.concatenate([w[0], z], axis=1), jnp.concatenate([z, w[1]], axis=1)], axis=0)


def _layer_params(l, a):
    row = lambda v: v.reshape(1, -1)
    perm = jnp.concatenate([a["w_in"][l][:, 0:2048], a["w_in"][l][:, 2176:2688], a["w_in"][l][:, 2048:2176]], axis=1)
    w_router = jnp.zeros((D_MODEL, ROUTER_LANES), F32)
    w_router = w_router.at[:, 0:N_GROUPS].set(a["w_rg"][l]).at[:, N_GROUPS:N_GROUPS + N_EXPERTS].set(a["w_re"][l])
    b_router = jnp.zeros((1, ROUTER_LANES), F32)
    b_router = b_router.at[0, 0:N_GROUPS].set(a["b_rg"][l]).at[0, N_GROUPS:N_GROUPS + N_EXPERTS].set(a["b_re"][l])
    ff = N_EXPERTS * EXPERT_FF
    return {
        "norm1_g": row(a["norm1_g"][l]), "norm2_g": row(a["norm2_g"][l]),
        "w_in": perm.astype(BF16),
        "gmlp_ln_g": row(a["gmlp_ln_g"][l]),
        "gmlp_wcat": a["gmlp_ws"][l].transpose(1, 0, 2).reshape(CHUNK, GMLP_GROUPS * CHUNK).astype(BF16),
        "gmlp_bsx": jnp.repeat(a["gmlp_bs"][l].T, BRANCH_W // GMLP_GROUPS, axis=1),
        "conv_w": a["conv_w"][l], "conv_b": row(a["conv_b"][l]),
        "conv_ln_g": row(a["conv_ln_g"][l]), "conv_ln_b": row(a["conv_ln_b"][l]),
        "rwkv_w0": row(a["rwkv_w0"][l]), "rwkv_w2blk": jnp.stack(_split(_block_diag2(a["rwkv_w2"][l]))),
        "rwkv_a0": row(a["rwkv_a0"][l]), "rwkv_a2blk": jnp.stack(_split(_block_diag2(a["rwkv_a2"][l]))),
        "rwkv_g2": jnp.stack(_split(a["rwkv_g2"][l])), "rwkv_kk": row(a["rwkv_kk"][l]), "rwkv_ka": row(a["rwkv_ka"][l]),
        "rwkv_rk": row(a["rwkv_rk"][l]),
        "rwkv_lnx_g": row(a["rwkv_lnx_g"][l]), "rwkv_lnx_b": row(a["rwkv_lnx_b"][l]),
        "attn_sink": a["attn_sink"][l],
        "w_gate": a["w_gate"][l].astype(BF16), "b_gate": row(a["b_gate"][l]),
        "w_branch": a["w_branch"][l].astype(BF16), "w_out": a["w_out"][l].astype(BF16),
        "w_router": _split(w_router), "b_router": b_router,
        "w_e_gate": a["w_e_gate"][l].transpose(1, 0, 2).reshape(D_MODEL, ff).astype(BF16),
        "w_e_up": a["w_e_up"][l].transpose(1, 0, 2).reshape(D_MODEL, ff).astype(BF16),
        "w_e_down": a["w_e_down"][l].reshape(ff, D_MODEL).astype(BF16),
    }


def _rope_tables():
    half = HEAD_DIM // 4
    inv = ROPE_BASE ** (-jnp.arange(half, dtype=F32) / half)
    t = jnp.arange(DEC_SEQ)
    row = (t // GRID_W).astype(F32)[:, None] * inv[None, :]
    col = (t % GRID_W).astype(F32)[:, None] * inv[None, :]
    cos_h = jnp.concatenate([jnp.cos(row), jnp.cos(row), jnp.cos(col), jnp.cos(col)], axis=1)
    sin_h = jnp.concatenate([-jnp.sin(row), jnp.sin(row), -jnp.sin(col), jnp.sin(col)], axis=1)
    reps = V7X_LANES // HEAD_DIM
    return jnp.tile(cos_h, (1, reps)), jnp.tile(sin_h, (1, reps))


def _trunk_layer(x, mod, p, ones, path, final_g, final, attn, scan):
    z, *kv = _inproj(x, mod, p["norm1_g"], p["w_in"], path)
    oa, ob = _local_mix(z, p, path)
    od = attn(z)
    g, bonus, yf, yb, s_fin = scan(z)
    x1 = _mix(x, mod, p, oa, ob, od, yf, yb, g, bonus, ones, path)
    outs = _moe(x1, mod, p, final_g, final, path)
    return outs[0], (outs[1] if final else None), kv, s_fin


def kernel(x_prompt, x_sample, cache_k, cache_v, state_rwkv, c, c_ctx, norm1_g, norm2_g, final_norm_g, w_mod, b_mod, w_in, gmlp_ln_g, gmlp_ws, gmlp_bs, conv_w, conv_b, conv_ln_g, conv_ln_b, rwkv_w0, rwkv_w2, rwkv_a0, rwkv_a2, rwkv_g2, rwkv_kk, rwkv_ka, rwkv_rk, rwkv_lnx_g, rwkv_lnx_b, attn_sink, w_gate, b_gate, w_branch, w_out, w_rg, b_rg, w_re, b_re, w_e_gate, w_e_up, w_e_down):
    arrays = dict(norm1_g=norm1_g, norm2_g=norm2_g, w_in=w_in, gmlp_ln_g=gmlp_ln_g, gmlp_ws=gmlp_ws,
                  gmlp_bs=gmlp_bs, conv_w=conv_w, conv_b=conv_b, conv_ln_g=conv_ln_g, conv_ln_b=conv_ln_b,
                  rwkv_w0=rwkv_w0, rwkv_w2=rwkv_w2, rwkv_a0=rwkv_a0, rwkv_a2=rwkv_a2, rwkv_g2=rwkv_g2,
                  rwkv_kk=rwkv_kk, rwkv_ka=rwkv_ka, rwkv_rk=rwkv_rk, rwkv_lnx_g=rwkv_lnx_g,
                  rwkv_lnx_b=rwkv_lnx_b, attn_sink=attn_sink, w_gate=w_gate, b_gate=b_gate,
                  w_branch=w_branch, w_out=w_out, w_rg=w_rg, b_rg=b_rg, w_re=w_re, b_re=b_re,
                  w_e_gate=w_e_gate, w_e_up=w_e_up, w_e_down=w_e_down)
    xc = x_prompt.reshape(CTX.n_tok, D_MODEL)
    xl = x_sample.reshape(LAT.n_tok, D_MODEL)
    cvec = jnp.concatenate([c_ctx[None], c, jnp.zeros((MOD_ROWS - 1 - DEC_BATCH, D_MODEL), F32)], axis=0)
    mod_all = _modulation(cvec, w_mod, b_mod)
    ones = _head_ones()
    cos_t, sin_t = _rope_tables()
    final_g = final_norm_g.reshape(1, D_MODEL)
    kvw = KV_HEADS * HEAD_DIM
    ks_out, vs_out, ss_out = [], [], []
    yc = yl = None
    for l in range(DEPTH):
        p = _layer_params(l, arrays)
        mod = mod_all[l]
        final = l == DEPTH - 1

        def scan_ctx(z):
            g, bonus, r_s, kk_s, v_s, w_s, kd_s, b_s = _prep_ctx(z, p, ones)
            y, s_fin = _scan_ctx(r_s, kk_s, v_s, w_s, kd_s, b_s)
            return g, bonus, y, y, s_fin

        def scan_lat(z):
            g, bonus, v_s, kq = _prep_lat(z, p, ones)
            s0 = state_rwkv[:, l].reshape(DEC_BATCH, N_DIR, RWKV_HEADS, LAT_VL, V7X_SUBLANES, HEAD_DIM)
            s0 = s0.transpose(1, 5, 4, 3, 0, 2).reshape(N_DIR, HEAD_DIM, V7X_SUBLANES, V7X_LANES)
            yf, yb = _scan_lat(kq, v_s, s0)
            return g, bonus, yf, yb, None

        ck = cache_k[:, l].reshape(DEC_BATCH, PAST_LEN, kvw)
        cv = cache_v[:, l].reshape(DEC_BATCH, PAST_LEN, kvw)
        xc, yc_l, kv_c, s_fin = _trunk_layer(
            xc, mod, p, ones, CTX, final_g, final, lambda z: _ctx_attn(z, p["attn_sink"]), scan_ctx)
        xl, yl_l, _, _ = _trunk_layer(
            xl, mod, p, ones, LAT, final_g, final,
            lambda z: _lat_attn(z, p["attn_sink"], ck, cv, cos_t, sin_t), scan_lat)
        if final:
            yc, yl = yc_l, yl_l
        ks_out.append(kv_c[0].reshape(BATCH, SEQ, KV_HEADS, HEAD_DIM))
        vs_out.append(kv_c[1].reshape(BATCH, SEQ, KV_HEADS, HEAD_DIM))
        ss_out.append(s_fin.reshape(N_DIR, HEAD_DIM, HEAD_DIM, RWKV_HEADS, BATCH).transpose(4, 0, 3, 2, 1))
    y_prompt = yc.reshape(BATCH, SEQ, D_MODEL)
    y_sample = yl.reshape(DEC_BATCH, DEC_SEQ, D_MODEL)
    return (y_prompt, y_sample, jnp.stack(ks_out, axis=1), jnp.stack(vs_out, axis=1), jnp.stack(ss_out, axis=1))
```

```python
import functools
import math
from typing import NamedTuple

import jax
import jax.numpy as jnp
from jax import lax
from jax.experimental import pallas as pl
from jax.experimental.pallas import tpu as pltpu

D_MODEL = 1024
BATCH = 32
SEQ = 256
DEPTH = 2
DEC_BATCH = 4
DEC_SEQ = 1024
PAST_LEN = 256
GRID_W = 64
HEAD_DIM = 64
BRANCH_W = 256
N_BRANCH = 4
CHUNK = 128
GMLP_GROUPS = 4
CONV_W = 31
RWKV_HEADS = 4
N_DIR = 2
DECAY_LORA = 64
ICL_LORA = 64
GATE_LORA = 128
ATT_HEADS = 4
KV_HEADS = 2
Q_PER_KV = ATT_HEADS // KV_HEADS
WINDOW = 128
BLOCK = 128
ROPE_BASE = 10000.0
ATT_SCALE = HEAD_DIM ** -0.5
N_GROUPS = 4
EXPERTS_PER_GROUP = 4
N_EXPERTS = N_GROUPS * EXPERTS_PER_GROUP
EXPERT_FF = 256
N_MOD = 6
RMS_EPS = 1e-6
LN_EPS = 1e-5
GN_EPS = 64e-5
IN_COLS = 2688

V7X_LANES = 128
V7X_SUBLANES = 8
V7X_VMEM_LIMIT = 56 * 1024 * 1024

MOD_ROWS = 8
TOK_TILE = 512
SCAN_TB = 32
CONV_PAD = 16

Z_AB, Z_C, Z_D, Z_G = 0, 1024, 2048, 2560

LAT_Q_R, LAT_Q_KK, LAT_Q_W, LAT_Q_KD, LAT_Q_B = 0, 1, 2, 4, 6
LAT_NQ = 8
LAT_VL = V7X_LANES // (DEC_BATCH * RWKV_HEADS)

F32 = jnp.float32
BF16 = jnp.bfloat16
HIGHEST = lax.Precision.HIGHEST


class _Path(NamedTuple):
    nb: int
    seq: int
    mod_row0: int
    mod_rows: int
    tt: int

    @property
    def n_tok(self):
        return self.nb * self.seq


CTX = _Path(BATCH, SEQ, 0, 1, TOK_TILE // BATCH)
LAT = _Path(DEC_BATCH, DEC_SEQ, 1, DEC_BATCH, TOK_TILE // DEC_BATCH)


def _cparams(n_axes, vmem=V7X_VMEM_LIMIT):
    return pltpu.CompilerParams(dimension_semantics=("arbitrary",) * n_axes, vmem_limit_bytes=vmem)


def _const_spec(shape):
    nd = len(shape)
    return pl.BlockSpec(shape, lambda *_: (0,) * nd)


def _dot(a, b):
    return jnp.dot(a, b, preferred_element_type=F32)


def _dot_hi(a, b):
    return jnp.dot(a, b, preferred_element_type=F32, precision=HIGHEST)


def _split(x):
    hi = x.astype(BF16)
    return hi, (x - hi.astype(F32)).astype(BF16)


def _dot_split(a, b_hi, b_lo):
    a_hi, a_lo = _split(a)
    return _dot(a_hi, b_hi) + (_dot(a_lo, b_hi) + _dot(a_hi, b_lo))


def _head_sum(x, ones_b):
    hi, lo = _split(x)
    return _dot(hi, ones_b) + _dot(lo, ones_b)


def _head_ones():
    r = lax.broadcasted_iota(jnp.int32, (BRANCH_W, BRANCH_W), 0) // HEAD_DIM
    c = lax.broadcasted_iota(jnp.int32, (BRANCH_W, BRANCH_W), 1) // HEAD_DIM
    return (r == c).astype(BF16)


def _mod_row(tile, path):
    if path.mod_rows == 1:
        return path.mod_row0
    return path.mod_row0 + tile // (path.seq // TOK_TILE)


def _mod_slice(mod_ref, row, j):
    return mod_ref[pl.ds(row, 1), j * D_MODEL:(j + 1) * D_MODEL]


def _rms(x, g):
    return x * lax.rsqrt(jnp.mean(x * x, axis=-1, keepdims=True) + RMS_EPS) * g


def _layernorm(x, g, b=None, eps=LN_EPS):
    mu = jnp.mean(x, axis=-1, keepdims=True)
    d = x - mu
    var = jnp.mean(d * d, axis=-1, keepdims=True)
    y = d * lax.rsqrt(var + eps) * g
    return y if b is None else y + b


def _mod_kernel(c_ref, w_ref, b_ref, o_ref):
    c = c_ref[...]
    a = c * jax.nn.sigmoid(c)
    o_ref[0] = _dot_hi(a, w_ref[0]) + b_ref[0]


def _modulation(cvec, w_mod, b_mod):
    return pl.pallas_call(
        _mod_kernel,
        grid=(DEPTH, N_MOD),
        in_specs=[
            pl.BlockSpec((MOD_ROWS, D_MODEL), lambda l, j: (0, 0)),
            pl.BlockSpec((1, D_MODEL, D_MODEL), lambda l, j: (l, 0, j)),
            pl.BlockSpec((1, 1, D_MODEL), lambda l, j: (l, 0, j)),
        ],
        out_specs=pl.BlockSpec((1, MOD_ROWS, D_MODEL), lambda l, j: (l, 0, j)),
        out_shape=jax.ShapeDtypeStruct((DEPTH, MOD_ROWS, N_MOD * D_MODEL), F32),
        compiler_params=_cparams(2),
        name="modulation",
    )(cvec, w_mod, b_mod.reshape(DEPTH, 1, N_MOD * D_MODEL))


def _inproj_kernel(x_ref, mod_ref, g_ref, w_ref, z_ref, *kv_refs, path):
    row = _mod_row(pl.program_id(0), path)
    h = _rms(x_ref[...], g_ref[...]) * (1.0 + _mod_slice(mod_ref, row, 1)) + _mod_slice(mod_ref, row, 0)
    z = _dot(h.astype(BF16), w_ref[...])
    z_ref[...] = z
    kvw = KV_HEADS * HEAD_DIM
    for j, ref in enumerate(kv_refs):
        off = Z_D + ATT_HEADS * HEAD_DIM + j * kvw
        ref[...] = z[:, off:off + kvw]


def _inproj(x, mod, g1, w_in_b, path):
    tm = TOK_TILE
    kvw = KV_HEADS * HEAD_DIM
    n_kv = 2 if path is CTX else 0
    return pl.pallas_call(
        functools.partial(_inproj_kernel, path=path),
        grid=(path.n_tok // tm,),
        in_specs=[
            pl.BlockSpec((tm, D_MODEL), lambda i: (i, 0)),
            _const_spec((MOD_ROWS, N_MOD * D_MODEL)),
            _const_spec((1, D_MODEL)),
            _const_spec((D_MODEL, IN_COLS)),
        ],
        out_specs=[pl.BlockSpec((tm, IN_COLS), lambda i: (i, 0))] + [pl.BlockSpec((tm, kvw), lambda i: (i, 0))] * n_kv,
        out_shape=[jax.ShapeDtypeStruct((path.n_tok, IN_COLS), F32)]
        + [jax.ShapeDtypeStruct((path.n_tok, kvw), F32)] * n_kv,
        compiler_params=_cparams(1),
        name=f"inproj_{path.seq}",
    )(x, mod, g1, w_in_b)


def _local_mix_kernel(z_ref, lng_ref, wcat_ref, bsx_ref, cw_ref, cb_ref, clg_ref, clb_ref,
                      oa_ref, ob_ref, ypad, yshift, *, seq_len):
    lane_group = lax.broadcasted_iota(jnp.int32, (CHUNK, BRANCH_W), 1) // (BRANCH_W // GMLP_GROUPS)
    for c in range(seq_len // CHUNK):
        rows = pl.ds(c * CHUNK, CHUNK)
        u = jax.nn.gelu(z_ref[rows, 0:BRANCH_W], approximate=True)
        v = jax.nn.gelu(z_ref[rows, BRANCH_W:2 * BRANCH_W], approximate=True)
        vn = _layernorm(v, lng_ref[...])
        vblk = jnp.concatenate(
            [jnp.where(lane_group == g, vn, 0.0) for g in range(GMLP_GROUPS)], axis=0).astype(BF16)
        mixed = _dot(wcat_ref[...], vblk) + bsx_ref[...]
        oa_ref[rows, :] = (u * mixed).astype(BF16)

    n8 = V7X_SUBLANES
    ypad[0:CONV_PAD, :] = jnp.zeros((CONV_PAD, BRANCH_W), F32)
    ypad[CONV_PAD + seq_len:, :] = jnp.zeros((CONV_PAD + n8, BRANCH_W), F32)
    ypad[CONV_PAD:CONV_PAD + seq_len, :] = (
        z_ref[:, 2 * BRANCH_W:3 * BRANCH_W] * jax.nn.sigmoid(z_ref[:, 3 * BRANCH_W:4 * BRANCH_W]))
    span = seq_len + 2 * CONV_PAD
    for r in range(1, n8):
        yshift[r - 1] = ypad[r:r + span, :]
    base = CONV_PAD - CONV_W // 2
    for c in range(seq_len // CHUNK):
        acc = jnp.zeros((CHUNK, BRANCH_W), F32)
        for j in range(CONV_W):
            off = c * CHUNK + base + j
            r = off % n8
            rows = pl.ds(off - r, CHUNK)
            acc = acc + cw_ref[j:j + 1, :] * (ypad[rows, :] if r == 0 else yshift[r - 1, rows, :])
        y = _layernorm(acc + cb_ref[...], clg_ref[...], clb_ref[...])
        ob_ref[pl.ds(c * CHUNK, CHUNK), :] = (y * jax.nn.sigmoid(y)).astype(BF16)


def _local_mix(z, p, path):
    out = jax.ShapeDtypeStruct((path.n_tok, BRANCH_W), BF16)
    return pl.pallas_call(
        functools.partial(_local_mix_kernel, seq_len=path.seq),
        grid=(path.nb,),
        in_specs=[
            pl.BlockSpec((path.seq, 4 * BRANCH_W), lambda i: (i, Z_AB // (4 * BRANCH_W))),
            _const_spec((1, BRANCH_W)),
            _const_spec((CHUNK, GMLP_GROUPS * CHUNK)),
            _const_spec((CHUNK, BRANCH_W)),
            _const_spec((CONV_W, BRANCH_W)),
            _const_spec((1, BRANCH_W)),
            _const_spec((1, BRANCH_W)),
            _const_spec((1, BRANCH_W)),
        ],
        out_specs=[pl.BlockSpec((path.seq, BRANCH_W), lambda i: (i, 0))] * 2,
        out_shape=[out, out],
        scratch_shapes=[pltpu.VMEM((path.seq + 2 * CONV_PAD + V7X_SUBLANES, BRANCH_W), F32),
                        pltpu.VMEM((V7X_SUBLANES - 1, path.seq + 2 * CONV_PAD, BRANCH_W), F32)],
        compiler_params=_cparams(1),
        name=f"local_mix_{path.seq}",
    )(z, p["gmlp_ln_g"], p["gmlp_wcat"], p["gmlp_bsx"], p["conv_w"], p["conv_b"],
      p["conv_ln_g"], p["conv_ln_b"])


def _rwkv_quantities(zc, gd, w0_ref, w2_ref, a0_ref, a2_ref, g2_ref, kkp_ref, ka_ref, rk_ref, ones):
    W = BRANCH_W
    r = zc[:, 0:W]
    k = zc[:, W:2 * W]
    v = zc[:, 2 * W:3 * W]
    wd = zc[:, 3 * W:3 * W + N_DIR * DECAY_LORA]
    ad = zc[:, 3 * W + N_DIR * DECAY_LORA:4 * W]
    w_raw = _dot_split(jnp.tanh(wd), w2_ref[0], w2_ref[1]) + w0_ref[...]
    decay = jnp.exp(-math.exp(-0.5) * jax.nn.sigmoid(w_raw))
    a = jax.nn.sigmoid(_dot_split(ad, a2_ref[0], a2_ref[1]) + a0_ref[...])
    g = _dot_split(jax.nn.sigmoid(gd), g2_ref[0], g2_ref[1])
    kkr = k * kkp_ref[...]
    kk = kkr / jnp.maximum(jnp.sqrt(_head_sum(kkr * kkr, ones)), 1e-12)
    ka = ka_ref[...]
    w, kd, b = [], [], []
    for d in range(N_DIR):
        a_d = a[:, d * W:(d + 1) * W]
        w.append(decay[:, d * W:(d + 1) * W])
        kd.append(k * (1.0 + (a_d - 1.0) * ka))
        b.append(kk * a_d)
    bonus = _head_sum(r * (kd[0] + kd[1]) * rk_ref[...], ones) * v
    return g, bonus, r, kk, v, w, kd, b


def _prep_ctx_kernel(zc_ref, zg_ref, w0_ref, w2_ref, a0_ref, a2_ref, g2_ref, kkp_ref, ka_ref, rk_ref,
                     ones_ref, g_o, bonus_o, r_o, kk_o, v_o, w_o, kd_o, b_o, slab):
    nb, tt = CTX.nb, CTX.tt
    rows = nb * tt
    zc = zc_ref[...].reshape(rows, 4 * BRANCH_W)
    gd = zg_ref[...].reshape(rows, GATE_LORA)
    g, bonus, r, kk, v, w, kd, b = _rwkv_quantities(
        zc, gd, w0_ref, w2_ref, a0_ref, a2_ref, g2_ref, kkp_ref, ka_ref, rk_ref, ones_ref[...])
    g_o[...] = g.reshape(nb, tt, BRANCH_W)
    bonus_o[...] = bonus.reshape(nb, tt, BRANCH_W)
    n_slab = BRANCH_W // V7X_LANES
    lanes_per_head = V7X_LANES // RWKV_HEADS

    def to_chains(q, put):
        for s in range(n_slab):
            slab[s] = q[:, s * V7X_LANES:(s + 1) * V7X_LANES]
        by_t = jnp.concatenate(
            [jnp.concatenate([slab[s, pl.ds(t, nb, stride=tt), :] for s in range(n_slab)], axis=1)
             for t in range(tt)], axis=0)
        tr = by_t.T
        lane = lax.broadcasted_iota(jnp.int32, (HEAD_DIM, V7X_LANES), 1)
        low_half = lane < V7X_LANES // 2
        even_quarter = (lane // nb) % 2 == 0
        roll = lambda a, s: pltpu.roll(a, s, 1)
        for c in range(tt // RWKV_HEADS):
            x0, x1, x2, x3 = (tr[h * HEAD_DIM:(h + 1) * HEAD_DIM, c * V7X_LANES:(c + 1) * V7X_LANES]
                              for h in range(RWKV_HEADS))
            y0 = jnp.where(low_half, x0, roll(x2, 2 * nb))
            y2 = jnp.where(low_half, roll(x0, 2 * nb), x2)
            y1 = jnp.where(low_half, x1, roll(x3, 2 * nb))
            y3 = jnp.where(low_half, roll(x1, 2 * nb), x3)
            put(4 * c + 0, jnp.where(even_quarter, y0, roll(y1, nb)))
            put(4 * c + 1, jnp.where(even_quarter, roll(y0, 3 * nb), y1))
            put(4 * c + 2, jnp.where(even_quarter, y2, roll(y3, nb)))
            put(4 * c + 3, jnp.where(even_quarter, roll(y2, 3 * nb), y3))
    assert lanes_per_head == nb and RWKV_HEADS == 4

    def put_into(ref, *lead):
        def put(t, tile):
            ref[(*lead, t)] = tile
        return put

    to_chains(r, put_into(r_o))
    to_chains(kk, put_into(kk_o))
    to_chains(v, put_into(v_o))
    for d in range(N_DIR):
        to_chains(w[d], put_into(w_o, d))
        to_chains(kd[d], put_into(kd_o, d))
        to_chains(b[d], put_into(b_o, d))


def _prep_lat_kernel(zc_ref, zg_ref, w0_ref, w2_ref, a0_ref, a2_ref, g2_ref, kkp_ref, ka_ref, rk_ref,
                     ones_ref, g_o, bonus_o, v_o, kq_o, gk, gv):
    nb, tt = LAT.nb, LAT.tt
    rows = nb * tt
    qi = pl.program_id(1)
    chains = nb * RWKV_HEADS

    @pl.when(qi == 0)
    def _():
        zc = zc_ref[...].reshape(rows, 4 * BRANCH_W)
        gd = zg_ref[...].reshape(rows, GATE_LORA)
        g, bonus, r, kk, v, w, kd, b = _rwkv_quantities(
            zc, gd, w0_ref, w2_ref, a0_ref, a2_ref, g2_ref, kkp_ref, ka_ref, rk_ref, ones_ref[...])
        g_o[...] = g.reshape(nb, tt, BRANCH_W)
        bonus_o[...] = bonus.reshape(nb, tt, BRANCH_W)
        k_indexed = {LAT_Q_R: r, LAT_Q_KK: kk, LAT_Q_W: w[0], LAT_Q_W + 1: w[1],
                     LAT_Q_KD: kd[0], LAT_Q_KD + 1: kd[1], LAT_Q_B: b[0], LAT_Q_B + 1: b[1]}
        for bi in range(nb):
            for q_idx, q in k_indexed.items():
                gk[q_idx, bi * BRANCH_W:(bi + 1) * BRANCH_W, :] = q[bi * tt:(bi + 1) * tt, :].T
            gv[bi * BRANCH_W:(bi + 1) * BRANCH_W, :] = v[bi * tt:(bi + 1) * tt, :].T
        for vp in range(V7X_SUBLANES):
            a = jnp.concatenate(
                [gv[pl.ds(vp + V7X_SUBLANES * vl, chains, stride=HEAD_DIM), :] for vl in range(LAT_VL)], axis=0)
            v_o[pl.ds(vp, tt, stride=V7X_SUBLANES), :] = a.T

    def body(k, carry):
        a = gk[qi, pl.ds(k, chains, stride=HEAD_DIM), :]
        n8 = V7X_SUBLANES
        kq_o[k // n8, pl.ds(k % n8, tt, stride=n8), :] = jnp.concatenate([a] * LAT_VL, axis=0).T
        return carry
    lax.fori_loop(0, HEAD_DIM, body, 0, unroll=8)


def _prep_param_specs():
    return [
        _const_spec((1, N_DIR * BRANCH_W)),
        _const_spec((2, N_DIR * DECAY_LORA, N_DIR * BRANCH_W)),
        _const_spec((1, N_DIR * BRANCH_W)),
        _const_spec((2, N_DIR * ICL_LORA, N_DIR * BRANCH_W)),
        _const_spec((2, GATE_LORA, BRANCH_W)),
        _const_spec((1, BRANCH_W)),
        _const_spec((1, BRANCH_W)),
        _const_spec((1, BRANCH_W)),
        _const_spec((BRANCH_W, BRANCH_W)),
    ]


def _prep_params(p, ones):
    return (p["rwkv_w0"], p["rwkv_w2blk"], p["rwkv_a0"], p["rwkv_a2blk"], p["rwkv_g2"],
            p["rwkv_kk"], p["rwkv_ka"], p["rwkv_rk"], ones)


def _prep_ctx(z, p, ones):
    nb, tt, seq = CTX.nb, CTX.tt, CTX.seq
    z3 = z.reshape(nb, seq, IN_COLS)
    tok = jax.ShapeDtypeStruct((nb, seq, BRANCH_W), F32)
    tok_spec = pl.BlockSpec((nb, tt, BRANCH_W), lambda i: (0, i, 0))
    ch = jax.ShapeDtypeStruct((seq, HEAD_DIM, V7X_LANES), F32)
    ch2 = jax.ShapeDtypeStruct((N_DIR, seq, HEAD_DIM, V7X_LANES), F32)
    ch_spec = pl.BlockSpec((tt, HEAD_DIM, V7X_LANES), lambda i: (i, 0, 0))
    ch2_spec = pl.BlockSpec((N_DIR, tt, HEAD_DIM, V7X_LANES), lambda i: (0, i, 0, 0))
    return pl.pallas_call(
        _prep_ctx_kernel,
        grid=(seq // tt,),
        in_specs=[
            pl.BlockSpec((nb, tt, 4 * BRANCH_W), lambda i: (0, i, Z_C // (4 * BRANCH_W))),
            pl.BlockSpec((nb, tt, GATE_LORA), lambda i: (0, i, Z_G // GATE_LORA)),
        ] + _prep_param_specs(),
        out_specs=[tok_spec, tok_spec, ch_spec, ch_spec, ch_spec, ch2_spec, ch2_spec, ch2_spec],
        out_shape=[tok, tok, ch, ch, ch, ch2, ch2, ch2],
        scratch_shapes=[pltpu.VMEM((BRANCH_W // V7X_LANES, nb * tt, V7X_LANES), F32)],
        compiler_params=_cparams(1),
        name="rwkv_prep_ctx",
    )(z3, z3, *_prep_params(p, ones))


def _prep_lat(z, p, ones):
    nb, tt, seq = LAT.nb, LAT.tt, LAT.seq
    z3 = z.reshape(nb, seq, IN_COLS)
    tok = jax.ShapeDtypeStruct((nb, seq, BRANCH_W), F32)
    tok_spec = pl.BlockSpec((nb, tt, BRANCH_W), lambda i, q: (0, i, 0))
    return pl.pallas_call(
        _prep_lat_kernel,
        grid=(seq // tt, LAT_NQ),
        in_specs=[
            pl.BlockSpec((nb, tt, 4 * BRANCH_W), lambda i, q: (0, i, Z_C // (4 * BRANCH_W))),
            pl.BlockSpec((nb, tt, GATE_LORA), lambda i, q: (0, i, Z_G // GATE_LORA)),
        ] + _prep_param_specs(),
        out_specs=[
            tok_spec, tok_spec,
            pl.BlockSpec((tt * V7X_SUBLANES, V7X_LANES), lambda i, q: (i, 0)),
            pl.BlockSpec((None, HEAD_DIM // V7X_SUBLANES, tt * V7X_SUBLANES, V7X_LANES), lambda i, q: (q, 0, i, 0)),
        ],
        out_shape=[tok, tok,
                   jax.ShapeDtypeStruct((seq * V7X_SUBLANES, V7X_LANES), F32),
                   jax.ShapeDtypeStruct((LAT_NQ, HEAD_DIM // V7X_SUBLANES, seq * V7X_SUBLANES, V7X_LANES), F32)],
        scratch_shapes=[pltpu.VMEM((LAT_NQ, nb * BRANCH_W, V7X_LANES), F32),
                        pltpu.VMEM((nb * BRANCH_W, V7X_LANES), F32)],
        compiler_params=_cparams(2),
        name="rwkv_prep_lat",
    )(z3, z3, *_prep_params(p, ones))


SCAN_ACCS = 4


def _strided_sum(terms):
    acc = [None] * SCAN_ACCS
    for j, x in enumerate(terms):
        a = j % SCAN_ACCS
        acc[a] = x if acc[a] is None else acc[a] + x
    while len(acc) > 1:
        acc = [acc[j] + acc[j + 1] for j in range(0, len(acc), 2)]
    return acc[0]


def _scan_ctx_kernel(r_ref, kk_ref, v_ref, w_ref, kd_ref, b_ref, y_ref, sfin_ref, S, *, tb):
    d = pl.program_id(0)
    i = pl.program_id(1)
    n8 = V7X_SUBLANES
    n_v8 = HEAD_DIM // n8

    @pl.when(i == 0)
    def _():
        S[...] = jnp.zeros_like(S)

    def row8(ref, t, k):
        return jnp.broadcast_to(ref[t, pl.ds(k, 1), :], (n8, V7X_LANES))

    def step(s, carry):
        t = s + d * (tb - 1 - 2 * s)

        def pass1(k, sa):
            kkb = row8(kk_ref, t, k)
            return tuple(sa[vo] + S[k, vo * n8:(vo + 1) * n8, :] * kkb for vo in range(n_v8))

        zero = tuple(jnp.zeros((n8, V7X_LANES), F32) for _ in range(n_v8))
        sa = lax.fori_loop(0, HEAD_DIM, pass1, zero, unroll=16)
        nsa = tuple(-x for x in sa)
        vt = tuple(v_ref[t, vo * n8:(vo + 1) * n8, :] for vo in range(n_v8))

        def pass2(k, y):
            wb = row8(w_ref, t, k)
            bb = row8(b_ref, t, k)
            kb = row8(kd_ref, t, k)
            rb = row8(r_ref, t, k)
            out = []
            for vo in range(n_v8):
                sl = slice(vo * n8, (vo + 1) * n8)
                sn = S[k, sl, :] * wb + (nsa[vo] * bb + vt[vo] * kb)
                S[k, sl, :] = sn
                out.append(y[vo] + sn * rb)
            return tuple(out)

        y = lax.fori_loop(0, HEAD_DIM, pass2, zero, unroll=16)
        for vo in range(n_v8):
            y_ref[t, vo * n8:(vo + 1) * n8, :] = y[vo]
        return carry

    lax.fori_loop(0, tb, step, 0)

    @pl.when(i == pl.num_programs(1) - 1)
    def _():
        sfin_ref[...] = S[...]


def _scan_ctx(r_s, kk_s, v_s, w_s, kd_s, b_s):
    seq, tb = CTX.seq, SCAN_TB
    nt = seq // tb

    def tm(d, i):
        return i + d * (nt - 1 - 2 * i)
    shared = pl.BlockSpec((tb, HEAD_DIM, V7X_LANES), lambda d, i: (tm(d, i), 0, 0))
    per_dir = pl.BlockSpec((None, tb, HEAD_DIM, V7X_LANES), lambda d, i: (d, tm(d, i), 0, 0))
    state = pl.BlockSpec((None, HEAD_DIM, HEAD_DIM, V7X_LANES), lambda d, i: (d, 0, 0, 0))
    return pl.pallas_call(
        functools.partial(_scan_ctx_kernel, tb=tb),
        grid=(N_DIR, nt),
        in_specs=[shared, shared, shared, per_dir, per_dir, per_dir],
        out_specs=[per_dir, state],
        out_shape=[jax.ShapeDtypeStruct((N_DIR, seq, HEAD_DIM, V7X_LANES), F32),
                   jax.ShapeDtypeStruct((N_DIR, HEAD_DIM, HEAD_DIM, V7X_LANES), F32)],
        scratch_shapes=[pltpu.VMEM((HEAD_DIM, HEAD_DIM, V7X_LANES), F32)],
        compiler_params=_cparams(2),
        name="rwkv_scan_ctx",
    )(r_s, kk_s, v_s, w_s, kd_s, b_s)


def _lat_scan_dir(S, d, t, r_ref, kk_ref, v_ref, w_ref, kd_ref, b_ref, y_ref):
    n8 = V7X_SUBLANES

    def row8(ref, k):
        return jnp.broadcast_to(ref[k // n8, pl.ds(t * n8 + k % n8, 1), :], (n8, V7X_LANES))
    rows = pl.ds(pl.multiple_of(t * n8, n8), n8)
    nsa = -_strided_sum(S[d, k] * row8(kk_ref, k) for k in range(HEAD_DIM))
    vt = v_ref[rows, :]

    def update(k):
        sn = S[d, k] * row8(w_ref, k) + (nsa * row8(b_ref, k) + vt * row8(kd_ref, k))
        S[d, k] = sn
        return sn * row8(r_ref, k)
    y_ref[rows, :] = _strided_sum(update(k) for k in range(HEAD_DIM))


def _lat_scan_specs():
    tb = SCAN_TB
    nt = LAT.seq // tb
    n8 = V7X_SUBLANES

    def specs(d):
        tblk = (lambda i: i) if d == 0 else (lambda i: nt - 1 - i)
        plane = lambda q: pl.BlockSpec((None, HEAD_DIM // n8, tb * n8, V7X_LANES), lambda i: (q, 0, tblk(i), 0))
        rows = pl.BlockSpec((tb * n8, V7X_LANES), lambda i: (tblk(i), 0))
        return [plane(LAT_Q_R), plane(LAT_Q_KK), rows,
                plane(LAT_Q_W + d), plane(LAT_Q_KD + d), plane(LAT_Q_B + d)], rows
    return specs(0), specs(1)


def _scan_lat_kernel(*refs):
    fwd, bwd = refs[0:6], refs[6:12]
    s0_ref, yf_ref, yb_ref, S = refs[12:]

    @pl.when(pl.program_id(0) == 0)
    def _():
        S[...] = s0_ref[...]

    def step(s, carry):
        _lat_scan_dir(S, 0, s, *fwd, yf_ref)
        _lat_scan_dir(S, 1, SCAN_TB - 1 - s, *bwd, yb_ref)
        return carry
    lax.fori_loop(0, SCAN_TB, step, 0)


def _scan_lat(kq, v_s, s0):
    n8 = V7X_SUBLANES
    (in_f, y_f), (in_b, y_b) = _lat_scan_specs()
    y_shape = jax.ShapeDtypeStruct((LAT.seq * n8, V7X_LANES), F32)
    one = (kq, kq, v_s, kq, kq, kq)
    return pl.pallas_call(
        _scan_lat_kernel,
        grid=(LAT.seq // SCAN_TB,),
        in_specs=in_f + in_b + [_const_spec((N_DIR, HEAD_DIM, n8, V7X_LANES))],
        out_specs=[y_f, y_b],
        out_shape=[y_shape, y_shape],
        scratch_shapes=[pltpu.VMEM((N_DIR, HEAD_DIM, n8, V7X_LANES), F32)],
        compiler_params=_cparams(1),
        name="rwkv_scan_lat",
    )(*one, *one, s0)


def _softmax_pv(scores, vals, sink_col):
    m = sink_col
    for s in scores:
        m = jnp.maximum(m, jnp.max(s, axis=-1, keepdims=True))
    den = jnp.exp(sink_col - m)
    out = None
    for s, vx in zip(scores, vals):
        p = jnp.exp(s - m)
        den = den + jnp.sum(p, axis=-1, keepdims=True)
        o = _dot(p.astype(BF16), vx)
        out = o if out is None else out + o
    return out / den


def _sink_col(sink_ref, kv, n_rows):
    row = lax.broadcasted_iota(jnp.int32, (Q_PER_KV * n_rows, 1), 0)
    col = jnp.full((Q_PER_KV * n_rows, 1), sink_ref[kv * Q_PER_KV], F32)
    for g in range(1, Q_PER_KV):
        col = jnp.where(row >= g * n_rows, sink_ref[kv * Q_PER_KV + g], col)
    return col


def _qk(q2, kh):
    return lax.dot_general(q2, kh, (((1,), (1,)), ((), ())), preferred_element_type=F32) * ATT_SCALE


def _ctx_attn_kernel(sink_ref, z_ref, o_ref):
    hd = HEAD_DIM
    q = z_ref[:, 0:ATT_HEADS * hd]
    outs = []
    for kv in range(KV_HEADS):
        kh = z_ref[:, ATT_HEADS * hd + kv * hd:ATT_HEADS * hd + (kv + 1) * hd].astype(BF16)
        vh = z_ref[:, (ATT_HEADS + KV_HEADS) * hd + kv * hd:(ATT_HEADS + KV_HEADS) * hd + (kv + 1) * hd].astype(BF16)
        q2 = jnp.concatenate(
            [q[:, (kv * Q_PER_KV + g) * hd:(kv * Q_PER_KV + g + 1) * hd] for g in range(Q_PER_KV)],
            axis=0).astype(BF16)
        o2 = _softmax_pv([_qk(q2, kh)], [vh], _sink_col(sink_ref, kv, SEQ))
        outs += [o2[g * SEQ:(g + 1) * SEQ] for g in range(Q_PER_KV)]
    o_ref[...] = jnp.concatenate(outs, axis=1).astype(BF16)


def _rope(x, cos, sin_signed):
    lane = lax.broadcasted_iota(jnp.int32, x.shape, 1)
    first = (lane % (HEAD_DIM // 2)) < (HEAD_DIM // 4)
    partner = jnp.where(first, pltpu.roll(x, V7X_LANES - HEAD_DIM // 4, 1), pltpu.roll(x, HEAD_DIM // 4, 1))
    return x * cos + partner * sin_signed


def _lat_attn_kernel(sink_ref, z_ref, ck_ref, cv_ref, cos_ref, sin_ref, o_ref, q_s, k_s):
    hd = HEAD_DIM
    cos = cos_ref[...]
    sin = sin_ref[...]
    for j in range(ATT_HEADS * hd // V7X_LANES):
        q_s[:, j * V7X_LANES:(j + 1) * V7X_LANES] = _rope(
            z_ref[:, j * V7X_LANES:(j + 1) * V7X_LANES], cos, sin).astype(BF16)
    k_s[...] = _rope(z_ref[:, ATT_HEADS * hd:(ATT_HEADS + KV_HEADS) * hd], cos, sin).astype(BF16)
    voff = (ATT_HEADS + KV_HEADS) * hd
    nb = DEC_SEQ // BLOCK
    for n in range(nb):
        lo = max(n - 1, 0) * BLOCK
        hi = min(n + 2, nb) * BLOCK
        i_abs = n * BLOCK + lax.broadcasted_iota(jnp.int32, (Q_PER_KV * BLOCK, hi - lo), 0) % BLOCK
        j_abs = lo + lax.broadcasted_iota(jnp.int32, (Q_PER_KV * BLOCK, hi - lo), 1)
        band = jnp.abs(i_abs - j_abs) <= WINDOW
        outs = []
        for kv in range(KV_HEADS):
            q2 = jnp.concatenate(
                [q_s[n * BLOCK:(n + 1) * BLOCK, (kv * Q_PER_KV + g) * hd:(kv * Q_PER_KV + g + 1) * hd]
                 for g in range(Q_PER_KV)], axis=0)
            kw = k_s[lo:hi, kv * hd:(kv + 1) * hd]
            vw = z_ref[lo:hi, voff + kv * hd:voff + (kv + 1) * hd].astype(BF16)
            kc = ck_ref[:, kv * hd:(kv + 1) * hd].astype(BF16)
            vc = cv_ref[:, kv * hd:(kv + 1) * hd].astype(BF16)
            s_lat = jnp.where(band, _qk(q2, kw), -1e30)
            o2 = _softmax_pv([s_lat, _qk(q2, kc)], [vw, vc], _sink_col(sink_ref, kv, BLOCK))
            outs += [o2[g * BLOCK:(g + 1) * BLOCK] for g in range(Q_PER_KV)]
        o_ref[n * BLOCK:(n + 1) * BLOCK, :] = jnp.concatenate(outs, axis=1).astype(BF16)


def _ctx_attn(z, sink):
    wd = (ATT_HEADS + 2 * KV_HEADS) * HEAD_DIM
    return pl.pallas_call(
        _ctx_attn_kernel,
        grid=(BATCH,),
        in_specs=[
            pl.BlockSpec(memory_space=pltpu.SMEM),
            pl.BlockSpec((SEQ, wd), lambda i: (i, Z_D // wd)),
        ],
        out_specs=pl.BlockSpec((SEQ, BRANCH_W), lambda i: (i, 0)),
        out_shape=jax.ShapeDtypeStruct((CTX.n_tok, BRANCH_W), BF16),
        compiler_params=_cparams(1),
        name="ctx_attn",
    )(sink, z)


def _lat_attn(z, sink, ck, cv, cos_t, sin_t):
    wd = (ATT_HEADS + 2 * KV_HEADS) * HEAD_DIM
    kvw = KV_HEADS * HEAD_DIM
    return pl.pallas_call(
        _lat_attn_kernel,
        grid=(DEC_BATCH,),
        in_specs=[
            pl.BlockSpec(memory_space=pltpu.SMEM),
            pl.BlockSpec((DEC_SEQ, wd), lambda i: (i, Z_D // wd)),
            pl.BlockSpec((None, PAST_LEN, kvw), lambda i: (i, 0, 0)),
            pl.BlockSpec((None, PAST_LEN, kvw), lambda i: (i, 0, 0)),
            _const_spec((DEC_SEQ, V7X_LANES)),
            _const_spec((DEC_SEQ, V7X_LANES)),
        ],
        out_specs=pl.BlockSpec((DEC_SEQ, BRANCH_W), lambda i: (i, 0)),
        out_shape=jax.ShapeDtypeStruct((LAT.n_tok, BRANCH_W), BF16),
        scratch_shapes=[pltpu.VMEM((DEC_SEQ, ATT_HEADS * HEAD_DIM), BF16),
                        pltpu.VMEM((DEC_SEQ, kvw), BF16)],
        compiler_params=_cparams(1),
        name="lat_attn",
    )(sink, z, ck, cv, cos_t, sin_t)


def _scan_out_ctx(yf_ref, yb_ref, slab):
    nb, tt = CTX.nb, CTX.tt
    cols = []
    for t in range(tt):
        ys = yf_ref[t] + yb_ref[t]
        cols.append(jnp.concatenate([ys[:, h * nb:(h + 1) * nb] for h in range(RWKV_HEADS)], axis=0))
    by_t = jnp.concatenate(cols, axis=1).T
    n_slab = BRANCH_W // V7X_LANES
    for t in range(tt):
        for s in range(n_slab):
            slab[s, pl.ds(t, nb, stride=tt), :] = by_t[t * nb:(t + 1) * nb, s * V7X_LANES:(s + 1) * V7X_LANES]
    return jnp.concatenate([slab[s] for s in range(n_slab)], axis=1)


def _scan_out_lat(yf_ref, yb_ref, ysum, gy):
    nb, tt = LAT.nb, LAT.tt
    chains = nb * RWKV_HEADS
    ysum[...] = yf_ref[...] + yb_ref[...]
    for vp in range(V7X_SUBLANES):
        a = ysum[pl.ds(vp, tt, stride=V7X_SUBLANES), :].T
        for vl in range(LAT_VL):
            gy[pl.ds(vp + V7X_SUBLANES * vl, chains, stride=HEAD_DIM), :] = a[vl * chains:(vl + 1) * chains, :]
    return jnp.concatenate([gy[b * BRANCH_W:(b + 1) * BRANCH_W, :].T for b in range(nb)], axis=0)


def _mix_kernel(x_ref, mod_ref, g1_ref, oa_ref, ob_ref, od_ref, yf_ref, yb_ref, g_ref, bonus_ref,
                lnxg_ref, lnxb_ref, ones_ref, wg_ref, bg_ref, wb_ref, wo_ref, x1_ref, *scratch, path):
    nb, tt = path.nb, path.tt
    rows = nb * tt
    r0, nr = path.mod_row0, path.mod_rows

    def mod3(j):
        return mod_ref[r0:r0 + nr, j * D_MODEL:(j + 1) * D_MODEL][:, None, :]

    x3 = x_ref[...]
    h3 = _rms(x3, g1_ref[...]) * (1.0 + mod3(1)) + mod3(0)
    hb = h3.reshape(rows, D_MODEL).astype(BF16)
    y = (_scan_out_ctx if path is CTX else _scan_out_lat)(yf_ref, yb_ref, *scratch)
    ones = ones_ref[...]
    mu = _head_sum(y, ones) * (1.0 / HEAD_DIM)
    dl = y - mu
    var = _head_sum(dl * dl, ones) * (1.0 / HEAD_DIM)
    yn = dl * lax.rsqrt(var + GN_EPS) * lnxg_ref[...] + lnxb_ref[...]
    bonus = bonus_ref[...].reshape(rows, BRANCH_W)
    gate_c = g_ref[...].reshape(rows, BRANCH_W)
    oc = ((yn + bonus) * gate_c).astype(BF16)
    branches = (oa_ref[...].reshape(rows, BRANCH_W), ob_ref[...].reshape(rows, BRANCH_W), oc,
                od_ref[...].reshape(rows, BRANCH_W))
    mixed = None
    for n, br in enumerate(branches):
        cols = slice(n * D_MODEL, (n + 1) * D_MODEL)
        gate = jax.nn.sigmoid(_dot(hb, wg_ref[:, cols]) + bg_ref[:, cols])
        term = gate * _dot(br, wb_ref[n])
        mixed = term if mixed is None else mixed + term
    mix = _dot(mixed.astype(BF16), wo_ref[...])
    x1_ref[...] = x3 + mod3(2) * mix.reshape(nb, tt, D_MODEL)


def _mix(x, mod, p, oa, ob, od, yf, yb, g, bonus, ones, path):
    nb, tt, seq = path.nb, path.tt, path.seq
    tokw = pl.BlockSpec((nb, tt, BRANCH_W), lambda i: (0, i, 0))
    tokd = pl.BlockSpec((nb, tt, D_MODEL), lambda i: (0, i, 0))
    n_slab = BRANCH_W // V7X_LANES
    if path is CTX:
        y_specs = [pl.BlockSpec((None, tt, HEAD_DIM, V7X_LANES), lambda i, d=d: (d, i, 0, 0)) for d in range(N_DIR)]
        scratch = [pltpu.VMEM((n_slab, nb * tt, V7X_LANES), F32)]
    else:
        y_specs = [pl.BlockSpec((tt * V7X_SUBLANES, V7X_LANES), lambda i: (i, 0))] * N_DIR
        scratch = [pltpu.VMEM((tt * V7X_SUBLANES, V7X_LANES), F32),
                   pltpu.VMEM((nb * BRANCH_W, V7X_LANES), F32)]
    as3 = lambda a: a.reshape(nb, seq, a.shape[-1])
    out = pl.pallas_call(
        functools.partial(_mix_kernel, path=path),
        grid=(seq // tt,),
        in_specs=[
            tokd,
            _const_spec((MOD_ROWS, N_MOD * D_MODEL)),
            _const_spec((1, D_MODEL)),
            tokw, tokw, tokw, *y_specs, tokw, tokw,
            _const_spec((1, BRANCH_W)),
            _const_spec((1, BRANCH_W)),
            _const_spec((BRANCH_W, BRANCH_W)),
            _const_spec((D_MODEL, N_BRANCH * D_MODEL)),
            _const_spec((1, N_BRANCH * D_MODEL)),
            _const_spec((N_BRANCH, BRANCH_W, D_MODEL)),
            _const_spec((D_MODEL, D_MODEL)),
        ],
        out_specs=tokd,
        out_shape=jax.ShapeDtypeStruct((nb, seq, D_MODEL), F32),
        scratch_shapes=scratch,
        compiler_params=_cparams(1),
        name=f"branch_mix_{seq}",
    )(as3(x), mod, p["norm1_g"], as3(oa), as3(ob), as3(od), yf, yb, g, bonus, p["rwkv_lnx_g"], p["rwkv_lnx_b"],
      ones, p["w_gate"], p["b_gate"], p["w_branch"], p["w_out"])
    return out.reshape(path.n_tok, D_MODEL)


ROUTER_LANES = V7X_LANES


def _route(logits):
    lane = lax.broadcasted_iota(jnp.int32, logits.shape, 1).astype(F32)
    ninf = -jnp.inf
    big = float(ROUTER_LANES)
    gmask = lane < N_GROUPS
    gl = jnp.where(gmask, logits, ninf)
    gmax = jnp.max(gl, axis=-1, keepdims=True)
    gidx = jnp.min(jnp.where(gl == gmax, lane, big), axis=-1, keepdims=True)
    g_w = 1.0 / jnp.sum(jnp.where(gmask, jnp.exp(gl - gmax), 0.0), axis=-1, keepdims=True)
    egroup = jnp.floor((lane - N_GROUPS) * (1.0 / EXPERTS_PER_GROUP))
    emask = (lane >= N_GROUPS) & (lane < N_GROUPS + N_EXPERTS) & (egroup == gidx)
    el = jnp.where(emask, logits, ninf)
    e1 = jnp.max(el, axis=-1, keepdims=True)
    i1 = jnp.min(jnp.where(emask & (el == e1), lane, big), axis=-1, keepdims=True)
    el2 = jnp.where(lane == i1, ninf, el)
    e2 = jnp.max(el2, axis=-1, keepdims=True)
    i2 = jnp.min(jnp.where(emask & (lane != i1) & (el2 == e2), lane, big), axis=-1, keepdims=True)
    t = jnp.exp(e2 - e1)
    den = 1.0 + t
    return jnp.where(lane == i1, g_w * (1.0 / den), 0.0) + jnp.where(lane == i2, g_w * (t / den), 0.0)


def _moe_input(x1, mod_ref, row, g2_ref, wrh_ref, wrl_ref, br_ref):
    h2 = _rms(x1, g2_ref[...]) * (1.0 + _mod_slice(mod_ref, row, 4)) + _mod_slice(mod_ref, row, 3)
    return h2.astype(BF16), _route(_dot_split(h2, wrh_ref[...], wrl_ref[...]) + br_ref[...])


def _expert(hb, weg, weu, wed, c):
    hg = _dot(hb, weg)
    hu = _dot(hb, weu)
    return _dot((hg * jax.nn.sigmoid(hg) * hu * c).astype(BF16), wed)


def _moe_finish(x1, acc, mod_ref, row, fg_ref, out_refs, final):
    x2 = x1 + _mod_slice(mod_ref, row, 5) * acc
    out_refs[0][...] = x2
    if final:
        out_refs[1][...] = _rms(x2, fg_ref[...])


def _moe_kernel(x1_ref, mod_ref, g2_ref, wrh_ref, wrl_ref, br_ref, weg_ref, weu_ref, wed_ref, fg_ref,
                *out_refs, path, final):
    row = _mod_row(pl.program_id(0), path)
    x1 = x1_ref[...]
    hb, comb = _moe_input(x1, mod_ref, row, g2_ref, wrh_ref, wrl_ref, br_ref)
    acc = None
    for e in range(N_EXPERTS):
        lane = N_GROUPS + e
        term = _expert(hb, weg_ref[e], weu_ref[e], wed_ref[e], comb[:, lane:lane + 1])
        acc = term if acc is None else acc + term
    _moe_finish(x1, acc, mod_ref, row, fg_ref, out_refs, final)


def _moe_in_specs(tm):
    single = pl.Buffered(1)
    up = pl.BlockSpec((N_EXPERTS, D_MODEL, EXPERT_FF), lambda i: (0, 0, 0), pipeline_mode=single)
    return [
        pl.BlockSpec((tm, D_MODEL), lambda i: (i, 0)),
        _const_spec((MOD_ROWS, N_MOD * D_MODEL)),
        _const_spec((1, D_MODEL)),
        _const_spec((D_MODEL, ROUTER_LANES)),
        _const_spec((D_MODEL, ROUTER_LANES)),
        _const_spec((1, ROUTER_LANES)),
        up, up,
        pl.BlockSpec((N_EXPERTS, EXPERT_FF, D_MODEL), lambda i: (0, 0, 0), pipeline_mode=single),
        _const_spec((1, D_MODEL)),
    ]


def _moe_args(x1, mod, p, final_g):
    return (x1, mod, p["norm2_g"], *p["w_router"], p["b_router"], p["w_e_gate"], p["w_e_up"], p["w_e_down"], final_g)


def _moe(x1, mod, p, final_g, final, path):
    tm = TOK_TILE
    tokd = pl.BlockSpec((tm, D_MODEL), lambda i: (i, 0))
    out = jax.ShapeDtypeStruct((path.n_tok, D_MODEL), F32)
    return pl.pallas_call(
        functools.partial(_moe_kernel, path=path, final=final),
        grid=(path.n_tok // tm,),
        in_specs=_moe_in_specs(tm),
        out_specs=[tokd, tokd] if final else [tokd],
        out_shape=[out, out] if final else [out],
        compiler_params=_cparams(1),
        name=f"moe_{path.seq}" + ("_final" if final else ""),
    )(*_moe_args(x1, mod, p, final_g))


def _block_diag2(w):
    z = jnp.zeros_like(w[0])
    return jnp.concatenate([jnp.concatenate([w[0], z], axis=1), jnp.concatenate([z, w[1]], axis=1)], axis=0)


def _layer_params(l, a):
    row = lambda v: v.reshape(1, -1)
    perm = jnp.concatenate([a["w_in"][l][:, 0:2048], a["w_in"][l][:, 2176:2688], a["w_in"][l][:, 2048:2176]], axis=1)
    w_router = jnp.zeros((D_MODEL, ROUTER_LANES), F32)
    w_router = w_router.at[:, 0:N_GROUPS].set(a["w_rg"][l]).at[:, N_GROUPS:N_GROUPS + N_EXPERTS].set(a["w_re"][l])
    b_router = jnp.zeros((1, ROUTER_LANES), F32)
    b_router = b_router.at[0, 0:N_GROUPS].set(a["b_rg"][l]).at[0, N_GROUPS:N_GROUPS + N_EXPERTS].set(a["b_re"][l])
    return {
        "norm1_g": row(a["norm1_g"][l]), "norm2_g": row(a["norm2_g"][l]),
        "w_in": perm.astype(BF16),
        "gmlp_ln_g": row(a["gmlp_ln_g"][l]),
        "gmlp_wcat": a["gmlp_ws"][l].transpose(1, 0, 2).reshape(CHUNK, GMLP_GROUPS * CHUNK).astype(BF16),
        "gmlp_bsx": jnp.repeat(a["gmlp_bs"][l].T, BRANCH_W // GMLP_GROUPS, axis=1),
        "conv_w": a["conv_w"][l], "conv_b": row(a["conv_b"][l]),
        "conv_ln_g": row(a["conv_ln_g"][l]), "conv_ln_b": row(a["conv_ln_b"][l]),
        "rwkv_w0": row(a["rwkv_w0"][l]), "rwkv_w2blk": jnp.stack(_split(_block_diag2(a["rwkv_w2"][l]))),
        "rwkv_a0": row(a["rwkv_a0"][l]), "rwkv_a2blk": jnp.stack(_split(_block_diag2(a["rwkv_a2"][l]))),
        "rwkv_g2": jnp.stack(_split(a["rwkv_g2"][l])), "rwkv_kk": row(a["rwkv_kk"][l]), "rwkv_ka": row(a["rwkv_ka"][l]),
        "rwkv_rk": row(a["rwkv_rk"][l]),
        "rwkv_lnx_g": row(a["rwkv_lnx_g"][l]), "rwkv_lnx_b": row(a["rwkv_lnx_b"][l]),
        "attn_sink": a["attn_sink"][l],
        "w_gate": a["w_gate"][l].astype(BF16), "b_gate": row(a["b_gate"][l]),
        "w_branch": a["w_branch"][l].astype(BF16), "w_out": a["w_out"][l].astype(BF16),
        "w_router": _split(w_router), "b_router": b_router,
        "w_e_gate": a["w_e_gate"][l].astype(BF16), "w_e_up": a["w_e_up"][l].astype(BF16),
        "w_e_down": a["w_e_down"][l].astype(BF16),
    }


def _rope_tables():
    half = HEAD_DIM // 4
    inv = ROPE_BASE ** (-jnp.arange(half, dtype=F32) / half)
    t = jnp.arange(DEC_SEQ)
    row = (t // GRID_W).astype(F32)[:, None] * inv[None, :]
    col = (t % GRID_W).astype(F32)[:, None] * inv[None, :]
    cos_h = jnp.concatenate([jnp.cos(row), jnp.cos(row), jnp.cos(col), jnp.cos(col)], axis=1)
    sin_h = jnp.concatenate([-jnp.sin(row), jnp.sin(row), -jnp.sin(col), jnp.sin(col)], axis=1)
    reps = V7X_LANES // HEAD_DIM
    return jnp.tile(cos_h, (1, reps)), jnp.tile(sin_h, (1, reps))


def _mixers(x, mod, p, path, attn):
    z, *kv = _inproj(x, mod, p["norm1_g"], p["w_in"], path)
    oa, ob = _local_mix(z, p, path)
    return z, kv, oa, ob, attn(z)


def kernel(x_prompt, x_sample, cache_k, cache_v, state_rwkv, c, c_ctx, norm1_g, norm2_g, final_norm_g, w_mod, b_mod, w_in, gmlp_ln_g, gmlp_ws, gmlp_bs, conv_w, conv_b, conv_ln_g, conv_ln_b, rwkv_w0, rwkv_w2, rwkv_a0, rwkv_a2, rwkv_g2, rwkv_kk, rwkv_ka, rwkv_rk, rwkv_lnx_g, rwkv_lnx_b, attn_sink, w_gate, b_gate, w_branch, w_out, w_rg, b_rg, w_re, b_re, w_e_gate, w_e_up, w_e_down):
    arrays = dict(norm1_g=norm1_g, norm2_g=norm2_g, w_in=w_in, gmlp_ln_g=gmlp_ln_g, gmlp_ws=gmlp_ws,
                  gmlp_bs=gmlp_bs, conv_w=conv_w, conv_b=conv_b, conv_ln_g=conv_ln_g, conv_ln_b=conv_ln_b,
                  rwkv_w0=rwkv_w0, rwkv_w2=rwkv_w2, rwkv_a0=rwkv_a0, rwkv_a2=rwkv_a2, rwkv_g2=rwkv_g2,
                  rwkv_kk=rwkv_kk, rwkv_ka=rwkv_ka, rwkv_rk=rwkv_rk, rwkv_lnx_g=rwkv_lnx_g,
                  rwkv_lnx_b=rwkv_lnx_b, attn_sink=attn_sink, w_gate=w_gate, b_gate=b_gate,
                  w_branch=w_branch, w_out=w_out, w_rg=w_rg, b_rg=b_rg, w_re=w_re, b_re=b_re,
                  w_e_gate=w_e_gate, w_e_up=w_e_up, w_e_down=w_e_down)
    xc = x_prompt.reshape(CTX.n_tok, D_MODEL)
    xl = x_sample.reshape(LAT.n_tok, D_MODEL)
    cvec = jnp.concatenate([c_ctx[None], c, jnp.zeros((MOD_ROWS - 1 - DEC_BATCH, D_MODEL), F32)], axis=0)
    mod_all = _modulation(cvec, w_mod, b_mod)
    ones = _head_ones()
    cos_t, sin_t = _rope_tables()
    final_g = final_norm_g.reshape(1, D_MODEL)
    kvw = KV_HEADS * HEAD_DIM
    ks_out, vs_out, ss_out = [], [], []
    yc = yl = None
    for l in range(DEPTH):
        p = _layer_params(l, arrays)
        mod = mod_all[l]
        final = l == DEPTH - 1

        ck = cache_k[:, l].reshape(DEC_BATCH, PAST_LEN, kvw)
        cv = cache_v[:, l].reshape(DEC_BATCH, PAST_LEN, kvw)
        zc, kv_c, oa, ob, od = _mixers(xc, mod, p, CTX, lambda z: _ctx_attn(z, p["attn_sink"]))
        g, bonus, r_s, kk_s, v_s, w_s, kd_s, b_s = _prep_ctx(zc, p, ones)
        y_c, s_fin = _scan_ctx(r_s, kk_s, v_s, w_s, kd_s, b_s)
        x1c = _mix(xc, mod, p, oa, ob, od, y_c, y_c, g, bonus, ones, CTX)
        zl, _, oa, ob, od = _mixers(xl, mod, p, LAT, lambda z: _lat_attn(z, p["attn_sink"], ck, cv, cos_t, sin_t))
        g, bonus, v_l, kq = _prep_lat(zl, p, ones)
        s0 = state_rwkv[:, l].reshape(DEC_BATCH, N_DIR, RWKV_HEADS, LAT_VL, V7X_SUBLANES, HEAD_DIM)
        s0 = s0.transpose(1, 5, 4, 3, 0, 2).reshape(N_DIR, HEAD_DIM, V7X_SUBLANES, V7X_LANES)
        yf, yb = _scan_lat(kq, v_l, s0)
        outs_c = _moe(x1c, mod, p, final_g, final, CTX)
        x1l = _mix(xl, mod, p, oa, ob, od, yf, yb, g, bonus, ones, LAT)
        outs_l = _moe(x1l, mod, p, final_g, final, LAT)
        xc, xl = outs_c[0], outs_l[0]
        if final:
            yc, yl = outs_c[1], outs_l[1]
        ks_out.append(kv_c[0].reshape(BATCH, SEQ, KV_HEADS, HEAD_DIM))
        vs_out.append(kv_c[1].reshape(BATCH, SEQ, KV_HEADS, HEAD_DIM))
        ss_out.append(s_fin.reshape(N_DIR, HEAD_DIM, HEAD_DIM, RWKV_HEADS, BATCH).transpose(4, 0, 3, 2, 1))
    y_prompt = yc.reshape(BATCH, SEQ, D_MODEL)
    y_sample = yl.reshape(DEC_BATCH, DEC_SEQ, D_MODEL)
    return (y_prompt, y_sample, jnp.stack(ks_out, axis=1), jnp.stack(vs_out, axis=1), jnp.stack(ss_out, axis=1))
```

```python
import functools
import math
from typing import NamedTuple

import jax
import jax.numpy as jnp
from jax import lax
from jax.experimental import pallas as pl
from jax.experimental.pallas import tpu as pltpu

D_MODEL = 1024
BATCH = 32
SEQ = 256
DEPTH = 2
DEC_BATCH = 4
DEC_SEQ = 1024
PAST_LEN = 256
GRID_W = 64
HEAD_DIM = 64
BRANCH_W = 256
N_BRANCH = 4
CHUNK = 128
GMLP_GROUPS = 4
CONV_W = 31
RWKV_HEADS = 4
N_DIR = 2
DECAY_LORA = 64
ICL_LORA = 64
GATE_LORA = 128
ATT_HEADS = 4
KV_HEADS = 2
Q_PER_KV = ATT_HEADS // KV_HEADS
WINDOW = 128
BLOCK = 128
ROPE_BASE = 10000.0
ATT_SCALE = HEAD_DIM ** -0.5
N_GROUPS = 4
EXPERTS_PER_GROUP = 4
N_EXPERTS = N_GROUPS * EXPERTS_PER_GROUP
EXPERT_FF = 256
N_MOD = 6
RMS_EPS = 1e-6
LN_EPS = 1e-5
GN_EPS = 64e-5
IN_COLS = 2688

V7X_LANES = 128
V7X_SUBLANES = 8
V7X_VMEM_LIMIT = 56 * 1024 * 1024

MOD_ROWS = 8
TOK_TILE = 512
SCAN_TB = 32
CONV_PAD = 16

Z_AB, Z_C, Z_D, Z_G = 0, 1024, 2048, 2560

LAT_Q_R, LAT_Q_KK, LAT_Q_W, LAT_Q_KD, LAT_Q_B = 0, 1, 2, 4, 6
LAT_NQ = 8
LAT_Q_PER_STEP = 2
LAT_VL = V7X_LANES // (DEC_BATCH * RWKV_HEADS)

F32 = jnp.float32
BF16 = jnp.bfloat16
HIGHEST = lax.Precision.HIGHEST


class _Path(NamedTuple):
    nb: int
    seq: int
    mod_row0: int
    mod_rows: int
    tt: int

    @property
    def n_tok(self):
        return self.nb * self.seq


CTX = _Path(BATCH, SEQ, 0, 1, TOK_TILE // BATCH)
LAT = _Path(DEC_BATCH, DEC_SEQ, 1, DEC_BATCH, TOK_TILE // DEC_BATCH)


def _cparams(n_axes, vmem=V7X_VMEM_LIMIT):
    return pltpu.CompilerParams(dimension_semantics=("arbitrary",) * n_axes, vmem_limit_bytes=vmem)


def _const_spec(shape):
    nd = len(shape)
    return pl.BlockSpec(shape, lambda *_: (0,) * nd)


def _dot(a, b):
    return jnp.dot(a, b, preferred_element_type=F32)


def _dot_hi(a, b):
    return jnp.dot(a, b, preferred_element_type=F32, precision=HIGHEST)


def _split(x):
    hi = x.astype(BF16)
    return hi, (x - hi.astype(F32)).astype(BF16)


def _dot_split(a, b_hi, b_lo):
    a_hi, a_lo = _split(a)
    return _dot(a_hi, b_hi) + (_dot(a_lo, b_hi) + _dot(a_hi, b_lo))


def _head_sum(x, ones_b):
    hi, lo = _split(x)
    return _dot(hi, ones_b) + _dot(lo, ones_b)


def _head_ones():
    r = lax.broadcasted_iota(jnp.int32, (BRANCH_W, BRANCH_W), 0) // HEAD_DIM
    c = lax.broadcasted_iota(jnp.int32, (BRANCH_W, BRANCH_W), 1) // HEAD_DIM
    return (r == c).astype(BF16)


def _mod_row(tile, path):
    if path.mod_rows == 1:
        return path.mod_row0
    return path.mod_row0 + tile // (path.seq // TOK_TILE)


def _mod_slice(mod_ref, row, j):
    return mod_ref[pl.ds(row, 1), j * D_MODEL:(j + 1) * D_MODEL]


def _rms(x, g):
    return x * lax.rsqrt(jnp.mean(x * x, axis=-1, keepdims=True) + RMS_EPS) * g


def _layernorm(x, g, b=None, eps=LN_EPS):
    mu = jnp.mean(x, axis=-1, keepdims=True)
    d = x - mu
    var = jnp.mean(d * d, axis=-1, keepdims=True)
    y = d * lax.rsqrt(var + eps) * g
    return y if b is None else y + b


def _mod_kernel(c_ref, w_ref, b_ref, o_ref):
    c = c_ref[...]
    a = c * jax.nn.sigmoid(c)
    o_ref[0] = _dot_hi(a, w_ref[0]) + b_ref[0]


def _modulation(cvec, w_mod, b_mod):
    cols = 2 * D_MODEL
    return pl.pallas_call(
        _mod_kernel,
        grid=(DEPTH, N_MOD * D_MODEL // cols),
        in_specs=[
            pl.BlockSpec((MOD_ROWS, D_MODEL), lambda l, j: (0, 0)),
            pl.BlockSpec((1, D_MODEL, cols), lambda l, j: (l, 0, j)),
            pl.BlockSpec((1, 1, cols), lambda l, j: (l, 0, j)),
        ],
        out_specs=pl.BlockSpec((1, MOD_ROWS, cols), lambda l, j: (l, 0, j)),
        out_shape=jax.ShapeDtypeStruct((DEPTH, MOD_ROWS, N_MOD * D_MODEL), F32),
        compiler_params=_cparams(2),
        name="modulation",
    )(cvec, w_mod, b_mod.reshape(DEPTH, 1, N_MOD * D_MODEL))


def _inproj_kernel(x_ref, mod_ref, g_ref, w_ref, z_ref, *kv_refs, path):
    row = _mod_row(pl.program_id(0), path)
    h = _rms(x_ref[...], g_ref[...]) * (1.0 + _mod_slice(mod_ref, row, 1)) + _mod_slice(mod_ref, row, 0)
    z = _dot(h.astype(BF16), w_ref[...])
    z_ref[...] = z
    kvw = KV_HEADS * HEAD_DIM
    for j, ref in enumerate(kv_refs):
        off = Z_D + ATT_HEADS * HEAD_DIM + j * kvw
        ref[...] = z[:, off:off + kvw]


def _inproj(x, mod, g1, w_in_b, path):
    tm = TOK_TILE
    kvw = KV_HEADS * HEAD_DIM
    n_kv = 2 if path is CTX else 0
    return pl.pallas_call(
        functools.partial(_inproj_kernel, path=path),
        grid=(path.n_tok // tm,),
        in_specs=[
            pl.BlockSpec((tm, D_MODEL), lambda i: (i, 0)),
            _const_spec((MOD_ROWS, N_MOD * D_MODEL)),
            _const_spec((1, D_MODEL)),
            _const_spec((D_MODEL, IN_COLS)),
        ],
        out_specs=[pl.BlockSpec((tm, IN_COLS), lambda i: (i, 0))] + [pl.BlockSpec((tm, kvw), lambda i: (i, 0))] * n_kv,
        out_shape=[jax.ShapeDtypeStruct((path.n_tok, IN_COLS), F32)]
        + [jax.ShapeDtypeStruct((path.n_tok, kvw), F32)] * n_kv,
        compiler_params=_cparams(1),
        name=f"inproj_{path.seq}",
    )(x, mod, g1, w_in_b)


def _local_mix_kernel(z_ref, lng_ref, wcat_ref, bsx_ref, cw_ref, cb_ref, clg_ref, clb_ref,
                      oa_ref, ob_ref, ypad, yshift, *, seq_len):
    lane_group = lax.broadcasted_iota(jnp.int32, (CHUNK, BRANCH_W), 1) // (BRANCH_W // GMLP_GROUPS)
    for c in range(seq_len // CHUNK):
        rows = pl.ds(c * CHUNK, CHUNK)
        u = jax.nn.gelu(z_ref[rows, 0:BRANCH_W], approximate=True)
        v = jax.nn.gelu(z_ref[rows, BRANCH_W:2 * BRANCH_W], approximate=True)
        vn = _layernorm(v, lng_ref[...])
        vblk = jnp.concatenate(
            [jnp.where(lane_group == g, vn, 0.0) for g in range(GMLP_GROUPS)], axis=0).astype(BF16)
        mixed = _dot(wcat_ref[...], vblk) + bsx_ref[...]
        oa_ref[rows, :] = (u * mixed).astype(BF16)

    n8 = V7X_SUBLANES
    ypad[0:CONV_PAD, :] = jnp.zeros((CONV_PAD, BRANCH_W), F32)
    ypad[CONV_PAD + seq_len:, :] = jnp.zeros((CONV_PAD + n8, BRANCH_W), F32)
    ypad[CONV_PAD:CONV_PAD + seq_len, :] = (
        z_ref[:, 2 * BRANCH_W:3 * BRANCH_W] * jax.nn.sigmoid(z_ref[:, 3 * BRANCH_W:4 * BRANCH_W]))
    span = seq_len + 2 * CONV_PAD
    for r in range(1, n8):
        yshift[r - 1] = ypad[r:r + span, :]
    base = CONV_PAD - CONV_W // 2
    for c in range(seq_len // CHUNK):
        acc = jnp.zeros((CHUNK, BRANCH_W), F32)
        for j in range(CONV_W):
            off = c * CHUNK + base + j
            r = off % n8
            rows = pl.ds(off - r, CHUNK)
            acc = acc + cw_ref[j:j + 1, :] * (ypad[rows, :] if r == 0 else yshift[r - 1, rows, :])
        y = _layernorm(acc + cb_ref[...], clg_ref[...], clb_ref[...])
        ob_ref[pl.ds(c * CHUNK, CHUNK), :] = (y * jax.nn.sigmoid(y)).astype(BF16)


def _local_mix(z, p, path):
    out = jax.ShapeDtypeStruct((path.n_tok, BRANCH_W), BF16)
    return pl.pallas_call(
        functools.partial(_local_mix_kernel, seq_len=path.seq),
        grid=(path.nb,),
        in_specs=[
            pl.BlockSpec((path.seq, 4 * BRANCH_W), lambda i: (i, Z_AB // (4 * BRANCH_W))),
            _const_spec((1, BRANCH_W)),
            _const_spec((CHUNK, GMLP_GROUPS * CHUNK)),
            _const_spec((CHUNK, BRANCH_W)),
            _const_spec((CONV_W, BRANCH_W)),
            _const_spec((1, BRANCH_W)),
            _const_spec((1, BRANCH_W)),
            _const_spec((1, BRANCH_W)),
        ],
        out_specs=[pl.BlockSpec((path.seq, BRANCH_W), lambda i: (i, 0))] * 2,
        out_shape=[out, out],
        scratch_shapes=[pltpu.VMEM((path.seq + 2 * CONV_PAD + V7X_SUBLANES, BRANCH_W), F32),
                        pltpu.VMEM((V7X_SUBLANES - 1, path.seq + 2 * CONV_PAD, BRANCH_W), F32)],
        compiler_params=_cparams(1),
        name=f"local_mix_{path.seq}",
    )(z, p["gmlp_ln_g"], p["gmlp_wcat"], p["gmlp_bsx"], p["conv_w"], p["conv_b"],
      p["conv_ln_g"], p["conv_ln_b"])


def _rwkv_quantities(zc, gd, w0_ref, w2_ref, a0_ref, a2_ref, g2_ref, kkp_ref, ka_ref, rk_ref, ones):
    W = BRANCH_W
    r = zc[:, 0:W]
    k = zc[:, W:2 * W]
    v = zc[:, 2 * W:3 * W]
    wd = zc[:, 3 * W:3 * W + N_DIR * DECAY_LORA]
    ad = zc[:, 3 * W + N_DIR * DECAY_LORA:4 * W]
    w_raw = _dot_split(jnp.tanh(wd), w2_ref[0], w2_ref[1]) + w0_ref[...]
    decay = jnp.exp(-math.exp(-0.5) * jax.nn.sigmoid(w_raw))
    a = jax.nn.sigmoid(_dot_split(ad, a2_ref[0], a2_ref[1]) + a0_ref[...])
    g = _dot_split(jax.nn.sigmoid(gd), g2_ref[0], g2_ref[1])
    kkr = k * kkp_ref[...]
    kk = kkr / jnp.maximum(jnp.sqrt(_head_sum(kkr * kkr, ones)), 1e-12)
    ka = ka_ref[...]
    w, kd, b = [], [], []
    for d in range(N_DIR):
        a_d = a[:, d * W:(d + 1) * W]
        w.append(decay[:, d * W:(d + 1) * W])
        kd.append(k * (1.0 + (a_d - 1.0) * ka))
        b.append(kk * a_d)
    bonus = _head_sum(r * (kd[0] + kd[1]) * rk_ref[...], ones) * v
    return g, bonus, r, kk, v, w, kd, b


def _prep_ctx_kernel(zc_ref, zg_ref, w0_ref, w2_ref, a0_ref, a2_ref, g2_ref, kkp_ref, ka_ref, rk_ref,
                     ones_ref, g_o, bonus_o, r_o, kk_o, v_o, w_o, kd_o, b_o, slab):
    nb, tt = CTX.nb, CTX.tt
    rows = nb * tt
    zc = zc_ref[...].reshape(rows, 4 * BRANCH_W)
    gd = zg_ref[...].reshape(rows, GATE_LORA)
    g, bonus, r, kk, v, w, kd, b = _rwkv_quantities(
        zc, gd, w0_ref, w2_ref, a0_ref, a2_ref, g2_ref, kkp_ref, ka_ref, rk_ref, ones_ref[...])
    g_o[...] = g.reshape(nb, tt, BRANCH_W)
    bonus_o[...] = bonus.reshape(nb, tt, BRANCH_W)
    n_slab = BRANCH_W // V7X_LANES
    lanes_per_head = V7X_LANES // RWKV_HEADS

    def to_chains(q, put):
        for s in range(n_slab):
            slab[s] = q[:, s * V7X_LANES:(s + 1) * V7X_LANES]
        by_t = jnp.concatenate(
            [jnp.concatenate([slab[s, pl.ds(t, nb, stride=tt), :] for s in range(n_slab)], axis=1)
             for t in range(tt)], axis=0)
        tr = by_t.T
        lane = lax.broadcasted_iota(jnp.int32, (HEAD_DIM, V7X_LANES), 1)
        low_half = lane < V7X_LANES // 2
        even_quarter = (lane // nb) % 2 == 0
        roll = lambda a, s: pltpu.roll(a, s, 1)
        for c in range(tt // RWKV_HEADS):
            x0, x1, x2, x3 = (tr[h * HEAD_DIM:(h + 1) * HEAD_DIM, c * V7X_LANES:(c + 1) * V7X_LANES]
                              for h in range(RWKV_HEADS))
            y0 = jnp.where(low_half, x0, roll(x2, 2 * nb))
            y2 = jnp.where(low_half, roll(x0, 2 * nb), x2)
            y1 = jnp.where(low_half, x1, roll(x3, 2 * nb))
            y3 = jnp.where(low_half, roll(x1, 2 * nb), x3)
            put(4 * c + 0, jnp.where(even_quarter, y0, roll(y1, nb)))
            put(4 * c + 1, jnp.where(even_quarter, roll(y0, 3 * nb), y1))
            put(4 * c + 2, jnp.where(even_quarter, y2, roll(y3, nb)))
            put(4 * c + 3, jnp.where(even_quarter, roll(y2, 3 * nb), y3))
    assert lanes_per_head == nb and RWKV_HEADS == 4

    def put_into(ref, *lead):
        def put(t, tile):
            ref[(*lead, t)] = tile
        return put

    to_chains(r, put_into(r_o))
    to_chains(kk, put_into(kk_o))
    to_chains(v, put_into(v_o))
    for d in range(N_DIR):
        to_chains(w[d], put_into(w_o, d))
        to_chains(kd[d], put_into(kd_o, d))
        to_chains(b[d], put_into(b_o, d))


def _prep_lat_kernel(zc_ref, zg_ref, w0_ref, w2_ref, a0_ref, a2_ref, g2_ref, kkp_ref, ka_ref, rk_ref,
                     ones_ref, g_o, bonus_o, v_o, kq_o, gk, gv):
    nb, tt = LAT.nb, LAT.tt
    rows = nb * tt
    q0 = pl.program_id(1) * LAT_Q_PER_STEP
    chains = nb * RWKV_HEADS

    @pl.when(q0 == 0)
    def _():
        zc = zc_ref[...].reshape(rows, 4 * BRANCH_W)
        gd = zg_ref[...].reshape(rows, GATE_LORA)
        g, bonus, r, kk, v, w, kd, b = _rwkv_quantities(
            zc, gd, w0_ref, w2_ref, a0_ref, a2_ref, g2_ref, kkp_ref, ka_ref, rk_ref, ones_ref[...])
        g_o[...] = g.reshape(nb, tt, BRANCH_W)
        bonus_o[...] = bonus.reshape(nb, tt, BRANCH_W)
        k_indexed = {LAT_Q_R: r, LAT_Q_KK: kk, LAT_Q_W: w[0], LAT_Q_W + 1: w[1],
                     LAT_Q_KD: kd[0], LAT_Q_KD + 1: kd[1], LAT_Q_B: b[0], LAT_Q_B + 1: b[1]}
        for bi in range(nb):
            for q_idx, q in k_indexed.items():
                gk[q_idx, bi * BRANCH_W:(bi + 1) * BRANCH_W, :] = q[bi * tt:(bi + 1) * tt, :].T
            gv[bi * BRANCH_W:(bi + 1) * BRANCH_W, :] = v[bi * tt:(bi + 1) * tt, :].T
        for vp in range(V7X_SUBLANES):
            a = jnp.concatenate(
                [gv[pl.ds(vp + V7X_SUBLANES * vl, chains, stride=HEAD_DIM), :] for vl in range(LAT_VL)], axis=0)
            v_o[pl.ds(vp, tt, stride=V7X_SUBLANES), :] = a.T

    def body(k, carry):
        n8 = V7X_SUBLANES
        for dq in range(LAT_Q_PER_STEP):
            a = gk[q0 + dq, pl.ds(k, chains, stride=HEAD_DIM), :]
            kq_o[dq, k // n8, pl.ds(k % n8, tt, stride=n8), :] = jnp.concatenate([a] * LAT_VL, axis=0).T
        return carry
    lax.fori_loop(0, HEAD_DIM, body, 0, unroll=4)


def _prep_param_specs():
    return [
        _const_spec((1, N_DIR * BRANCH_W)),
        _const_spec((2, N_DIR * DECAY_LORA, N_DIR * BRANCH_W)),
        _const_spec((1, N_DIR * BRANCH_W)),
        _const_spec((2, N_DIR * ICL_LORA, N_DIR * BRANCH_W)),
        _const_spec((2, GATE_LORA, BRANCH_W)),
        _const_spec((1, BRANCH_W)),
        _const_spec((1, BRANCH_W)),
        _const_spec((1, BRANCH_W)),
        _const_spec((BRANCH_W, BRANCH_W)),
    ]


def _prep_params(p, ones):
    return (p["rwkv_w0"], p["rwkv_w2blk"], p["rwkv_a0"], p["rwkv_a2blk"], p["rwkv_g2"],
            p["rwkv_kk"], p["rwkv_ka"], p["rwkv_rk"], ones)


def _prep_ctx(z, p, ones):
    nb, tt, seq = CTX.nb, CTX.tt, CTX.seq
    z3 = z.reshape(nb, seq, IN_COLS)
    tok = jax.ShapeDtypeStruct((nb, seq, BRANCH_W), F32)
    tok_spec = pl.BlockSpec((nb, tt, BRANCH_W), lambda i: (0, i, 0))
    ch = jax.ShapeDtypeStruct((seq, HEAD_DIM, V7X_LANES), F32)
    ch2 = jax.ShapeDtypeStruct((N_DIR, seq, HEAD_DIM, V7X_LANES), F32)
    ch_spec = pl.BlockSpec((tt, HEAD_DIM, V7X_LANES), lambda i: (i, 0, 0))
    ch2_spec = pl.BlockSpec((N_DIR, tt, HEAD_DIM, V7X_LANES), lambda i: (0, i, 0, 0))
    return pl.pallas_call(
        _prep_ctx_kernel,
        grid=(seq // tt,),
        in_specs=[
            pl.BlockSpec((nb, tt, 4 * BRANCH_W), lambda i: (0, i, Z_C // (4 * BRANCH_W))),
            pl.BlockSpec((nb, tt, GATE_LORA), lambda i: (0, i, Z_G // GATE_LORA)),
        ] + _prep_param_specs(),
        out_specs=[tok_spec, tok_spec, ch_spec, ch_spec, ch_spec, ch2_spec, ch2_spec, ch2_spec],
        out_shape=[tok, tok, ch, ch, ch, ch2, ch2, ch2],
        scratch_shapes=[pltpu.VMEM((BRANCH_W // V7X_LANES, nb * tt, V7X_LANES), F32)],
        compiler_params=_cparams(1),
        name="rwkv_prep_ctx",
    )(z3, z3, *_prep_params(p, ones))


def _prep_lat(z, p, ones):
    nb, tt, seq = LAT.nb, LAT.tt, LAT.seq
    z3 = z.reshape(nb, seq, IN_COLS)
    tok = jax.ShapeDtypeStruct((nb, seq, BRANCH_W), F32)
    tok_spec = pl.BlockSpec((nb, tt, BRANCH_W), lambda i, q: (0, i, 0))
    return pl.pallas_call(
        _prep_lat_kernel,
        grid=(seq // tt, LAT_NQ // LAT_Q_PER_STEP),
        in_specs=[
            pl.BlockSpec((nb, tt, 4 * BRANCH_W), lambda i, q: (0, i, Z_C // (4 * BRANCH_W))),
            pl.BlockSpec((nb, tt, GATE_LORA), lambda i, q: (0, i, Z_G // GATE_LORA)),
        ] + _prep_param_specs(),
        out_specs=[
            tok_spec, tok_spec,
            pl.BlockSpec((tt * V7X_SUBLANES, V7X_LANES), lambda i, q: (i, 0)),
            pl.BlockSpec((LAT_Q_PER_STEP, HEAD_DIM // V7X_SUBLANES, tt * V7X_SUBLANES, V7X_LANES),
                         lambda i, q: (q, 0, i, 0)),
        ],
        out_shape=[tok, tok,
                   jax.ShapeDtypeStruct((seq * V7X_SUBLANES, V7X_LANES), F32),
                   jax.ShapeDtypeStruct((LAT_NQ, HEAD_DIM // V7X_SUBLANES, seq * V7X_SUBLANES, V7X_LANES), F32)],
        scratch_shapes=[pltpu.VMEM((LAT_NQ, nb * BRANCH_W, V7X_LANES), F32),
                        pltpu.VMEM((nb * BRANCH_W, V7X_LANES), F32)],
        compiler_params=_cparams(2),
        name="rwkv_prep_lat",
    )(z3, z3, *_prep_params(p, ones))


SCAN_ACCS = 4


def _strided_sum(terms):
    acc = [None] * SCAN_ACCS
    for j, x in enumerate(terms):
        a = j % SCAN_ACCS
        acc[a] = x if acc[a] is None else acc[a] + x
    while len(acc) > 1:
        acc = [acc[j] + acc[j + 1] for j in range(0, len(acc), 2)]
    return acc[0]


def _scan_ctx_kernel(r_ref, kk_ref, v_ref, w_ref, kd_ref, b_ref, y_ref, sfin_ref, S, *, tb):
    d = pl.program_id(0)
    i = pl.program_id(1)
    n8 = V7X_SUBLANES
    n_v8 = HEAD_DIM // n8

    @pl.when(i == 0)
    def _():
        S[...] = jnp.zeros_like(S)

    def row8(ref, t, k):
        return jnp.broadcast_to(ref[t, pl.ds(k, 1), :], (n8, V7X_LANES))

    def step(s, carry):
        t = s + d * (tb - 1 - 2 * s)

        def pass1(k, sa):
            kkb = row8(kk_ref, t, k)
            return tuple(sa[vo] + S[k, vo * n8:(vo + 1) * n8, :] * kkb for vo in range(n_v8))

        zero = tuple(jnp.zeros((n8, V7X_LANES), F32) for _ in range(n_v8))
        sa = lax.fori_loop(0, HEAD_DIM, pass1, zero, unroll=16)
        nsa = tuple(-x for x in sa)
        vt = tuple(v_ref[t, vo * n8:(vo + 1) * n8, :] for vo in range(n_v8))

        def pass2(k, y):
            wb = row8(w_ref, t, k)
            bb = row8(b_ref, t, k)
            kb = row8(kd_ref, t, k)
            rb = row8(r_ref, t, k)
            out = []
            for vo in range(n_v8):
                sl = slice(vo * n8, (vo + 1) * n8)
                sn = S[k, sl, :] * wb + (nsa[vo] * bb + vt[vo] * kb)
                S[k, sl, :] = sn
                out.append(y[vo] + sn * rb)
            return tuple(out)

        y = lax.fori_loop(0, HEAD_DIM, pass2, zero, unroll=16)
        for vo in range(n_v8):
            y_ref[t, vo * n8:(vo + 1) * n8, :] = y[vo]
        return carry

    lax.fori_loop(0, tb, step, 0)

    @pl.when(i == pl.num_programs(1) - 1)
    def _():
        sfin_ref[...] = S[...]


def _scan_ctx(r_s, kk_s, v_s, w_s, kd_s, b_s):
    seq, tb = CTX.seq, SCAN_TB
    nt = seq // tb

    def tm(d, i):
        return i + d * (nt - 1 - 2 * i)
    shared = pl.BlockSpec((tb, HEAD_DIM, V7X_LANES), lambda d, i: (tm(d, i), 0, 0))
    per_dir = pl.BlockSpec((None, tb, HEAD_DIM, V7X_LANES), lambda d, i: (d, tm(d, i), 0, 0))
    state = pl.BlockSpec((None, HEAD_DIM, HEAD_DIM, V7X_LANES), lambda d, i: (d, 0, 0, 0))
    return pl.pallas_call(
        functools.partial(_scan_ctx_kernel, tb=tb),
        grid=(N_DIR, nt),
        in_specs=[shared, shared, shared, per_dir, per_dir, per_dir],
        out_specs=[per_dir, state],
        out_shape=[jax.ShapeDtypeStruct((N_DIR, seq, HEAD_DIM, V7X_LANES), F32),
                   jax.ShapeDtypeStruct((N_DIR, HEAD_DIM, HEAD_DIM, V7X_LANES), F32)],
        scratch_shapes=[pltpu.VMEM((HEAD_DIM, HEAD_DIM, V7X_LANES), F32)],
        compiler_params=_cparams(2),
        name="rwkv_scan_ctx",
    )(r_s, kk_s, v_s, w_s, kd_s, b_s)


def _lat_scan_dir(S, d, t, r_ref, kk_ref, v_ref, w_ref, kd_ref, b_ref, y_ref):
    n8 = V7X_SUBLANES

    def row8(ref, k):
        return jnp.broadcast_to(ref[k // n8, pl.ds(t * n8 + k % n8, 1), :], (n8, V7X_LANES))
    rows = pl.ds(pl.multiple_of(t * n8, n8), n8)
    nsa = -_strided_sum(S[d, k] * row8(kk_ref, k) for k in range(HEAD_DIM))
    vt = v_ref[rows, :]

    def update(k):
        sn = S[d, k] * row8(w_ref, k) + (nsa * row8(b_ref, k) + vt * row8(kd_ref, k))
        S[d, k] = sn
        return sn * row8(r_ref, k)
    y_ref[rows, :] = _strided_sum(update(k) for k in range(HEAD_DIM))


def _lat_scan_specs():
    tb = SCAN_TB
    nt = LAT.seq // tb
    n8 = V7X_SUBLANES

    def specs(d):
        tblk = (lambda i: i) if d == 0 else (lambda i: nt - 1 - i)
        plane = lambda q: pl.BlockSpec((None, HEAD_DIM // n8, tb * n8, V7X_LANES), lambda i: (q, 0, tblk(i), 0))
        rows = pl.BlockSpec((tb * n8, V7X_LANES), lambda i: (tblk(i), 0))
        return [plane(LAT_Q_R), plane(LAT_Q_KK), rows,
                plane(LAT_Q_W + d), plane(LAT_Q_KD + d), plane(LAT_Q_B + d)], rows
    return specs(0), specs(1)


def _scan_lat_kernel(*refs):
    fwd, bwd = refs[0:6], refs[6:12]
    s0_ref, yf_ref, yb_ref, S = refs[12:]

    @pl.when(pl.program_id(0) == 0)
    def _():
        S[...] = s0_ref[...]

    def step(s, carry):
        _lat_scan_dir(S, 0, s, *fwd, yf_ref)
        _lat_scan_dir(S, 1, SCAN_TB - 1 - s, *bwd, yb_ref)
        return carry
    lax.fori_loop(0, SCAN_TB, step, 0)


def _scan_lat(kq, v_s, s0):
    n8 = V7X_SUBLANES
    (in_f, y_f), (in_b, y_b) = _lat_scan_specs()
    y_shape = jax.ShapeDtypeStruct((LAT.seq * n8, V7X_LANES), F32)
    one = (kq, kq, v_s, kq, kq, kq)
    return pl.pallas_call(
        _scan_lat_kernel,
        grid=(LAT.seq // SCAN_TB,),
        in_specs=in_f + in_b + [_const_spec((N_DIR, HEAD_DIM, n8, V7X_LANES))],
        out_specs=[y_f, y_b],
        out_shape=[y_shape, y_shape],
        scratch_shapes=[pltpu.VMEM((N_DIR, HEAD_DIM, n8, V7X_LANES), F32)],
        compiler_params=_cparams(1),
        name="rwkv_scan_lat",
    )(*one, *one, s0)


def _softmax_pv(scores, vals, sink_col):
    m = sink_col
    for s in scores:
        m = jnp.maximum(m, jnp.max(s, axis=-1, keepdims=True))
    den = jnp.exp(sink_col - m)
    out = None
    for s, vx in zip(scores, vals):
        p = jnp.exp(s - m)
        den = den + jnp.sum(p, axis=-1, keepdims=True)
        o = _dot(p.astype(BF16), vx)
        out = o if out is None else out + o
    return out / den


def _sink_col(sink_ref, kv, n_rows):
    row = lax.broadcasted_iota(jnp.int32, (Q_PER_KV * n_rows, 1), 0)
    col = jnp.full((Q_PER_KV * n_rows, 1), sink_ref[kv * Q_PER_KV], F32)
    for g in range(1, Q_PER_KV):
        col = jnp.where(row >= g * n_rows, sink_ref[kv * Q_PER_KV + g], col)
    return col


def _qk(q2, kh):
    return lax.dot_general(q2, kh, (((1,), (1,)), ((), ())), preferred_element_type=F32) * ATT_SCALE


CTX_ATTN_SEQS = 4


def _ctx_attn_kernel(sink_ref, z_ref, o_ref):
    hd = HEAD_DIM
    for s in range(CTX_ATTN_SEQS):
        rows = slice(s * SEQ, (s + 1) * SEQ)
        q = z_ref[rows, 0:ATT_HEADS * hd]
        outs = []
        for kv in range(KV_HEADS):
            kh = z_ref[rows, ATT_HEADS * hd + kv * hd:ATT_HEADS * hd + (kv + 1) * hd].astype(BF16)
            voff = (ATT_HEADS + KV_HEADS) * hd + kv * hd
            vh = z_ref[rows, voff:voff + hd].astype(BF16)
            q2 = jnp.concatenate(
                [q[:, (kv * Q_PER_KV + g) * hd:(kv * Q_PER_KV + g + 1) * hd] for g in range(Q_PER_KV)],
                axis=0).astype(BF16)
            o2 = _softmax_pv([_qk(q2, kh)], [vh], _sink_col(sink_ref, kv, SEQ))
            outs += [o2[g * SEQ:(g + 1) * SEQ] for g in range(Q_PER_KV)]
        o_ref[rows, :] = jnp.concatenate(outs, axis=1).astype(BF16)


def _rope(x, cos, sin_signed):
    lane = lax.broadcasted_iota(jnp.int32, x.shape, 1)
    first = (lane % (HEAD_DIM // 2)) < (HEAD_DIM // 4)
    partner = jnp.where(first, pltpu.roll(x, V7X_LANES - HEAD_DIM // 4, 1), pltpu.roll(x, HEAD_DIM // 4, 1))
    return x * cos + partner * sin_signed


def _lat_attn_kernel(sink_ref, z_ref, ck_ref, cv_ref, cos_ref, sin_ref, o_ref, q_s, k_s):
    hd = HEAD_DIM
    cos = cos_ref[...]
    sin = sin_ref[...]
    for j in range(ATT_HEADS * hd // V7X_LANES):
        q_s[:, j * V7X_LANES:(j + 1) * V7X_LANES] = _rope(
            z_ref[:, j * V7X_LANES:(j + 1) * V7X_LANES], cos, sin).astype(BF16)
    k_s[...] = _rope(z_ref[:, ATT_HEADS * hd:(ATT_HEADS + KV_HEADS) * hd], cos, sin).astype(BF16)
    voff = (ATT_HEADS + KV_HEADS) * hd
    nb = DEC_SEQ // BLOCK
    for n in range(nb):
        lo = max(n - 1, 0) * BLOCK
        hi = min(n + 2, nb) * BLOCK
        i_abs = n * BLOCK + lax.broadcasted_iota(jnp.int32, (Q_PER_KV * BLOCK, hi - lo), 0) % BLOCK
        j_abs = lo + lax.broadcasted_iota(jnp.int32, (Q_PER_KV * BLOCK, hi - lo), 1)
        band = jnp.abs(i_abs - j_abs) <= WINDOW
        outs = []
        for kv in range(KV_HEADS):
            q2 = jnp.concatenate(
                [q_s[n * BLOCK:(n + 1) * BLOCK, (kv * Q_PER_KV + g) * hd:(kv * Q_PER_KV + g + 1) * hd]
                 for g in range(Q_PER_KV)], axis=0)
            kw = k_s[lo:hi, kv * hd:(kv + 1) * hd]
            vw = z_ref[lo:hi, voff + kv * hd:voff + (kv + 1) * hd].astype(BF16)
            kc = ck_ref[:, kv * hd:(kv + 1) * hd].astype(BF16)
            vc = cv_ref[:, kv * hd:(kv + 1) * hd].astype(BF16)
            s_lat = jnp.where(band, _qk(q2, kw), -1e30)
            o2 = _softmax_pv([s_lat, _qk(q2, kc)], [vw, vc], _sink_col(sink_ref, kv, BLOCK))
            outs += [o2[g * BLOCK:(g + 1) * BLOCK] for g in range(Q_PER_KV)]
        o_ref[n * BLOCK:(n + 1) * BLOCK, :] = jnp.concatenate(outs, axis=1).astype(BF16)


def _ctx_attn(z, sink):
    wd = (ATT_HEADS + 2 * KV_HEADS) * HEAD_DIM
    return pl.pallas_call(
        _ctx_attn_kernel,
        grid=(BATCH // CTX_ATTN_SEQS,),
        in_specs=[
            pl.BlockSpec(memory_space=pltpu.SMEM),
            pl.BlockSpec((CTX_ATTN_SEQS * SEQ, wd), lambda i: (i, Z_D // wd)),
        ],
        out_specs=pl.BlockSpec((CTX_ATTN_SEQS * SEQ, BRANCH_W), lambda i: (i, 0)),
        out_shape=jax.ShapeDtypeStruct((CTX.n_tok, BRANCH_W), BF16),
        compiler_params=_cparams(1),
        name="ctx_attn",
    )(sink, z)


def _lat_attn(z, sink, ck, cv, cos_t, sin_t):
    wd = (ATT_HEADS + 2 * KV_HEADS) * HEAD_DIM
    kvw = KV_HEADS * HEAD_DIM
    return pl.pallas_call(
        _lat_attn_kernel,
        grid=(DEC_BATCH,),
        in_specs=[
            pl.BlockSpec(memory_space=pltpu.SMEM),
            pl.BlockSpec((DEC_SEQ, wd), lambda i: (i, Z_D // wd)),
            pl.BlockSpec((None, PAST_LEN, kvw), lambda i: (i, 0, 0)),
            pl.BlockSpec((None, PAST_LEN, kvw), lambda i: (i, 0, 0)),
            _const_spec((DEC_SEQ, V7X_LANES)),
            _const_spec((DEC_SEQ, V7X_LANES)),
        ],
        out_specs=pl.BlockSpec((DEC_SEQ, BRANCH_W), lambda i: (i, 0)),
        out_shape=jax.ShapeDtypeStruct((LAT.n_tok, BRANCH_W), BF16),
        scratch_shapes=[pltpu.VMEM((DEC_SEQ, ATT_HEADS * HEAD_DIM), BF16),
                        pltpu.VMEM((DEC_SEQ, kvw), BF16)],
        compiler_params=_cparams(1),
        name="lat_attn",
    )(sink, z, ck, cv, cos_t, sin_t)


def _scan_out_ctx(yf_ref, yb_ref, slab):
    nb, tt = CTX.nb, CTX.tt
    cols = []
    for t in range(tt):
        ys = yf_ref[t] + yb_ref[t]
        cols.append(jnp.concatenate([ys[:, h * nb:(h + 1) * nb] for h in range(RWKV_HEADS)], axis=0))
    by_t = jnp.concatenate(cols, axis=1).T
    n_slab = BRANCH_W // V7X_LANES
    for t in range(tt):
        for s in range(n_slab):
            slab[s, pl.ds(t, nb, stride=tt), :] = by_t[t * nb:(t + 1) * nb, s * V7X_LANES:(s + 1) * V7X_LANES]
    return jnp.concatenate([slab[s] for s in range(n_slab)], axis=1)


def _scan_out_lat(yf_ref, yb_ref, ysum, gy):
    nb, tt = LAT.nb, LAT.tt
    chains = nb * RWKV_HEADS
    ysum[...] = yf_ref[...] + yb_ref[...]
    for vp in range(V7X_SUBLANES):
        a = ysum[pl.ds(vp, tt, stride=V7X_SUBLANES), :].T
        for vl in range(LAT_VL):
            gy[pl.ds(vp + V7X_SUBLANES * vl, chains, stride=HEAD_DIM), :] = a[vl * chains:(vl + 1) * chains, :]
    return jnp.concatenate([gy[b * BRANCH_W:(b + 1) * BRANCH_W, :].T for b in range(nb)], axis=0)


def _mix_kernel(x_ref, mod_ref, g1_ref, oa_ref, ob_ref, od_ref, yf_ref, yb_ref, g_ref, bonus_ref,
                lnxg_ref, lnxb_ref, ones_ref, wg_ref, bg_ref, wb_ref, wo_ref, x1_ref, *scratch, path):
    nb, tt = path.nb, path.tt
    rows = nb * tt
    r0, nr = path.mod_row0, path.mod_rows

    def mod3(j):
        return mod_ref[r0:r0 + nr, j * D_MODEL:(j + 1) * D_MODEL][:, None, :]

    x3 = x_ref[...]
    h3 = _rms(x3, g1_ref[...]) * (1.0 + mod3(1)) + mod3(0)
    hb = h3.reshape(rows, D_MODEL).astype(BF16)
    y = (_scan_out_ctx if path is CTX else _scan_out_lat)(yf_ref, yb_ref, *scratch)
    ones = ones_ref[...]
    mu = _head_sum(y, ones) * (1.0 / HEAD_DIM)
    dl = y - mu
    var = _head_sum(dl * dl, ones) * (1.0 / HEAD_DIM)
    yn = dl * lax.rsqrt(var + GN_EPS) * lnxg_ref[...] + lnxb_ref[...]
    bonus = bonus_ref[...].reshape(rows, BRANCH_W)
    gate_c = g_ref[...].reshape(rows, BRANCH_W)
    oc = ((yn + bonus) * gate_c).astype(BF16)
    branches = (oa_ref[...].reshape(rows, BRANCH_W), ob_ref[...].reshape(rows, BRANCH_W), oc,
                od_ref[...].reshape(rows, BRANCH_W))
    mixed = None
    for n, br in enumerate(branches):
        cols = slice(n * D_MODEL, (n + 1) * D_MODEL)
        gate = jax.nn.sigmoid(_dot(hb, wg_ref[:, cols]) + bg_ref[:, cols])
        term = gate * _dot(br, wb_ref[n])
        mixed = term if mixed is None else mixed + term
    mix = _dot(mixed.astype(BF16), wo_ref[...])
    x1_ref[...] = x3 + mod3(2) * mix.reshape(nb, tt, D_MODEL)


def _mix(x, mod, p, oa, ob, od, yf, yb, g, bonus, ones, path):
    nb, tt, seq = path.nb, path.tt, path.seq
    tokw = pl.BlockSpec((nb, tt, BRANCH_W), lambda i: (0, i, 0))
    tokd = pl.BlockSpec((nb, tt, D_MODEL), lambda i: (0, i, 0))
    n_slab = BRANCH_W // V7X_LANES
    if path is CTX:
        y_specs = [pl.BlockSpec((None, tt, HEAD_DIM, V7X_LANES), lambda i, d=d: (d, i, 0, 0)) for d in range(N_DIR)]
        scratch = [pltpu.VMEM((n_slab, nb * tt, V7X_LANES), F32)]
    else:
        y_specs = [pl.BlockSpec((tt * V7X_SUBLANES, V7X_LANES), lambda i: (i, 0))] * N_DIR
        scratch = [pltpu.VMEM((tt * V7X_SUBLANES, V7X_LANES), F32),
                   pltpu.VMEM((nb * BRANCH_W, V7X_LANES), F32)]
    as3 = lambda a: a.reshape(nb, seq, a.shape[-1])
    out = pl.pallas_call(
        functools.partial(_mix_kernel, path=path),
        grid=(seq // tt,),
        in_specs=[
            tokd,
            _const_spec((MOD_ROWS, N_MOD * D_MODEL)),
            _const_spec((1, D_MODEL)),
            tokw, tokw, tokw, *y_specs, tokw, tokw,
            _const_spec((1, BRANCH_W)),
            _const_spec((1, BRANCH_W)),
            _const_spec((BRANCH_W, BRANCH_W)),
            _const_spec((D_MODEL, N_BRANCH * D_MODEL)),
            _const_spec((1, N_BRANCH * D_MODEL)),
            _const_spec((N_BRANCH, BRANCH_W, D_MODEL)),
            _const_spec((D_MODEL, D_MODEL)),
        ],
        out_specs=tokd,
        out_shape=jax.ShapeDtypeStruct((nb, seq, D_MODEL), F32),
        scratch_shapes=scratch,
        compiler_params=_cparams(1),
        name=f"branch_mix_{seq}",
    )(as3(x), mod, p["norm1_g"], as3(oa), as3(ob), as3(od), yf, yb, g, bonus, p["rwkv_lnx_g"], p["rwkv_lnx_b"],
      ones, p["w_gate"], p["b_gate"], p["w_branch"], p["w_out"])
    return out.reshape(path.n_tok, D_MODEL)


ROUTER_LANES = V7X_LANES


def _route(logits):
    lane = lax.broadcasted_iota(jnp.int32, logits.shape, 1).astype(F32)
    ninf = -jnp.inf
    big = float(ROUTER_LANES)
    gmask = lane < N_GROUPS
    gl = jnp.where(gmask, logits, ninf)
    gmax = jnp.max(gl, axis=-1, keepdims=True)
    gidx = jnp.min(jnp.where(gl == gmax, lane, big), axis=-1, keepdims=True)
    g_w = 1.0 / jnp.sum(jnp.where(gmask, jnp.exp(gl - gmax), 0.0), axis=-1, keepdims=True)
    egroup = jnp.floor((lane - N_GROUPS) * (1.0 / EXPERTS_PER_GROUP))
    emask = (lane >= N_GROUPS) & (lane < N_GROUPS + N_EXPERTS) & (egroup == gidx)
    el = jnp.where(emask, logits, ninf)
    e1 = jnp.max(el, axis=-1, keepdims=True)
    i1 = jnp.min(jnp.where(emask & (el == e1), lane, big), axis=-1, keepdims=True)
    el2 = jnp.where(lane == i1, ninf, el)
    e2 = jnp.max(el2, axis=-1, keepdims=True)
    i2 = jnp.min(jnp.where(emask & (lane != i1) & (el2 == e2), lane, big), axis=-1, keepdims=True)
    t = jnp.exp(e2 - e1)
    den = 1.0 + t
    return jnp.where(lane == i1, g_w * (1.0 / den), 0.0) + jnp.where(lane == i2, g_w * (t / den), 0.0)


def _moe_input(x1, mod_ref, row, g2_ref, wrh_ref, wrl_ref, br_ref):
    h2 = _rms(x1, g2_ref[...]) * (1.0 + _mod_slice(mod_ref, row, 4)) + _mod_slice(mod_ref, row, 3)
    return h2.astype(BF16), _route(_dot_split(h2, wrh_ref[...], wrl_ref[...]) + br_ref[...])


def _expert(hb, weg, weu, wed, c):
    hg = _dot(hb, weg)
    hu = _dot(hb, weu)
    return _dot((hg * jax.nn.sigmoid(hg) * hu * c).astype(BF16), wed)


def _moe_finish(x1, acc, mod_ref, row, fg_ref, out_refs, final):
    x2 = x1 + _mod_slice(mod_ref, row, 5) * acc
    out_refs[0][...] = x2
    if final:
        out_refs[1][...] = _rms(x2, fg_ref[...])


def _moe_kernel(x1_ref, mod_ref, g2_ref, wrh_ref, wrl_ref, br_ref, weg_ref, weu_ref, wed_ref, fg_ref,
                *out_refs, path, final):
    row = _mod_row(pl.program_id(0), path)
    x1 = x1_ref[...]
    hb, comb = _moe_input(x1, mod_ref, row, g2_ref, wrh_ref, wrl_ref, br_ref)
    acc = None
    for e in range(N_EXPERTS):
        lane = N_GROUPS + e
        term = _expert(hb, weg_ref[e], weu_ref[e], wed_ref[e], comb[:, lane:lane + 1])
        acc = term if acc is None else acc + term
    _moe_finish(x1, acc, mod_ref, row, fg_ref, out_refs, final)


def _moe_in_specs(tm):
    single = pl.Buffered(1)
    up = pl.BlockSpec((N_EXPERTS, D_MODEL, EXPERT_FF), lambda i: (0, 0, 0), pipeline_mode=single)
    return [
        pl.BlockSpec((tm, D_MODEL), lambda i: (i, 0)),
        _const_spec((MOD_ROWS, N_MOD * D_MODEL)),
        _const_spec((1, D_MODEL)),
        _const_spec((D_MODEL, ROUTER_LANES)),
        _const_spec((D_MODEL, ROUTER_LANES)),
        _const_spec((1, ROUTER_LANES)),
        up, up,
        pl.BlockSpec((N_EXPERTS, EXPERT_FF, D_MODEL), lambda i: (0, 0, 0), pipeline_mode=single),
        _const_spec((1, D_MODEL)),
    ]


def _moe_args(x1, mod, p, final_g):
    return (x1, mod, p["norm2_g"], *p["w_router"], p["b_router"], p["w_e_gate"], p["w_e_up"], p["w_e_down"], final_g)


def _moe(x1, mod, p, final_g, final, path):
    tm = TOK_TILE
    tokd = pl.BlockSpec((tm, D_MODEL), lambda i: (i, 0))
    out = jax.ShapeDtypeStruct((path.n_tok, D_MODEL), F32)
    return pl.pallas_call(
        functools.partial(_moe_kernel, path=path, final=final),
        grid=(path.n_tok // tm,),
        in_specs=_moe_in_specs(tm),
        out_specs=[tokd, tokd] if final else [tokd],
        out_shape=[out, out] if final else [out],
        compiler_params=_cparams(1),
        name=f"moe_{path.seq}" + ("_final" if final else ""),
    )(*_moe_args(x1, mod, p, final_g))


def _block_diag2(w):
    z = jnp.zeros_like(w[0])
    return jnp.concatenate([jnp.concatenate([w[0], z], axis=1), jnp.concatenate([z, w[1]], axis=1)], axis=0)


def _layer_params(l, a):
    row = lambda v: v.reshape(1, -1)
    perm = jnp.concatenate([a["w_in"][l][:, 0:2048], a["w_in"][l][:, 2176:2688], a["w_in"][l][:, 2048:2176]], axis=1)
    w_router = jnp.zeros((D_MODEL, ROUTER_LANES), F32)
    w_router = w_router.at[:, 0:N_GROUPS].set(a["w_rg"][l]).at[:, N_GROUPS:N_GROUPS + N_EXPERTS].set(a["w_re"][l])
    b_router = jnp.zeros((1, ROUTER_LANES), F32)
    b_router = b_router.at[0, 0:N_GROUPS].set(a["b_rg"][l]).at[0, N_GROUPS:N_GROUPS + N_EXPERTS].set(a["b_re"][l])
    return {
        "norm1_g": row(a["norm1_g"][l]), "norm2_g": row(a["norm2_g"][l]),
        "w_in": perm.astype(BF16),
        "gmlp_ln_g": row(a["gmlp_ln_g"][l]),
        "gmlp_wcat": a["gmlp_ws"][l].transpose(1, 0, 2).reshape(CHUNK, GMLP_GROUPS * CHUNK).astype(BF16),
        "gmlp_bsx": jnp.repeat(a["gmlp_bs"][l].T, BRANCH_W // GMLP_GROUPS, axis=1),
        "conv_w": a["conv_w"][l], "conv_b": row(a["conv_b"][l]),
        "conv_ln_g": row(a["conv_ln_g"][l]), "conv_ln_b": row(a["conv_ln_b"][l]),
        "rwkv_w0": row(a["rwkv_w0"][l]), "rwkv_w2blk": jnp.stack(_split(_block_diag2(a["rwkv_w2"][l]))),
        "rwkv_a0": row(a["rwkv_a0"][l]), "rwkv_a2blk": jnp.stack(_split(_block_diag2(a["rwkv_a2"][l]))),
        "rwkv_g2": jnp.stack(_split(a["rwkv_g2"][l])), "rwkv_kk": row(a["rwkv_kk"][l]), "rwkv_ka": row(a["rwkv_ka"][l]),
        "rwkv_rk": row(a["rwkv_rk"][l]),
        "rwkv_lnx_g": row(a["rwkv_lnx_g"][l]), "rwkv_lnx_b": row(a["rwkv_lnx_b"][l]),
        "attn_sink": a["attn_sink"][l],
        "w_gate": a["w_gate"][l].astype(BF16), "b_gate": row(a["b_gate"][l]),
        "w_branch": a["w_branch"][l].astype(BF16), "w_out": a["w_out"][l].astype(BF16),
        "w_router": _split(w_router), "b_router": b_router,
        "w_e_gate": a["w_e_gate"][l].astype(BF16), "w_e_up": a["w_e_up"][l].astype(BF16),
        "w_e_down": a["w_e_down"][l].astype(BF16),
    }


def _rope_tables():
    half = HEAD_DIM // 4
    inv = ROPE_BASE ** (-jnp.arange(half, dtype=F32) / half)
    t = jnp.arange(DEC_SEQ)
    row = (t // GRID_W).astype(F32)[:, None] * inv[None, :]
    col = (t % GRID_W).astype(F32)[:, None] * inv[None, :]
    cos_h = jnp.concatenate([jnp.cos(row), jnp.cos(row), jnp.cos(col), jnp.cos(col)], axis=1)
    sin_h = jnp.concatenate([-jnp.sin(row), jnp.sin(row), -jnp.sin(col), jnp.sin(col)], axis=1)
    reps = V7X_LANES // HEAD_DIM
    return jnp.tile(cos_h, (1, reps)), jnp.tile(sin_h, (1, reps))


def _mixers(x, mod, p, path, attn):
    z, *kv = _inproj(x, mod, p["norm1_g"], p["w_in"], path)
    oa, ob = _local_mix(z, p, path)
    return z, kv, oa, ob, attn(z)


def kernel(x_prompt, x_sample, cache_k, cache_v, state_rwkv, c, c_ctx, norm1_g, norm2_g, final_norm_g, w_mod, b_mod, w_in, gmlp_ln_g, gmlp_ws, gmlp_bs, conv_w, conv_b, conv_ln_g, conv_ln_b, rwkv_w0, rwkv_w2, rwkv_a0, rwkv_a2, rwkv_g2, rwkv_kk, rwkv_ka, rwkv_rk, rwkv_lnx_g, rwkv_lnx_b, attn_sink, w_gate, b_gate, w_branch, w_out, w_rg, b_rg, w_re, b_re, w_e_gate, w_e_up, w_e_down):
    arrays = dict(norm1_g=norm1_g, norm2_g=norm2_g, w_in=w_in, gmlp_ln_g=gmlp_ln_g, gmlp_ws=gmlp_ws,
                  gmlp_bs=gmlp_bs, conv_w=conv_w, conv_b=conv_b, conv_ln_g=conv_ln_g, conv_ln_b=conv_ln_b,
                  rwkv_w0=rwkv_w0, rwkv_w2=rwkv_w2, rwkv_a0=rwkv_a0, rwkv_a2=rwkv_a2, rwkv_g2=rwkv_g2,
                  rwkv_kk=rwkv_kk, rwkv_ka=rwkv_ka, rwkv_rk=rwkv_rk, rwkv_lnx_g=rwkv_lnx_g,
                  rwkv_lnx_b=rwkv_lnx_b, attn_sink=attn_sink, w_gate=w_gate, b_gate=b_gate,
                  w_branch=w_branch, w_out=w_out, w_rg=w_rg, b_rg=b_rg, w_re=w_re, b_re=b_re,
                  w_e_gate=w_e_gate, w_e_up=w_e_up, w_e_down=w_e_down)
    xc = x_prompt.reshape(CTX.n_tok, D_MODEL)
    xl = x_sample.reshape(LAT.n_tok, D_MODEL)
    cvec = jnp.concatenate([c_ctx[None], c, jnp.zeros((MOD_ROWS - 1 - DEC_BATCH, D_MODEL), F32)], axis=0)
    mod_all = _modulation(cvec, w_mod, b_mod)
    ones = _head_ones()
    cos_t, sin_t = _rope_tables()
    final_g = final_norm_g.reshape(1, D_MODEL)
    kvw = KV_HEADS * HEAD_DIM
    ks_out, vs_out, ss_out = [], [], []
    yc = yl = None
    for l in range(DEPTH):
        p = _layer_params(l, arrays)
        mod = mod_all[l]
        final = l == DEPTH - 1

        ck = cache_k[:, l].reshape(DEC_BATCH, PAST_LEN, kvw)
        cv = cache_v[:, l].reshape(DEC_BATCH, PAST_LEN, kvw)
        zc, kv_c, oa, ob, od = _mixers(xc, mod, p, CTX, lambda z: _ctx_attn(z, p["attn_sink"]))
        g, bonus, r_s, kk_s, v_s, w_s, kd_s, b_s = _prep_ctx(zc, p, ones)
        y_c, s_fin = _scan_ctx(r_s, kk_s, v_s, w_s, kd_s, b_s)
        x1c = _mix(xc, mod, p, oa, ob, od, y_c, y_c, g, bonus, ones, CTX)
        zl, _, oa, ob, od = _mixers(xl, mod, p, LAT, lambda z: _lat_attn(z, p["attn_sink"], ck, cv, cos_t, sin_t))
        g, bonus, v_l, kq = _prep_lat(zl, p, ones)
        s0 = state_rwkv[:, l].reshape(DEC_BATCH, N_DIR, RWKV_HEADS, LAT_VL, V7X_SUBLANES, HEAD_DIM)
        s0 = s0.transpose(1, 5, 4, 3, 0, 2).reshape(N_DIR, HEAD_DIM, V7X_SUBLANES, V7X_LANES)
        yf, yb = _scan_lat(kq, v_l, s0)
        outs_c = _moe(x1c, mod, p, final_g, final, CTX)
        x1l = _mix(xl, mod, p, oa, ob, od, yf, yb, g, bonus, ones, LAT)
        outs_l = _moe(x1l, mod, p, final_g, final, LAT)
        xc, xl = outs_c[0], outs_l[0]
        if final:
            yc, yl = outs_c[1], outs_l[1]
        ks_out.append(kv_c[0].reshape(BATCH, SEQ, KV_HEADS, HEAD_DIM))
        vs_out.append(kv_c[1].reshape(BATCH, SEQ, KV_HEADS, HEAD_DIM))
        ss_out.append(s_fin.reshape(N_DIR, HEAD_DIM, HEAD_DIM, RWKV_HEADS, BATCH).transpose(4, 0, 3, 2, 1))
    y_prompt = yc.reshape(BATCH, SEQ, D_MODEL)
    y_sample = yl.reshape(DEC_BATCH, DEC_SEQ, D_MODEL)
    return (y_prompt, y_sample, jnp.stack(ks_out, axis=1), jnp.stack(vs_out, axis=1), jnp.stack(ss_out, axis=1))
```

```python
import functools
import math
from typing import NamedTuple

import jax
import jax.numpy as jnp
from jax import lax
from jax.experimental import pallas as pl
from jax.experimental.pallas import tpu as pltpu

D_MODEL = 1024
BATCH = 32
SEQ = 256
DEPTH = 2
DEC_BATCH = 4
DEC_SEQ = 1024
PAST_LEN = 256
GRID_W = 64
HEAD_DIM = 64
BRANCH_W = 256
N_BRANCH = 4
CHUNK = 128
GMLP_GROUPS = 4
CONV_W = 31
RWKV_HEADS = 4
N_DIR = 2
DECAY_LORA = 64
ICL_LORA = 64
GATE_LORA = 128
ATT_HEADS = 4
KV_HEADS = 2
Q_PER_KV = ATT_HEADS // KV_HEADS
WINDOW = 128
BLOCK = 128
ROPE_BASE = 10000.0
ATT_SCALE = HEAD_DIM ** -0.5
N_GROUPS = 4
EXPERTS_PER_GROUP = 4
N_EXPERTS = N_GROUPS * EXPERTS_PER_GROUP
EXPERT_FF = 256
N_MOD = 6
RMS_EPS = 1e-6
LN_EPS = 1e-5
GN_EPS = 64e-5
IN_COLS = 2688

V7X_LANES = 128
V7X_SUBLANES = 8
V7X_VMEM_LIMIT = 56 * 1024 * 1024

MOD_ROWS = 8
TOK_TILE = 512
SCAN_TB = 32
CONV_PAD = 16

Z_AB, Z_C, Z_D, Z_G = 0, 1024, 2048, 2560

LAT_Q_R, LAT_Q_KK, LAT_Q_W, LAT_Q_KD, LAT_Q_B = 0, 1, 2, 4, 6
LAT_NQ = 8
LAT_Q_PER_STEP = 2
LAT_VL = V7X_LANES // (DEC_BATCH * RWKV_HEADS)

F32 = jnp.float32
BF16 = jnp.bfloat16
HIGHEST = lax.Precision.HIGHEST


class _Path(NamedTuple):
    nb: int
    seq: int
    mod_row0: int
    mod_rows: int
    tt: int

    @property
    def n_tok(self):
        return self.nb * self.seq


CTX = _Path(BATCH, SEQ, 0, 1, TOK_TILE // BATCH)
LAT = _Path(DEC_BATCH, DEC_SEQ, 1, DEC_BATCH, TOK_TILE // DEC_BATCH)


def _cparams(n_axes, vmem=V7X_VMEM_LIMIT):
    return pltpu.CompilerParams(dimension_semantics=("arbitrary",) * n_axes, vmem_limit_bytes=vmem)


def _const_spec(shape):
    nd = len(shape)
    return pl.BlockSpec(shape, lambda *_: (0,) * nd)


def _layer_spec(shape, l, **kw):
    nd = len(shape)
    return pl.BlockSpec((None,) + tuple(shape), lambda *_: (l,) + (0,) * nd, **kw)


def _dot(a, b):
    return jnp.dot(a, b, preferred_element_type=F32)


def _dot_hi(a, b):
    return jnp.dot(a, b, preferred_element_type=F32, precision=HIGHEST)


def _split(x):
    hi = x.astype(BF16)
    return hi, (x - hi.astype(F32)).astype(BF16)


def _dot_split(a, b_hi, b_lo):
    a_hi, a_lo = _split(a)
    return _dot(a_hi, b_hi) + (_dot(a_lo, b_hi) + _dot(a_hi, b_lo))


def _head_sum(x, ones_b):
    hi, lo = _split(x)
    return _dot(hi, ones_b) + _dot(lo, ones_b)


def _head_ones():
    r = lax.broadcasted_iota(jnp.int32, (BRANCH_W, BRANCH_W), 0) // HEAD_DIM
    c = lax.broadcasted_iota(jnp.int32, (BRANCH_W, BRANCH_W), 1) // HEAD_DIM
    return (r == c).astype(BF16)


def _mod_row(tile, path):
    if path.mod_rows == 1:
        return path.mod_row0
    return path.mod_row0 + tile // (path.seq // TOK_TILE)


def _mod_slice(mod_ref, row, j):
    return mod_ref[pl.ds(row, 1), j * D_MODEL:(j + 1) * D_MODEL]


def _rms(x, g):
    return x * lax.rsqrt(jnp.mean(x * x, axis=-1, keepdims=True) + RMS_EPS) * g


def _layernorm(x, g, b=None, eps=LN_EPS):
    mu = jnp.mean(x, axis=-1, keepdims=True)
    d = x - mu
    var = jnp.mean(d * d, axis=-1, keepdims=True)
    y = d * lax.rsqrt(var + eps) * g
    return y if b is None else y + b


def _mod_kernel(c_ref, w_ref, b_ref, o_ref):
    c = c_ref[...]
    a = c * jax.nn.sigmoid(c)
    o_ref[0] = _dot_hi(a, w_ref[0]) + b_ref[0]


def _modulation(cvec, w_mod, b_mod):
    cols = 2 * D_MODEL
    return pl.pallas_call(
        _mod_kernel,
        grid=(DEPTH, N_MOD * D_MODEL // cols),
        in_specs=[
            pl.BlockSpec((MOD_ROWS, D_MODEL), lambda l, j: (0, 0)),
            pl.BlockSpec((1, D_MODEL, cols), lambda l, j: (l, 0, j)),
            pl.BlockSpec((1, 1, cols), lambda l, j: (l, 0, j)),
        ],
        out_specs=pl.BlockSpec((1, MOD_ROWS, cols), lambda l, j: (l, 0, j)),
        out_shape=jax.ShapeDtypeStruct((DEPTH, MOD_ROWS, N_MOD * D_MODEL), F32),
        compiler_params=_cparams(2),
        name="modulation",
    )(cvec, w_mod, b_mod.reshape(DEPTH, 1, N_MOD * D_MODEL))


def _inproj_kernel(x_ref, mod_ref, g_ref, w_ref, z_ref, *kv_refs, path):
    row = _mod_row(pl.program_id(0), path)
    h = _rms(x_ref[...], g_ref[...]) * (1.0 + _mod_slice(mod_ref, row, 1)) + _mod_slice(mod_ref, row, 0)
    z = _dot(h.astype(BF16), w_ref[...])
    z_ref[...] = z
    kvw = KV_HEADS * HEAD_DIM
    for j, ref in enumerate(kv_refs):
        off = Z_D + ATT_HEADS * HEAD_DIM + j * kvw
        ref[...] = z[:, off:off + kvw]


def _inproj(x, mod, p, path):
    tm = TOK_TILE
    l = p["layer"]
    kvw = KV_HEADS * HEAD_DIM
    n_kv = 2 if path is CTX else 0
    return pl.pallas_call(
        functools.partial(_inproj_kernel, path=path),
        grid=(path.n_tok // tm,),
        in_specs=[
            pl.BlockSpec((tm, D_MODEL), lambda i: (i, 0)),
            _layer_spec((MOD_ROWS, N_MOD * D_MODEL), l),
            _layer_spec((1, D_MODEL), l),
            _layer_spec((D_MODEL, IN_COLS), l),
        ],
        out_specs=[pl.BlockSpec((tm, IN_COLS), lambda i: (i, 0))] + [pl.BlockSpec((tm, kvw), lambda i: (i, 0))] * n_kv,
        out_shape=[jax.ShapeDtypeStruct((path.n_tok, IN_COLS), F32)]
        + [jax.ShapeDtypeStruct((path.n_tok, kvw), F32)] * n_kv,
        compiler_params=_cparams(1),
        name=f"inproj_{path.seq}",
    )(x, mod, p["norm1_g"], p["w_in"])


def _local_mix_kernel(z_ref, lng_ref, wcat_ref, bsx_ref, cw_ref, cb_ref, clg_ref, clb_ref,
                      oa_ref, ob_ref, ypad, yshift, *, seq_len):
    lane_group = lax.broadcasted_iota(jnp.int32, (CHUNK, BRANCH_W), 1) // (BRANCH_W // GMLP_GROUPS)
    for c in range(seq_len // CHUNK):
        rows = pl.ds(c * CHUNK, CHUNK)
        u = jax.nn.gelu(z_ref[rows, 0:BRANCH_W], approximate=True)
        v = jax.nn.gelu(z_ref[rows, BRANCH_W:2 * BRANCH_W], approximate=True)
        vn = _layernorm(v, lng_ref[...])
        vblk = jnp.concatenate(
            [jnp.where(lane_group == g, vn, 0.0) for g in range(GMLP_GROUPS)], axis=0).astype(BF16)
        mixed = _dot(wcat_ref[...], vblk) + bsx_ref[...]
        oa_ref[rows, :] = (u * mixed).astype(BF16)

    n8 = V7X_SUBLANES
    ypad[0:CONV_PAD, :] = jnp.zeros((CONV_PAD, BRANCH_W), F32)
    ypad[CONV_PAD + seq_len:, :] = jnp.zeros((CONV_PAD + n8, BRANCH_W), F32)
    ypad[CONV_PAD:CONV_PAD + seq_len, :] = (
        z_ref[:, 2 * BRANCH_W:3 * BRANCH_W] * jax.nn.sigmoid(z_ref[:, 3 * BRANCH_W:4 * BRANCH_W]))
    span = seq_len + 2 * CONV_PAD
    for r in range(1, n8):
        yshift[r - 1] = ypad[r:r + span, :]
    base = CONV_PAD - CONV_W // 2
    for c in range(seq_len // CHUNK):
        acc = jnp.zeros((CHUNK, BRANCH_W), F32)
        for j in range(CONV_W):
            off = c * CHUNK + base + j
            r = off % n8
            rows = pl.ds(off - r, CHUNK)
            acc = acc + cw_ref[j:j + 1, :] * (ypad[rows, :] if r == 0 else yshift[r - 1, rows, :])
        y = _layernorm(acc + cb_ref[...], clg_ref[...], clb_ref[...])
        ob_ref[pl.ds(c * CHUNK, CHUNK), :] = (y * jax.nn.sigmoid(y)).astype(BF16)


def _local_mix(z, p, path):
    l = p["layer"]
    out = jax.ShapeDtypeStruct((path.n_tok, BRANCH_W), BF16)
    return pl.pallas_call(
        functools.partial(_local_mix_kernel, seq_len=path.seq),
        grid=(path.nb,),
        in_specs=[
            pl.BlockSpec((path.seq, 4 * BRANCH_W), lambda i: (i, Z_AB // (4 * BRANCH_W))),
            _layer_spec((1, BRANCH_W), l),
            _layer_spec((CHUNK, GMLP_GROUPS * CHUNK), l),
            _layer_spec((CHUNK, BRANCH_W), l),
            _layer_spec((CONV_W, BRANCH_W), l),
            _layer_spec((1, BRANCH_W), l),
            _layer_spec((1, BRANCH_W), l),
            _layer_spec((1, BRANCH_W), l),
        ],
        out_specs=[pl.BlockSpec((path.seq, BRANCH_W), lambda i: (i, 0))] * 2,
        out_shape=[out, out],
        scratch_shapes=[pltpu.VMEM((path.seq + 2 * CONV_PAD + V7X_SUBLANES, BRANCH_W), F32),
                        pltpu.VMEM((V7X_SUBLANES - 1, path.seq + 2 * CONV_PAD, BRANCH_W), F32)],
        compiler_params=_cparams(1),
        name=f"local_mix_{path.seq}",
    )(z, p["gmlp_ln_g"], p["gmlp_wcat"], p["gmlp_bsx"], p["conv_w"], p["conv_b"],
      p["conv_ln_g"], p["conv_ln_b"])


def _rwkv_quantities(zc, gd, w0_ref, w2_ref, a0_ref, a2_ref, g2_ref, kkp_ref, ka_ref, rk_ref, ones):
    W = BRANCH_W
    r = zc[:, 0:W]
    k = zc[:, W:2 * W]
    v = zc[:, 2 * W:3 * W]
    wd = zc[:, 3 * W:3 * W + N_DIR * DECAY_LORA]
    ad = zc[:, 3 * W + N_DIR * DECAY_LORA:4 * W]
    w_raw = _dot_split(jnp.tanh(wd), w2_ref[0], w2_ref[1]) + w0_ref[...]
    decay = jnp.exp(-math.exp(-0.5) * jax.nn.sigmoid(w_raw))
    a = jax.nn.sigmoid(_dot_split(ad, a2_ref[0], a2_ref[1]) + a0_ref[...])
    g = _dot_split(jax.nn.sigmoid(gd), g2_ref[0], g2_ref[1])
    kkr = k * kkp_ref[...]
    kk = kkr / jnp.maximum(jnp.sqrt(_head_sum(kkr * kkr, ones)), 1e-12)
    ka = ka_ref[...]
    w, kd, b = [], [], []
    for d in range(N_DIR):
        a_d = a[:, d * W:(d + 1) * W]
        w.append(decay[:, d * W:(d + 1) * W])
        kd.append(k * (1.0 + (a_d - 1.0) * ka))
        b.append(kk * a_d)
    bonus = _head_sum(r * (kd[0] + kd[1]) * rk_ref[...], ones) * v
    return g, bonus, r, kk, v, w, kd, b


def _prep_ctx_kernel(zc_ref, zg_ref, w0_ref, w2_ref, a0_ref, a2_ref, g2_ref, kkp_ref, ka_ref, rk_ref,
                     ones_ref, g_o, bonus_o, r_o, kk_o, v_o, w_o, kd_o, b_o, slab):
    nb, tt = CTX.nb, CTX.tt
    rows = nb * tt
    zc = zc_ref[...].reshape(rows, 4 * BRANCH_W)
    gd = zg_ref[...].reshape(rows, GATE_LORA)
    g, bonus, r, kk, v, w, kd, b = _rwkv_quantities(
        zc, gd, w0_ref, w2_ref, a0_ref, a2_ref, g2_ref, kkp_ref, ka_ref, rk_ref, ones_ref[...])
    g_o[...] = g.reshape(nb, tt, BRANCH_W)
    bonus_o[...] = bonus.reshape(nb, tt, BRANCH_W)
    n_slab = BRANCH_W // V7X_LANES
    lanes_per_head = V7X_LANES // RWKV_HEADS

    def to_chains(q, put):
        for s in range(n_slab):
            slab[s] = q[:, s * V7X_LANES:(s + 1) * V7X_LANES]
        by_t = jnp.concatenate(
            [jnp.concatenate([slab[s, pl.ds(t, nb, stride=tt), :] for s in range(n_slab)], axis=1)
             for t in range(tt)], axis=0)
        tr = by_t.T
        lane = lax.broadcasted_iota(jnp.int32, (HEAD_DIM, V7X_LANES), 1)
        low_half = lane < V7X_LANES // 2
        even_quarter = (lane // nb) % 2 == 0
        roll = lambda a, s: pltpu.roll(a, s, 1)
        for c in range(tt // RWKV_HEADS):
            x0, x1, x2, x3 = (tr[h * HEAD_DIM:(h + 1) * HEAD_DIM, c * V7X_LANES:(c + 1) * V7X_LANES]
                              for h in range(RWKV_HEADS))
            y0 = jnp.where(low_half, x0, roll(x2, 2 * nb))
            y2 = jnp.where(low_half, roll(x0, 2 * nb), x2)
            y1 = jnp.where(low_half, x1, roll(x3, 2 * nb))
            y3 = jnp.where(low_half, roll(x1, 2 * nb), x3)
            put(4 * c + 0, jnp.where(even_quarter, y0, roll(y1, nb)))
            put(4 * c + 1, jnp.where(even_quarter, roll(y0, 3 * nb), y1))
            put(4 * c + 2, jnp.where(even_quarter, y2, roll(y3, nb)))
            put(4 * c + 3, jnp.where(even_quarter, roll(y2, 3 * nb), y3))
    assert lanes_per_head == nb and RWKV_HEADS == 4

    def put_into(ref, *lead):
        def put(t, tile):
            ref[(*lead, t)] = tile
        return put

    to_chains(r, put_into(r_o))
    to_chains(kk, put_into(kk_o))
    to_chains(v, put_into(v_o))
    for d in range(N_DIR):
        to_chains(w[d], put_into(w_o, d))
        to_chains(kd[d], put_into(kd_o, d))
        to_chains(b[d], put_into(b_o, d))


def _prep_lat_kernel(zc_ref, zg_ref, w0_ref, w2_ref, a0_ref, a2_ref, g2_ref, kkp_ref, ka_ref, rk_ref,
                     ones_ref, g_o, bonus_o, v_o, kq_o, gk, gv):
    nb, tt = LAT.nb, LAT.tt
    rows = nb * tt
    q0 = pl.program_id(1) * LAT_Q_PER_STEP
    chains = nb * RWKV_HEADS

    @pl.when(q0 == 0)
    def _():
        zc = zc_ref[...].reshape(rows, 4 * BRANCH_W)
        gd = zg_ref[...].reshape(rows, GATE_LORA)
        g, bonus, r, kk, v, w, kd, b = _rwkv_quantities(
            zc, gd, w0_ref, w2_ref, a0_ref, a2_ref, g2_ref, kkp_ref, ka_ref, rk_ref, ones_ref[...])
        g_o[...] = g.reshape(nb, tt, BRANCH_W)
        bonus_o[...] = bonus.reshape(nb, tt, BRANCH_W)
        k_indexed = {LAT_Q_R: r, LAT_Q_KK: kk, LAT_Q_W: w[0], LAT_Q_W + 1: w[1],
                     LAT_Q_KD: kd[0], LAT_Q_KD + 1: kd[1], LAT_Q_B: b[0], LAT_Q_B + 1: b[1]}
        for bi in range(nb):
            for q_idx, q in k_indexed.items():
                gk[q_idx, bi * BRANCH_W:(bi + 1) * BRANCH_W, :] = q[bi * tt:(bi + 1) * tt, :].T
            gv[bi * BRANCH_W:(bi + 1) * BRANCH_W, :] = v[bi * tt:(bi + 1) * tt, :].T
        for vp in range(V7X_SUBLANES):
            a = jnp.concatenate(
                [gv[pl.ds(vp + V7X_SUBLANES * vl, chains, stride=HEAD_DIM), :] for vl in range(LAT_VL)], axis=0)
            v_o[pl.ds(vp, tt, stride=V7X_SUBLANES), :] = a.T

    def body(k, carry):
        n8 = V7X_SUBLANES
        for dq in range(LAT_Q_PER_STEP):
            a = gk[q0 + dq, pl.ds(k, chains, stride=HEAD_DIM), :]
            kq_o[dq, k // n8, pl.ds(k % n8, tt, stride=n8), :] = jnp.concatenate([a] * LAT_VL, axis=0).T
        return carry
    lax.fori_loop(0, HEAD_DIM, body, 0, unroll=4)


def _prep_param_specs(l):
    return [
        _layer_spec((1, N_DIR * BRANCH_W), l),
        _layer_spec((2, N_DIR * DECAY_LORA, N_DIR * BRANCH_W), l),
        _layer_spec((1, N_DIR * BRANCH_W), l),
        _layer_spec((2, N_DIR * ICL_LORA, N_DIR * BRANCH_W), l),
        _layer_spec((2, GATE_LORA, BRANCH_W), l),
        _layer_spec((1, BRANCH_W), l),
        _layer_spec((1, BRANCH_W), l),
        _layer_spec((1, BRANCH_W), l),
        _const_spec((BRANCH_W, BRANCH_W)),
    ]


def _prep_params(p, ones):
    return (p["rwkv_w0"], p["rwkv_w2blk"], p["rwkv_a0"], p["rwkv_a2blk"], p["rwkv_g2"],
            p["rwkv_kk"], p["rwkv_ka"], p["rwkv_rk"], ones)


def _prep_ctx(z, p, ones):
    nb, tt, seq = CTX.nb, CTX.tt, CTX.seq
    z3 = z.reshape(nb, seq, IN_COLS)
    tok = jax.ShapeDtypeStruct((nb, seq, BRANCH_W), F32)
    tok_spec = pl.BlockSpec((nb, tt, BRANCH_W), lambda i: (0, i, 0))
    ch = jax.ShapeDtypeStruct((seq, HEAD_DIM, V7X_LANES), F32)
    ch2 = jax.ShapeDtypeStruct((N_DIR, seq, HEAD_DIM, V7X_LANES), F32)
    ch_spec = pl.BlockSpec((tt, HEAD_DIM, V7X_LANES), lambda i: (i, 0, 0))
    ch2_spec = pl.BlockSpec((N_DIR, tt, HEAD_DIM, V7X_LANES), lambda i: (0, i, 0, 0))
    return pl.pallas_call(
        _prep_ctx_kernel,
        grid=(seq // tt,),
        in_specs=[
            pl.BlockSpec((nb, tt, 4 * BRANCH_W), lambda i: (0, i, Z_C // (4 * BRANCH_W))),
            pl.BlockSpec((nb, tt, GATE_LORA), lambda i: (0, i, Z_G // GATE_LORA)),
        ] + _prep_param_specs(p["layer"]),
        out_specs=[tok_spec, tok_spec, ch_spec, ch_spec, ch_spec, ch2_spec, ch2_spec, ch2_spec],
        out_shape=[tok, tok, ch, ch, ch, ch2, ch2, ch2],
        scratch_shapes=[pltpu.VMEM((BRANCH_W // V7X_LANES, nb * tt, V7X_LANES), F32)],
        compiler_params=_cparams(1),
        name="rwkv_prep_ctx",
    )(z3, z3, *_prep_params(p, ones))


def _prep_lat(z, p, ones):
    nb, tt, seq = LAT.nb, LAT.tt, LAT.seq
    z3 = z.reshape(nb, seq, IN_COLS)
    tok = jax.ShapeDtypeStruct((nb, seq, BRANCH_W), F32)
    tok_spec = pl.BlockSpec((nb, tt, BRANCH_W), lambda i, q: (0, i, 0))
    return pl.pallas_call(
        _prep_lat_kernel,
        grid=(seq // tt, LAT_NQ // LAT_Q_PER_STEP),
        in_specs=[
            pl.BlockSpec((nb, tt, 4 * BRANCH_W), lambda i, q: (0, i, Z_C // (4 * BRANCH_W))),
            pl.BlockSpec((nb, tt, GATE_LORA), lambda i, q: (0, i, Z_G // GATE_LORA)),
        ] + _prep_param_specs(p["layer"]),
        out_specs=[
            tok_spec, tok_spec,
            pl.BlockSpec((tt * V7X_SUBLANES, V7X_LANES), lambda i, q: (i, 0)),
            pl.BlockSpec((LAT_Q_PER_STEP, HEAD_DIM // V7X_SUBLANES, tt * V7X_SUBLANES, V7X_LANES),
                         lambda i, q: (q, 0, i, 0)),
        ],
        out_shape=[tok, tok,
                   jax.ShapeDtypeStruct((seq * V7X_SUBLANES, V7X_LANES), F32),
                   jax.ShapeDtypeStruct((LAT_NQ, HEAD_DIM // V7X_SUBLANES, seq * V7X_SUBLANES, V7X_LANES), F32)],
        scratch_shapes=[pltpu.VMEM((LAT_NQ, nb * BRANCH_W, V7X_LANES), F32),
                        pltpu.VMEM((nb * BRANCH_W, V7X_LANES), F32)],
        compiler_params=_cparams(2),
        name="rwkv_prep_lat",
    )(z3, z3, *_prep_params(p, ones))


SCAN_ACCS = 4


def _strided_sum(terms):
    acc = [None] * SCAN_ACCS
    for j, x in enumerate(terms):
        a = j % SCAN_ACCS
        acc[a] = x if acc[a] is None else acc[a] + x
    while len(acc) > 1:
        acc = [acc[j] + acc[j + 1] for j in range(0, len(acc), 2)]
    return acc[0]


def _scan_ctx_kernel(r_ref, kk_ref, v_ref, w_ref, kd_ref, b_ref, y_ref, sfin_ref, S, *, tb):
    d = pl.program_id(0)
    i = pl.program_id(1)
    n8 = V7X_SUBLANES
    n_v8 = HEAD_DIM // n8

    @pl.when(i == 0)
    def _():
        S[...] = jnp.zeros_like(S)

    def row8(ref, t, k):
        return jnp.broadcast_to(ref[t, pl.ds(k, 1), :], (n8, V7X_LANES))

    def step(s, carry):
        t = s + d * (tb - 1 - 2 * s)

        def pass1(k, sa):
            kkb = row8(kk_ref, t, k)
            return tuple(sa[vo] + S[k, vo * n8:(vo + 1) * n8, :] * kkb for vo in range(n_v8))

        zero = tuple(jnp.zeros((n8, V7X_LANES), F32) for _ in range(n_v8))
        sa = lax.fori_loop(0, HEAD_DIM, pass1, zero, unroll=16)
        nsa = tuple(-x for x in sa)
        vt = tuple(v_ref[t, vo * n8:(vo + 1) * n8, :] for vo in range(n_v8))

        def pass2(k, y):
            wb = row8(w_ref, t, k)
            bb = row8(b_ref, t, k)
            kb = row8(kd_ref, t, k)
            rb = row8(r_ref, t, k)
            out = []
            for vo in range(n_v8):
                sl = slice(vo * n8, (vo + 1) * n8)
                sn = S[k, sl, :] * wb + (nsa[vo] * bb + vt[vo] * kb)
                S[k, sl, :] = sn
                out.append(y[vo] + sn * rb)
            return tuple(out)

        y = lax.fori_loop(0, HEAD_DIM, pass2, zero, unroll=16)
        for vo in range(n_v8):
            y_ref[t, vo * n8:(vo + 1) * n8, :] = y[vo]
        return carry

    lax.fori_loop(0, tb, step, 0)

    @pl.when(i == pl.num_programs(1) - 1)
    def _():
        sfin_ref[...] = S[...]


def _scan_ctx(r_s, kk_s, v_s, w_s, kd_s, b_s):
    seq, tb = CTX.seq, SCAN_TB
    nt = seq // tb

    def tm(d, i):
        return i + d * (nt - 1 - 2 * i)
    shared = pl.BlockSpec((tb, HEAD_DIM, V7X_LANES), lambda d, i: (tm(d, i), 0, 0))
    per_dir = pl.BlockSpec((None, tb, HEAD_DIM, V7X_LANES), lambda d, i: (d, tm(d, i), 0, 0))
    state = pl.BlockSpec((None, HEAD_DIM, HEAD_DIM, V7X_LANES), lambda d, i: (d, 0, 0, 0))
    return pl.pallas_call(
        functools.partial(_scan_ctx_kernel, tb=tb),
        grid=(N_DIR, nt),
        in_specs=[shared, shared, shared, per_dir, per_dir, per_dir],
        out_specs=[per_dir, state],
        out_shape=[jax.ShapeDtypeStruct((N_DIR, seq, HEAD_DIM, V7X_LANES), F32),
                   jax.ShapeDtypeStruct((N_DIR, HEAD_DIM, HEAD_DIM, V7X_LANES), F32)],
        scratch_shapes=[pltpu.VMEM((HEAD_DIM, HEAD_DIM, V7X_LANES), F32)],
        compiler_params=_cparams(2),
        name="rwkv_scan_ctx",
    )(r_s, kk_s, v_s, w_s, kd_s, b_s)


def _lat_scan_dir(S, d, t, r_ref, kk_ref, v_ref, w_ref, kd_ref, b_ref, y_ref):
    n8 = V7X_SUBLANES

    def row8(ref, k):
        return jnp.broadcast_to(ref[k // n8, pl.ds(t * n8 + k % n8, 1), :], (n8, V7X_LANES))
    rows = pl.ds(pl.multiple_of(t * n8, n8), n8)
    nsa = -_strided_sum(S[d, k] * row8(kk_ref, k) for k in range(HEAD_DIM))
    vt = v_ref[rows, :]

    def update(k):
        sn = S[d, k] * row8(w_ref, k) + (nsa * row8(b_ref, k) + vt * row8(kd_ref, k))
        S[d, k] = sn
        return sn * row8(r_ref, k)
    y_ref[rows, :] = _strided_sum(update(k) for k in range(HEAD_DIM))


def _lat_scan_specs():
    tb = SCAN_TB
    nt = LAT.seq // tb
    n8 = V7X_SUBLANES

    def specs(d):
        tblk = (lambda i: i) if d == 0 else (lambda i: nt - 1 - i)
        plane = lambda q: pl.BlockSpec((None, HEAD_DIM // n8, tb * n8, V7X_LANES), lambda i: (q, 0, tblk(i), 0))
        rows = pl.BlockSpec((tb * n8, V7X_LANES), lambda i: (tblk(i), 0))
        return [plane(LAT_Q_R), plane(LAT_Q_KK), rows,
                plane(LAT_Q_W + d), plane(LAT_Q_KD + d), plane(LAT_Q_B + d)], rows
    return specs(0), specs(1)


def _scan_lat_kernel(*refs):
    fwd, bwd = refs[0:6], refs[6:12]
    s0_ref, yf_ref, yb_ref, S = refs[12:]

    @pl.when(pl.program_id(0) == 0)
    def _():
        S[...] = s0_ref[...]

    def step(s, carry):
        _lat_scan_dir(S, 0, s, *fwd, yf_ref)
        _lat_scan_dir(S, 1, SCAN_TB - 1 - s, *bwd, yb_ref)
        return carry
    lax.fori_loop(0, SCAN_TB, step, 0)


def _scan_lat(kq, v_s, s0):
    n8 = V7X_SUBLANES
    (in_f, y_f), (in_b, y_b) = _lat_scan_specs()
    y_shape = jax.ShapeDtypeStruct((LAT.seq * n8, V7X_LANES), F32)
    one = (kq, kq, v_s, kq, kq, kq)
    return pl.pallas_call(
        _scan_lat_kernel,
        grid=(LAT.seq // SCAN_TB,),
        in_specs=in_f + in_b + [_const_spec((N_DIR, HEAD_DIM, n8, V7X_LANES))],
        out_specs=[y_f, y_b],
        out_shape=[y_shape, y_shape],
        scratch_shapes=[pltpu.VMEM((N_DIR, HEAD_DIM, n8, V7X_LANES), F32)],
        compiler_params=_cparams(1),
        name="rwkv_scan_lat",
    )(*one, *one, s0)


def _softmax_pv(scores, vals, sink_col):
    m = sink_col
    for s in scores:
        m = jnp.maximum(m, jnp.max(s, axis=-1, keepdims=True))
    den = jnp.exp(sink_col - m)
    out = None
    for s, vx in zip(scores, vals):
        p = jnp.exp(s - m)
        den = den + jnp.sum(p, axis=-1, keepdims=True)
        o = _dot(p.astype(BF16), vx)
        out = o if out is None else out + o
    return out / den


def _sink_col(sink_ref, layer, kv, n_rows):
    row = lax.broadcasted_iota(jnp.int32, (Q_PER_KV * n_rows, 1), 0)
    col = jnp.full((Q_PER_KV * n_rows, 1), sink_ref[layer, kv * Q_PER_KV], F32)
    for g in range(1, Q_PER_KV):
        col = jnp.where(row >= g * n_rows, sink_ref[layer, kv * Q_PER_KV + g], col)
    return col


def _qk(q2, kh):
    return lax.dot_general(q2, kh, (((1,), (1,)), ((), ())), preferred_element_type=F32) * ATT_SCALE


CTX_ATTN_SEQS = 4


def _ctx_attn_kernel(sink_ref, z_ref, o_ref, *, layer):
    hd = HEAD_DIM
    for s in range(CTX_ATTN_SEQS):
        rows = slice(s * SEQ, (s + 1) * SEQ)
        q = z_ref[rows, 0:ATT_HEADS * hd]
        outs = []
        for kv in range(KV_HEADS):
            kh = z_ref[rows, ATT_HEADS * hd + kv * hd:ATT_HEADS * hd + (kv + 1) * hd].astype(BF16)
            voff = (ATT_HEADS + KV_HEADS) * hd + kv * hd
            vh = z_ref[rows, voff:voff + hd].astype(BF16)
            q2 = jnp.concatenate(
                [q[:, (kv * Q_PER_KV + g) * hd:(kv * Q_PER_KV + g + 1) * hd] for g in range(Q_PER_KV)],
                axis=0).astype(BF16)
            o2 = _softmax_pv([_qk(q2, kh)], [vh], _sink_col(sink_ref, layer, kv, SEQ))
            outs += [o2[g * SEQ:(g + 1) * SEQ] for g in range(Q_PER_KV)]
        o_ref[rows, :] = jnp.concatenate(outs, axis=1).astype(BF16)


def _rope(x, cos, sin_signed):
    lane = lax.broadcasted_iota(jnp.int32, x.shape, 1)
    first = (lane % (HEAD_DIM // 2)) < (HEAD_DIM // 4)
    partner = jnp.where(first, pltpu.roll(x, V7X_LANES - HEAD_DIM // 4, 1), pltpu.roll(x, HEAD_DIM // 4, 1))
    return x * cos + partner * sin_signed


def _lat_attn_kernel(sink_ref, z_ref, ck_ref, cv_ref, cos_ref, sin_ref, o_ref, q_s, k_s, *, layer):
    hd = HEAD_DIM
    cos = cos_ref[...]
    sin = sin_ref[...]
    for j in range(ATT_HEADS * hd // V7X_LANES):
        q_s[:, j * V7X_LANES:(j + 1) * V7X_LANES] = _rope(
            z_ref[:, j * V7X_LANES:(j + 1) * V7X_LANES], cos, sin).astype(BF16)
    k_s[...] = _rope(z_ref[:, ATT_HEADS * hd:(ATT_HEADS + KV_HEADS) * hd], cos, sin).astype(BF16)
    voff = (ATT_HEADS + KV_HEADS) * hd
    nb = DEC_SEQ // BLOCK
    for n in range(nb):
        lo = max(n - 1, 0) * BLOCK
        hi = min(n + 2, nb) * BLOCK
        i_abs = n * BLOCK + lax.broadcasted_iota(jnp.int32, (Q_PER_KV * BLOCK, hi - lo), 0) % BLOCK
        j_abs = lo + lax.broadcasted_iota(jnp.int32, (Q_PER_KV * BLOCK, hi - lo), 1)
        band = jnp.abs(i_abs - j_abs) <= WINDOW
        outs = []
        for kv in range(KV_HEADS):
            q2 = jnp.concatenate(
                [q_s[n * BLOCK:(n + 1) * BLOCK, (kv * Q_PER_KV + g) * hd:(kv * Q_PER_KV + g + 1) * hd]
                 for g in range(Q_PER_KV)], axis=0)
            kw = k_s[lo:hi, kv * hd:(kv + 1) * hd]
            vw = z_ref[lo:hi, voff + kv * hd:voff + (kv + 1) * hd].astype(BF16)
            kc = ck_ref[:, kv * hd:(kv + 1) * hd].astype(BF16)
            vc = cv_ref[:, kv * hd:(kv + 1) * hd].astype(BF16)
            s_lat = jnp.where(band, _qk(q2, kw), -1e30)
            o2 = _softmax_pv([s_lat, _qk(q2, kc)], [vw, vc], _sink_col(sink_ref, layer, kv, BLOCK))
            outs += [o2[g * BLOCK:(g + 1) * BLOCK] for g in range(Q_PER_KV)]
        o_ref[n * BLOCK:(n + 1) * BLOCK, :] = jnp.concatenate(outs, axis=1).astype(BF16)


def _ctx_attn(z, sink, layer):
    wd = (ATT_HEADS + 2 * KV_HEADS) * HEAD_DIM
    return pl.pallas_call(
        functools.partial(_ctx_attn_kernel, layer=layer),
        grid=(BATCH // CTX_ATTN_SEQS,),
        in_specs=[
            pl.BlockSpec(memory_space=pltpu.SMEM),
            pl.BlockSpec((CTX_ATTN_SEQS * SEQ, wd), lambda i: (i, Z_D // wd)),
        ],
        out_specs=pl.BlockSpec((CTX_ATTN_SEQS * SEQ, BRANCH_W), lambda i: (i, 0)),
        out_shape=jax.ShapeDtypeStruct((CTX.n_tok, BRANCH_W), BF16),
        compiler_params=_cparams(1),
        name="ctx_attn",
    )(sink, z)


def _lat_attn(z, sink, layer, ck, cv, cos_t, sin_t):
    wd = (ATT_HEADS + 2 * KV_HEADS) * HEAD_DIM
    kvw = KV_HEADS * HEAD_DIM
    return pl.pallas_call(
        functools.partial(_lat_attn_kernel, layer=layer),
        grid=(DEC_BATCH,),
        in_specs=[
            pl.BlockSpec(memory_space=pltpu.SMEM),
            pl.BlockSpec((DEC_SEQ, wd), lambda i: (i, Z_D // wd)),
            pl.BlockSpec((None, None, PAST_LEN, kvw), lambda i: (i, layer, 0, 0)),
            pl.BlockSpec((None, None, PAST_LEN, kvw), lambda i: (i, layer, 0, 0)),
            _const_spec((DEC_SEQ, V7X_LANES)),
            _const_spec((DEC_SEQ, V7X_LANES)),
        ],
        out_specs=pl.BlockSpec((DEC_SEQ, BRANCH_W), lambda i: (i, 0)),
        out_shape=jax.ShapeDtypeStruct((LAT.n_tok, BRANCH_W), BF16),
        scratch_shapes=[pltpu.VMEM((DEC_SEQ, ATT_HEADS * HEAD_DIM), BF16),
                        pltpu.VMEM((DEC_SEQ, kvw), BF16)],
        compiler_params=_cparams(1),
        name="lat_attn",
    )(sink, z, ck, cv, cos_t, sin_t)


def _scan_out_ctx(yf_ref, yb_ref, slab):
    nb, tt = CTX.nb, CTX.tt
    cols = []
    for t in range(tt):
        ys = yf_ref[t] + yb_ref[t]
        cols.append(jnp.concatenate([ys[:, h * nb:(h + 1) * nb] for h in range(RWKV_HEADS)], axis=0))
    by_t = jnp.concatenate(cols, axis=1).T
    n_slab = BRANCH_W // V7X_LANES
    for t in range(tt):
        for s in range(n_slab):
            slab[s, pl.ds(t, nb, stride=tt), :] = by_t[t * nb:(t + 1) * nb, s * V7X_LANES:(s + 1) * V7X_LANES]
    return jnp.concatenate([slab[s] for s in range(n_slab)], axis=1)


def _scan_out_lat(yf_ref, yb_ref, ysum, gy):
    nb, tt = LAT.nb, LAT.tt
    chains = nb * RWKV_HEADS
    ysum[...] = yf_ref[...] + yb_ref[...]
    for vp in range(V7X_SUBLANES):
        a = ysum[pl.ds(vp, tt, stride=V7X_SUBLANES), :].T
        for vl in range(LAT_VL):
            gy[pl.ds(vp + V7X_SUBLANES * vl, chains, stride=HEAD_DIM), :] = a[vl * chains:(vl + 1) * chains, :]
    return jnp.concatenate([gy[b * BRANCH_W:(b + 1) * BRANCH_W, :].T for b in range(nb)], axis=0)


def _mix_kernel(x_ref, mod_ref, g1_ref, oa_ref, ob_ref, od_ref, yf_ref, yb_ref, g_ref, bonus_ref,
                lnxg_ref, lnxb_ref, ones_ref, wg_ref, bg_ref, wb_ref, wo_ref, x1_ref, *scratch, path):
    nb, tt = path.nb, path.tt
    rows = nb * tt
    r0, nr = path.mod_row0, path.mod_rows

    def mod3(j):
        return mod_ref[r0:r0 + nr, j * D_MODEL:(j + 1) * D_MODEL][:, None, :]

    x3 = x_ref[...]
    h3 = _rms(x3, g1_ref[...]) * (1.0 + mod3(1)) + mod3(0)
    hb = h3.reshape(rows, D_MODEL).astype(BF16)
    y = (_scan_out_ctx if path is CTX else _scan_out_lat)(yf_ref, yb_ref, *scratch)
    ones = ones_ref[...]
    mu = _head_sum(y, ones) * (1.0 / HEAD_DIM)
    dl = y - mu
    var = _head_sum(dl * dl, ones) * (1.0 / HEAD_DIM)
    yn = dl * lax.rsqrt(var + GN_EPS) * lnxg_ref[...] + lnxb_ref[...]
    bonus = bonus_ref[...].reshape(rows, BRANCH_W)
    gate_c = g_ref[...].reshape(rows, BRANCH_W)
    oc = ((yn + bonus) * gate_c).astype(BF16)
    branches = (oa_ref[...].reshape(rows, BRANCH_W), ob_ref[...].reshape(rows, BRANCH_W), oc,
                od_ref[...].reshape(rows, BRANCH_W))
    mixed = None
    for n, br in enumerate(branches):
        cols = slice(n * D_MODEL, (n + 1) * D_MODEL)
        gate = jax.nn.sigmoid(_dot(hb, wg_ref[:, cols]) + bg_ref[:, cols])
        term = gate * _dot(br, wb_ref[n])
        mixed = term if mixed is None else mixed + term
    mix = _dot(mixed.astype(BF16), wo_ref[...])
    x1_ref[...] = x3 + mod3(2) * mix.reshape(nb, tt, D_MODEL)


def _mix(x, mod, p, oa, ob, od, yf, yb, g, bonus, ones, path):
    nb, tt, seq = path.nb, path.tt, path.seq
    l = p["layer"]
    tokw = pl.BlockSpec((nb, tt, BRANCH_W), lambda i: (0, i, 0))
    tokd = pl.BlockSpec((nb, tt, D_MODEL), lambda i: (0, i, 0))
    n_slab = BRANCH_W // V7X_LANES
    if path is CTX:
        y_specs = [pl.BlockSpec((None, tt, HEAD_DIM, V7X_LANES), lambda i, d=d: (d, i, 0, 0)) for d in range(N_DIR)]
        scratch = [pltpu.VMEM((n_slab, nb * tt, V7X_LANES), F32)]
    else:
        y_specs = [pl.BlockSpec((tt * V7X_SUBLANES, V7X_LANES), lambda i: (i, 0))] * N_DIR
        scratch = [pltpu.VMEM((tt * V7X_SUBLANES, V7X_LANES), F32),
                   pltpu.VMEM((nb * BRANCH_W, V7X_LANES), F32)]
    as3 = lambda a: a.reshape(nb, seq, a.shape[-1])
    out = pl.pallas_call(
        functools.partial(_mix_kernel, path=path),
        grid=(seq // tt,),
        in_specs=[
            tokd,
            _layer_spec((MOD_ROWS, N_MOD * D_MODEL), l),
            _layer_spec((1, D_MODEL), l),
            tokw, tokw, tokw, *y_specs, tokw, tokw,
            _layer_spec((1, BRANCH_W), l),
            _layer_spec((1, BRANCH_W), l),
            _const_spec((BRANCH_W, BRANCH_W)),
            _layer_spec((D_MODEL, N_BRANCH * D_MODEL), l),
            _layer_spec((1, N_BRANCH * D_MODEL), l),
            _layer_spec((N_BRANCH, BRANCH_W, D_MODEL), l),
            _layer_spec((D_MODEL, D_MODEL), l),
        ],
        out_specs=tokd,
        out_shape=jax.ShapeDtypeStruct((nb, seq, D_MODEL), F32),
        scratch_shapes=scratch,
        compiler_params=_cparams(1),
        name=f"branch_mix_{seq}",
    )(as3(x), mod, p["norm1_g"], as3(oa), as3(ob), as3(od), yf, yb, g, bonus, p["rwkv_lnx_g"], p["rwkv_lnx_b"],
      ones, p["w_gate"], p["b_gate"], p["w_branch"], p["w_out"])
    return out.reshape(path.n_tok, D_MODEL)


ROUTER_LANES = V7X_LANES


def _route(logits):
    lane = lax.broadcasted_iota(jnp.int32, logits.shape, 1).astype(F32)
    ninf = -jnp.inf
    big = float(ROUTER_LANES)
    gmask = lane < N_GROUPS
    gl = jnp.where(gmask, logits, ninf)
    gmax = jnp.max(gl, axis=-1, keepdims=True)
    gidx = jnp.min(jnp.where(gl == gmax, lane, big), axis=-1, keepdims=True)
    g_w = 1.0 / jnp.sum(jnp.where(gmask, jnp.exp(gl - gmax), 0.0), axis=-1, keepdims=True)
    egroup = jnp.floor((lane - N_GROUPS) * (1.0 / EXPERTS_PER_GROUP))
    emask = (lane >= N_GROUPS) & (lane < N_GROUPS + N_EXPERTS) & (egroup == gidx)
    el = jnp.where(emask, logits, ninf)
    e1 = jnp.max(el, axis=-1, keepdims=True)
    i1 = jnp.min(jnp.where(emask & (el == e1), lane, big), axis=-1, keepdims=True)
    el2 = jnp.where(lane == i1, ninf, el)
    e2 = jnp.max(el2, axis=-1, keepdims=True)
    i2 = jnp.min(jnp.where(emask & (lane != i1) & (el2 == e2), lane, big), axis=-1, keepdims=True)
    t = jnp.exp(e2 - e1)
    den = 1.0 + t
    return jnp.where(lane == i1, g_w * (1.0 / den), 0.0) + jnp.where(lane == i2, g_w * (t / den), 0.0)


def _moe_input(x1, mod_ref, row, g2_ref, wrh_ref, wrl_ref, br_ref):
    h2 = _rms(x1, g2_ref[...]) * (1.0 + _mod_slice(mod_ref, row, 4)) + _mod_slice(mod_ref, row, 3)
    return h2.astype(BF16), _route(_dot_split(h2, wrh_ref[...], wrl_ref[...]) + br_ref[...])


def _expert(hb, weg, weu, wed, c):
    hg = _dot(hb, weg)
    hu = _dot(hb, weu)
    return _dot((hg * jax.nn.sigmoid(hg) * hu * c).astype(BF16), wed)


def _moe_finish(x1, acc, mod_ref, row, fg_ref, out_refs, final):
    x2 = x1 + _mod_slice(mod_ref, row, 5) * acc
    out_refs[0][...] = x2
    if final:
        out_refs[1][...] = _rms(x2, fg_ref[...])


def _moe_kernel(x1_ref, mod_ref, g2_ref, wrh_ref, wrl_ref, br_ref, weg_ref, weu_ref, wed_ref, fg_ref,
                *out_refs, path, final):
    row = _mod_row(pl.program_id(0), path)
    x1 = x1_ref[...]
    hb, comb = _moe_input(x1, mod_ref, row, g2_ref, wrh_ref, wrl_ref, br_ref)
    acc = None
    for e in range(N_EXPERTS):
        lane = N_GROUPS + e
        term = _expert(hb, weg_ref[e], weu_ref[e], wed_ref[e], comb[:, lane:lane + 1])
        acc = term if acc is None else acc + term
    _moe_finish(x1, acc, mod_ref, row, fg_ref, out_refs, final)


def _moe_in_specs(tm, l):
    single = pl.Buffered(1)
    up = _layer_spec((N_EXPERTS, D_MODEL, EXPERT_FF), l, pipeline_mode=single)
    return [
        pl.BlockSpec((tm, D_MODEL), lambda i: (i, 0)),
        _layer_spec((MOD_ROWS, N_MOD * D_MODEL), l),
        _layer_spec((1, D_MODEL), l),
        _layer_spec((D_MODEL, ROUTER_LANES), l),
        _layer_spec((D_MODEL, ROUTER_LANES), l),
        _layer_spec((1, ROUTER_LANES), l),
        up, up,
        _layer_spec((N_EXPERTS, EXPERT_FF, D_MODEL), l, pipeline_mode=single),
        _const_spec((1, D_MODEL)),
    ]


def _moe_args(x1, mod, p, final_g):
    return (x1, mod, p["norm2_g"], *p["w_router"], p["b_router"], p["w_e_gate"], p["w_e_up"], p["w_e_down"], final_g)


def _moe(x1, mod, p, final_g, final, path):
    tm = TOK_TILE
    tokd = pl.BlockSpec((tm, D_MODEL), lambda i: (i, 0))
    out = jax.ShapeDtypeStruct((path.n_tok, D_MODEL), F32)
    return pl.pallas_call(
        functools.partial(_moe_kernel, path=path, final=final),
        grid=(path.n_tok // tm,),
        in_specs=_moe_in_specs(tm, p["layer"]),
        out_specs=[tokd, tokd] if final else [tokd],
        out_shape=[out, out] if final else [out],
        compiler_params=_cparams(1),
        name=f"moe_{path.seq}" + ("_final" if final else ""),
    )(*_moe_args(x1, mod, p, final_g))


def _block_diag2(w):
    z = jnp.zeros_like(w[:, 0])
    return jnp.concatenate([jnp.concatenate([w[:, 0], z], axis=2), jnp.concatenate([z, w[:, 1]], axis=2)], axis=1)


def _all_layer_params(a):
    row = lambda v: v.reshape(DEPTH, 1, -1)
    hi_lo = lambda w: jnp.stack(_split(w), axis=1)
    w_in = a["w_in"]
    w_router = jnp.zeros((DEPTH, D_MODEL, ROUTER_LANES), F32)
    w_router = w_router.at[:, :, 0:N_GROUPS].set(a["w_rg"]).at[:, :, N_GROUPS:N_GROUPS + N_EXPERTS].set(a["w_re"])
    b_router = jnp.zeros((DEPTH, 1, ROUTER_LANES), F32)
    b_router = b_router.at[:, 0, 0:N_GROUPS].set(a["b_rg"]).at[:, 0, N_GROUPS:N_GROUPS + N_EXPERTS].set(a["b_re"])
    return {
        "norm1_g": row(a["norm1_g"]), "norm2_g": row(a["norm2_g"]),
        "w_in": jnp.concatenate([w_in[:, :, 0:2048], w_in[:, :, 2176:2688], w_in[:, :, 2048:2176]], axis=2).astype(BF16),
        "gmlp_ln_g": row(a["gmlp_ln_g"]),
        "gmlp_wcat": a["gmlp_ws"].transpose(0, 2, 1, 3).reshape(DEPTH, CHUNK, GMLP_GROUPS * CHUNK).astype(BF16),
        "gmlp_bsx": jnp.repeat(a["gmlp_bs"].transpose(0, 2, 1), BRANCH_W // GMLP_GROUPS, axis=2),
        "conv_w": a["conv_w"], "conv_b": row(a["conv_b"]),
        "conv_ln_g": row(a["conv_ln_g"]), "conv_ln_b": row(a["conv_ln_b"]),
        "rwkv_w0": row(a["rwkv_w0"]), "rwkv_w2blk": hi_lo(_block_diag2(a["rwkv_w2"])),
        "rwkv_a0": row(a["rwkv_a0"]), "rwkv_a2blk": hi_lo(_block_diag2(a["rwkv_a2"])),
        "rwkv_g2": hi_lo(a["rwkv_g2"]), "rwkv_kk": row(a["rwkv_kk"]), "rwkv_ka": row(a["rwkv_ka"]),
        "rwkv_rk": row(a["rwkv_rk"]),
        "rwkv_lnx_g": row(a["rwkv_lnx_g"]), "rwkv_lnx_b": row(a["rwkv_lnx_b"]),
        "attn_sink": a["attn_sink"],
        "w_gate": a["w_gate"].astype(BF16), "b_gate": row(a["b_gate"]),
        "w_branch": a["w_branch"].astype(BF16), "w_out": a["w_out"].astype(BF16),
        "w_router": _split(w_router), "b_router": b_router,
        "w_e_gate": a["w_e_gate"].astype(BF16), "w_e_up": a["w_e_up"].astype(BF16),
        "w_e_down": a["w_e_down"].astype(BF16),
    }


def _rope_tables():
    half = HEAD_DIM // 4
    inv = ROPE_BASE ** (-jnp.arange(half, dtype=F32) / half)
    t = jnp.arange(DEC_SEQ)
    row = (t // GRID_W).astype(F32)[:, None] * inv[None, :]
    col = (t % GRID_W).astype(F32)[:, None] * inv[None, :]
    cos_h = jnp.concatenate([jnp.cos(row), jnp.cos(row), jnp.cos(col), jnp.cos(col)], axis=1)
    sin_h = jnp.concatenate([-jnp.sin(row), jnp.sin(row), -jnp.sin(col), jnp.sin(col)], axis=1)
    reps = V7X_LANES // HEAD_DIM
    return jnp.tile(cos_h, (1, reps)), jnp.tile(sin_h, (1, reps))


def _mixers(x, mod, p, path, attn):
    z, *kv = _inproj(x, mod, p, path)
    oa, ob = _local_mix(z, p, path)
    return z, kv, oa, ob, attn(z)


def kernel(x_prompt, x_sample, cache_k, cache_v, state_rwkv, c, c_ctx, norm1_g, norm2_g, final_norm_g, w_mod, b_mod, w_in, gmlp_ln_g, gmlp_ws, gmlp_bs, conv_w, conv_b, conv_ln_g, conv_ln_b, rwkv_w0, rwkv_w2, rwkv_a0, rwkv_a2, rwkv_g2, rwkv_kk, rwkv_ka, rwkv_rk, rwkv_lnx_g, rwkv_lnx_b, attn_sink, w_gate, b_gate, w_branch, w_out, w_rg, b_rg, w_re, b_re, w_e_gate, w_e_up, w_e_down):
    arrays = dict(norm1_g=norm1_g, norm2_g=norm2_g, w_in=w_in, gmlp_ln_g=gmlp_ln_g, gmlp_ws=gmlp_ws,
                  gmlp_bs=gmlp_bs, conv_w=conv_w, conv_b=conv_b, conv_ln_g=conv_ln_g, conv_ln_b=conv_ln_b,
                  rwkv_w0=rwkv_w0, rwkv_w2=rwkv_w2, rwkv_a0=rwkv_a0, rwkv_a2=rwkv_a2, rwkv_g2=rwkv_g2,
                  rwkv_kk=rwkv_kk, rwkv_ka=rwkv_ka, rwkv_rk=rwkv_rk, rwkv_lnx_g=rwkv_lnx_g,
                  rwkv_lnx_b=rwkv_lnx_b, attn_sink=attn_sink, w_gate=w_gate, b_gate=b_gate,
                  w_branch=w_branch, w_out=w_out, w_rg=w_rg, b_rg=b_rg, w_re=w_re, b_re=b_re,
                  w_e_gate=w_e_gate, w_e_up=w_e_up, w_e_down=w_e_down)
    xc = x_prompt.reshape(CTX.n_tok, D_MODEL)
    xl = x_sample.reshape(LAT.n_tok, D_MODEL)
    cvec = jnp.concatenate([c_ctx[None], c, jnp.zeros((MOD_ROWS - 1 - DEC_BATCH, D_MODEL), F32)], axis=0)
    mod_all = _modulation(cvec, w_mod, b_mod)
    ones = _head_ones()
    cos_t, sin_t = _rope_tables()
    final_g = final_norm_g.reshape(1, D_MODEL)
    kvw = KV_HEADS * HEAD_DIM
    ks_out, vs_out, ss_out = [], [], []
    yc = yl = None
    params = _all_layer_params(arrays)
    ck = cache_k.reshape(DEC_BATCH, DEPTH, PAST_LEN, kvw)
    cv = cache_v.reshape(DEC_BATCH, DEPTH, PAST_LEN, kvw)
    mod = mod_all
    for l in range(DEPTH):
        p = dict(params, layer=l)
        final = l == DEPTH - 1
        zc, kv_c, oa, ob, od = _mixers(xc, mod, p, CTX, lambda z: _ctx_attn(z, p["attn_sink"], l))
        g, bonus, r_s, kk_s, v_s, w_s, kd_s, b_s = _prep_ctx(zc, p, ones)
        y_c, s_fin = _scan_ctx(r_s, kk_s, v_s, w_s, kd_s, b_s)
        x1c = _mix(xc, mod, p, oa, ob, od, y_c, y_c, g, bonus, ones, CTX)
        zl, _, oa, ob, od = _mixers(xl, mod, p, LAT, lambda z: _lat_attn(z, p["attn_sink"], l, ck, cv, cos_t, sin_t))
        g, bonus, v_l, kq = _prep_lat(zl, p, ones)
        s0 = state_rwkv[:, l].reshape(DEC_BATCH, N_DIR, RWKV_HEADS, LAT_VL, V7X_SUBLANES, HEAD_DIM)
        s0 = s0.transpose(1, 5, 4, 3, 0, 2).reshape(N_DIR, HEAD_DIM, V7X_SUBLANES, V7X_LANES)
        yf, yb = _scan_lat(kq, v_l, s0)
        outs_c = _moe(x1c, mod, p, final_g, final, CTX)
        x1l = _mix(xl, mod, p, oa, ob, od, yf, yb, g, bonus, ones, LAT)
        outs_l = _moe(x1l, mod, p, final_g, final, LAT)
        xc, xl = outs_c[0], outs_l[0]
        if final:
            yc, yl = outs_c[1], outs_l[1]
        ks_out.append(kv_c[0].reshape(BATCH, SEQ, KV_HEADS, HEAD_DIM))
        vs_out.append(kv_c[1].reshape(BATCH, SEQ, KV_HEADS, HEAD_DIM))
        ss_out.append(s_fin.reshape(N_DIR, HEAD_DIM, HEAD_DIM, RWKV_HEADS, BATCH).transpose(4, 0, 3, 2, 1))
    y_prompt = yc.reshape(BATCH, SEQ, D_MODEL)
    y_sample = yl.reshape(DEC_BATCH, DEC_SEQ, D_MODEL)
    return (y_prompt, y_sample, jnp.stack(ks_out, axis=1), jnp.stack(vs_out, axis=1), jnp.stack(ss_out, axis=1))
```

```python
import functools
import math
from typing import NamedTuple

import jax
import jax.numpy as jnp
from jax import lax
from jax.experimental import pallas as pl
from jax.experimental.pallas import tpu as pltpu

D_MODEL = 1024
BATCH = 32
SEQ = 256
DEPTH = 2
DEC_BATCH = 4
DEC_SEQ = 1024
PAST_LEN = 256
GRID_W = 64
HEAD_DIM = 64
BRANCH_W = 256
N_BRANCH = 4
CHUNK = 128
GMLP_GROUPS = 4
CONV_W = 31
RWKV_HEADS = 4
N_DIR = 2
DECAY_LORA = 64
ICL_LORA = 64
GATE_LORA = 128
ATT_HEADS = 4
KV_HEADS = 2
Q_PER_KV = ATT_HEADS // KV_HEADS
WINDOW = 128
BLOCK = 128
ROPE_BASE = 10000.0
ATT_SCALE = HEAD_DIM ** -0.5
N_GROUPS = 4
EXPERTS_PER_GROUP = 4
N_EXPERTS = N_GROUPS * EXPERTS_PER_GROUP
EXPERT_FF = 256
N_MOD = 6
RMS_EPS = 1e-6
LN_EPS = 1e-5
GN_EPS = 64e-5
IN_COLS = 2688

V7X_LANES = 128
V7X_SUBLANES = 8
V7X_VMEM_LIMIT = 56 * 1024 * 1024

MOD_ROWS = 8
TOK_TILE = 512
SCAN_TB = 32
CONV_PAD = 16

Z_AB, Z_C, Z_D, Z_G = 0, 1024, 2048, 2560

LAT_Q_R, LAT_Q_KK, LAT_Q_W, LAT_Q_KD, LAT_Q_B = 0, 1, 2, 4, 6
LAT_NQ = 8
LAT_Q_PER_STEP = 4
LAT_VL = V7X_LANES // (DEC_BATCH * RWKV_HEADS)

F32 = jnp.float32
BF16 = jnp.bfloat16
HIGHEST = lax.Precision.HIGHEST


class _Path(NamedTuple):
    nb: int
    seq: int
    mod_row0: int
    mod_rows: int
    tt: int

    @property
    def n_tok(self):
        return self.nb * self.seq


CTX = _Path(BATCH, SEQ, 0, 1, TOK_TILE // BATCH)
LAT = _Path(DEC_BATCH, DEC_SEQ, 1, DEC_BATCH, TOK_TILE // DEC_BATCH)


def _cparams(n_axes, vmem=V7X_VMEM_LIMIT):
    return pltpu.CompilerParams(dimension_semantics=("arbitrary",) * n_axes, vmem_limit_bytes=vmem)


def _const_spec(shape):
    nd = len(shape)
    return pl.BlockSpec(shape, lambda *_: (0,) * nd)


def _layer_spec(shape, l, **kw):
    nd = len(shape)
    return pl.BlockSpec((None,) + tuple(shape), lambda *_: (l,) + (0,) * nd, **kw)


def _dot(a, b):
    return jnp.dot(a, b, preferred_element_type=F32)


def _dot_hi(a, b):
    return jnp.dot(a, b, preferred_element_type=F32, precision=HIGHEST)


def _split(x):
    hi = x.astype(BF16)
    return hi, (x - hi.astype(F32)).astype(BF16)


def _dot_split(a, b_hi, b_lo):
    a_hi, a_lo = _split(a)
    return _dot(a_hi, b_hi) + (_dot(a_lo, b_hi) + _dot(a_hi, b_lo))


def _head_sum(x, ones_b):
    hi, lo = _split(x)
    return _dot(hi, ones_b) + _dot(lo, ones_b)


def _head_ones():
    r = lax.broadcasted_iota(jnp.int32, (BRANCH_W, BRANCH_W), 0) // HEAD_DIM
    c = lax.broadcasted_iota(jnp.int32, (BRANCH_W, BRANCH_W), 1) // HEAD_DIM
    return (r == c).astype(BF16)


def _mod_row(tile, path):
    if path.mod_rows == 1:
        return path.mod_row0
    return path.mod_row0 + tile // (path.seq // TOK_TILE)


def _mod_slice(mod_ref, row, j):
    return mod_ref[pl.ds(row, 1), j * D_MODEL:(j + 1) * D_MODEL]


def _rms(x, g):
    return x * lax.rsqrt(jnp.mean(x * x, axis=-1, keepdims=True) + RMS_EPS) * g


def _layernorm(x, g, b=None, eps=LN_EPS):
    mu = jnp.mean(x, axis=-1, keepdims=True)
    d = x - mu
    var = jnp.mean(d * d, axis=-1, keepdims=True)
    y = d * lax.rsqrt(var + eps) * g
    return y if b is None else y + b


def _mod_kernel(c_ref, w_ref, b_ref, o_ref):
    c = c_ref[...]
    a = c * jax.nn.sigmoid(c)
    o_ref[0] = _dot_hi(a, w_ref[0]) + b_ref[0]


def _modulation(cvec, w_mod, b_mod):
    cols = 2 * D_MODEL
    return pl.pallas_call(
        _mod_kernel,
        grid=(DEPTH, N_MOD * D_MODEL // cols),
        in_specs=[
            pl.BlockSpec((MOD_ROWS, D_MODEL), lambda l, j: (0, 0)),
            pl.BlockSpec((1, D_MODEL, cols), lambda l, j: (l, 0, j)),
            pl.BlockSpec((1, 1, cols), lambda l, j: (l, 0, j)),
        ],
        out_specs=pl.BlockSpec((1, MOD_ROWS, cols), lambda l, j: (l, 0, j)),
        out_shape=jax.ShapeDtypeStruct((DEPTH, MOD_ROWS, N_MOD * D_MODEL), F32),
        compiler_params=_cparams(2),
        name="modulation",
    )(cvec, w_mod, b_mod.reshape(DEPTH, 1, N_MOD * D_MODEL))


def _inproj_kernel(x_ref, mod_ref, g_ref, w_ref, z_ref, *kv_refs, path):
    row = _mod_row(pl.program_id(0), path)
    h = _rms(x_ref[...], g_ref[...]) * (1.0 + _mod_slice(mod_ref, row, 1)) + _mod_slice(mod_ref, row, 0)
    z = _dot(h.astype(BF16), w_ref[...])
    z_ref[...] = z
    kvw = KV_HEADS * HEAD_DIM
    for j, ref in enumerate(kv_refs):
        off = Z_D + ATT_HEADS * HEAD_DIM + j * kvw
        ref[...] = z[:, off:off + kvw]


def _inproj(x, mod, p, path):
    tm = TOK_TILE
    l = p["layer"]
    kvw = KV_HEADS * HEAD_DIM
    n_kv = 2 if path is CTX else 0
    return pl.pallas_call(
        functools.partial(_inproj_kernel, path=path),
        grid=(path.n_tok // tm,),
        in_specs=[
            pl.BlockSpec((tm, D_MODEL), lambda i: (i, 0)),
            _layer_spec((MOD_ROWS, N_MOD * D_MODEL), l),
            _layer_spec((1, D_MODEL), l),
            _layer_spec((D_MODEL, IN_COLS), l),
        ],
        out_specs=[pl.BlockSpec((tm, IN_COLS), lambda i: (i, 0))] + [pl.BlockSpec((tm, kvw), lambda i: (i, 0))] * n_kv,
        out_shape=[jax.ShapeDtypeStruct((path.n_tok, IN_COLS), F32)]
        + [jax.ShapeDtypeStruct((path.n_tok, kvw), F32)] * n_kv,
        compiler_params=_cparams(1),
        name=f"inproj_{path.seq}",
    )(x, mod, p["norm1_g"], p["w_in"])


def _local_mix_kernel(z_ref, lng_ref, wcat_ref, bsx_ref, cw_ref, cb_ref, clg_ref, clb_ref,
                      oa_ref, ob_ref, ypad, yshift, *, seq_len):
    lane_group = lax.broadcasted_iota(jnp.int32, (CHUNK, BRANCH_W), 1) // (BRANCH_W // GMLP_GROUPS)
    for c in range(seq_len // CHUNK):
        rows = pl.ds(c * CHUNK, CHUNK)
        u = jax.nn.gelu(z_ref[rows, 0:BRANCH_W], approximate=True)
        v = jax.nn.gelu(z_ref[rows, BRANCH_W:2 * BRANCH_W], approximate=True)
        vn = _layernorm(v, lng_ref[...])
        vblk = jnp.concatenate(
            [jnp.where(lane_group == g, vn, 0.0) for g in range(GMLP_GROUPS)], axis=0).astype(BF16)
        mixed = _dot(wcat_ref[...], vblk) + bsx_ref[...]
        oa_ref[rows, :] = (u * mixed).astype(BF16)

    n8 = V7X_SUBLANES
    ypad[0:CONV_PAD, :] = jnp.zeros((CONV_PAD, BRANCH_W), F32)
    ypad[CONV_PAD + seq_len:, :] = jnp.zeros((CONV_PAD + n8, BRANCH_W), F32)
    ypad[CONV_PAD:CONV_PAD + seq_len, :] = (
        z_ref[:, 2 * BRANCH_W:3 * BRANCH_W] * jax.nn.sigmoid(z_ref[:, 3 * BRANCH_W:4 * BRANCH_W]))
    span = seq_len + 2 * CONV_PAD
    for r in range(1, n8):
        yshift[r - 1] = ypad[r:r + span, :]
    base = CONV_PAD - CONV_W // 2
    for c in range(seq_len // CHUNK):
        acc = jnp.zeros((CHUNK, BRANCH_W), F32)
        for j in range(CONV_W):
            off = c * CHUNK + base + j
            r = off % n8
            rows = pl.ds(off - r, CHUNK)
            acc = acc + cw_ref[j:j + 1, :] * (ypad[rows, :] if r == 0 else yshift[r - 1, rows, :])
        y = _layernorm(acc + cb_ref[...], clg_ref[...], clb_ref[...])
        ob_ref[pl.ds(c * CHUNK, CHUNK), :] = (y * jax.nn.sigmoid(y)).astype(BF16)


def _local_mix(z, p, path):
    l = p["layer"]
    out = jax.ShapeDtypeStruct((path.n_tok, BRANCH_W), BF16)
    return pl.pallas_call(
        functools.partial(_local_mix_kernel, seq_len=path.seq),
        grid=(path.nb,),
        in_specs=[
            pl.BlockSpec((path.seq, 4 * BRANCH_W), lambda i: (i, Z_AB // (4 * BRANCH_W))),
            _layer_spec((1, BRANCH_W), l),
            _layer_spec((CHUNK, GMLP_GROUPS * CHUNK), l),
            _layer_spec((CHUNK, BRANCH_W), l),
            _layer_spec((CONV_W, BRANCH_W), l),
            _layer_spec((1, BRANCH_W), l),
            _layer_spec((1, BRANCH_W), l),
            _layer_spec((1, BRANCH_W), l),
        ],
        out_specs=[pl.BlockSpec((path.seq, BRANCH_W), lambda i: (i, 0))] * 2,
        out_shape=[out, out],
        scratch_shapes=[pltpu.VMEM((path.seq + 2 * CONV_PAD + V7X_SUBLANES, BRANCH_W), F32),
                        pltpu.VMEM((V7X_SUBLANES - 1, path.seq + 2 * CONV_PAD, BRANCH_W), F32)],
        compiler_params=_cparams(1),
        name=f"local_mix_{path.seq}",
    )(z, p["gmlp_ln_g"], p["gmlp_wcat"], p["gmlp_bsx"], p["conv_w"], p["conv_b"],
      p["conv_ln_g"], p["conv_ln_b"])


def _rwkv_quantities(zc, gd, w0_ref, w2_ref, a0_ref, a2_ref, g2_ref, kkp_ref, ka_ref, rk_ref, ones):
    W = BRANCH_W
    r = zc[:, 0:W]
    k = zc[:, W:2 * W]
    v = zc[:, 2 * W:3 * W]
    wd = zc[:, 3 * W:3 * W + N_DIR * DECAY_LORA]
    ad = zc[:, 3 * W + N_DIR * DECAY_LORA:4 * W]
    w_raw = _dot_split(jnp.tanh(wd), w2_ref[0], w2_ref[1]) + w0_ref[...]
    decay = jnp.exp(-math.exp(-0.5) * jax.nn.sigmoid(w_raw))
    a = jax.nn.sigmoid(_dot_split(ad, a2_ref[0], a2_ref[1]) + a0_ref[...])
    g = _dot_split(jax.nn.sigmoid(gd), g2_ref[0], g2_ref[1])
    kkr = k * kkp_ref[...]
    kk = kkr / jnp.maximum(jnp.sqrt(_head_sum(kkr * kkr, ones)), 1e-12)
    ka = ka_ref[...]
    w, kd, b = [], [], []
    for d in range(N_DIR):
        a_d = a[:, d * W:(d + 1) * W]
        w.append(decay[:, d * W:(d + 1) * W])
        kd.append(k * (1.0 + (a_d - 1.0) * ka))
        b.append(kk * a_d)
    bonus = _head_sum(r * (kd[0] + kd[1]) * rk_ref[...], ones) * v
    return g, bonus, r, kk, v, w, kd, b


def _exchange_lane_groups(x):
    x0, x1, x2, x3 = x
    quarter = V7X_LANES // 4
    lane = lax.broadcasted_iota(jnp.int32, x0.shape, 1)
    low_half = lane < 2 * quarter
    even_quarter = (lane // quarter) % 2 == 0
    roll = lambda a, s: pltpu.roll(a, s, 1)
    y0 = jnp.where(low_half, x0, roll(x2, 2 * quarter))
    y2 = jnp.where(low_half, roll(x0, 2 * quarter), x2)
    y1 = jnp.where(low_half, x1, roll(x3, 2 * quarter))
    y3 = jnp.where(low_half, roll(x1, 2 * quarter), x3)
    return (jnp.where(even_quarter, y0, roll(y1, quarter)), jnp.where(even_quarter, roll(y0, 3 * quarter), y1),
            jnp.where(even_quarter, y2, roll(y3, quarter)), jnp.where(even_quarter, roll(y2, 3 * quarter), y3))


def _prep_ctx_kernel(zc_ref, zg_ref, w0_ref, w2_ref, a0_ref, a2_ref, g2_ref, kkp_ref, ka_ref, rk_ref,
                     ones_ref, g_o, bonus_o, r_o, kk_o, v_o, w_o, kd_o, b_o, slab):
    nb, tt = CTX.nb, CTX.tt
    rows = nb * tt
    zc = zc_ref[...].reshape(rows, 4 * BRANCH_W)
    gd = zg_ref[...].reshape(rows, GATE_LORA)
    g, bonus, r, kk, v, w, kd, b = _rwkv_quantities(
        zc, gd, w0_ref, w2_ref, a0_ref, a2_ref, g2_ref, kkp_ref, ka_ref, rk_ref, ones_ref[...])
    g_o[...] = g.reshape(nb, tt, BRANCH_W)
    bonus_o[...] = bonus.reshape(nb, tt, BRANCH_W)
    n_slab = BRANCH_W // V7X_LANES
    lanes_per_head = V7X_LANES // RWKV_HEADS

    def to_chains(q, put):
        for s in range(n_slab):
            slab[s] = q[:, s * V7X_LANES:(s + 1) * V7X_LANES]
        by_t = jnp.concatenate(
            [jnp.concatenate([slab[s, pl.ds(t, nb, stride=tt), :] for s in range(n_slab)], axis=1)
             for t in range(tt)], axis=0)
        tr = by_t.T
        for c in range(tt // RWKV_HEADS):
            tiles = _exchange_lane_groups(
                [tr[h * HEAD_DIM:(h + 1) * HEAD_DIM, c * V7X_LANES:(c + 1) * V7X_LANES] for h in range(RWKV_HEADS)])
            for j, tile in enumerate(tiles):
                put(RWKV_HEADS * c + j, tile)
    assert lanes_per_head == nb

    def put_into(ref, *lead):
        def put(t, tile):
            ref[(*lead, t)] = tile
        return put

    to_chains(r, put_into(r_o))
    to_chains(kk, put_into(kk_o))
    to_chains(v, put_into(v_o))
    for d in range(N_DIR):
        to_chains(w[d], put_into(w_o, d))
        to_chains(kd[d], put_into(kd_o, d))
        to_chains(b[d], put_into(b_o, d))


def _prep_lat_kernel(zc_ref, zg_ref, w0_ref, w2_ref, a0_ref, a2_ref, g2_ref, kkp_ref, ka_ref, rk_ref,
                     ones_ref, g_o, bonus_o, v_o, kq_o, gk, gv):
    nb, tt = LAT.nb, LAT.tt
    rows = nb * tt
    q0 = pl.program_id(1) * LAT_Q_PER_STEP
    chains = nb * RWKV_HEADS

    @pl.when(q0 == 0)
    def _():
        zc = zc_ref[...].reshape(rows, 4 * BRANCH_W)
        gd = zg_ref[...].reshape(rows, GATE_LORA)
        g, bonus, r, kk, v, w, kd, b = _rwkv_quantities(
            zc, gd, w0_ref, w2_ref, a0_ref, a2_ref, g2_ref, kkp_ref, ka_ref, rk_ref, ones_ref[...])
        g_o[...] = g.reshape(nb, tt, BRANCH_W)
        bonus_o[...] = bonus.reshape(nb, tt, BRANCH_W)
        k_indexed = {LAT_Q_R: r, LAT_Q_KK: kk, LAT_Q_W: w[0], LAT_Q_W + 1: w[1],
                     LAT_Q_KD: kd[0], LAT_Q_KD + 1: kd[1], LAT_Q_B: b[0], LAT_Q_B + 1: b[1]}
        for bi in range(nb):
            for q_idx, q in k_indexed.items():
                gk[q_idx, bi * BRANCH_W:(bi + 1) * BRANCH_W, :] = q[bi * tt:(bi + 1) * tt, :].T
            gv[bi * BRANCH_W:(bi + 1) * BRANCH_W, :] = v[bi * tt:(bi + 1) * tt, :].T
        for vp in range(V7X_SUBLANES):
            a = jnp.concatenate(
                [gv[pl.ds(vp + V7X_SUBLANES * vl, chains, stride=HEAD_DIM), :] for vl in range(LAT_VL)], axis=0)
            v_o[pl.ds(vp, tt, stride=V7X_SUBLANES), :] = a.T

    def body(k, carry):
        n8 = V7X_SUBLANES
        for dq in range(LAT_Q_PER_STEP):
            a = gk[q0 + dq, pl.ds(k, chains, stride=HEAD_DIM), :]
            kq_o[dq, k // n8, pl.ds(k % n8, tt, stride=n8), :] = jnp.concatenate([a] * LAT_VL, axis=0).T
        return carry
    lax.fori_loop(0, HEAD_DIM, body, 0, unroll=4)


def _prep_param_specs(l):
    return [
        _layer_spec((1, N_DIR * BRANCH_W), l),
        _layer_spec((2, N_DIR * DECAY_LORA, N_DIR * BRANCH_W), l),
        _layer_spec((1, N_DIR * BRANCH_W), l),
        _layer_spec((2, N_DIR * ICL_LORA, N_DIR * BRANCH_W), l),
        _layer_spec((2, GATE_LORA, BRANCH_W), l),
        _layer_spec((1, BRANCH_W), l),
        _layer_spec((1, BRANCH_W), l),
        _layer_spec((1, BRANCH_W), l),
        _const_spec((BRANCH_W, BRANCH_W)),
    ]


def _prep_params(p, ones):
    return (p["rwkv_w0"], p["rwkv_w2blk"], p["rwkv_a0"], p["rwkv_a2blk"], p["rwkv_g2"],
            p["rwkv_kk"], p["rwkv_ka"], p["rwkv_rk"], ones)


def _prep_ctx(z, p, ones):
    nb, tt, seq = CTX.nb, CTX.tt, CTX.seq
    z3 = z.reshape(nb, seq, IN_COLS)
    tok = jax.ShapeDtypeStruct((nb, seq, BRANCH_W), F32)
    tok_spec = pl.BlockSpec((nb, tt, BRANCH_W), lambda i: (0, i, 0))
    ch = jax.ShapeDtypeStruct((seq, HEAD_DIM, V7X_LANES), F32)
    ch2 = jax.ShapeDtypeStruct((N_DIR, seq, HEAD_DIM, V7X_LANES), F32)
    ch_spec = pl.BlockSpec((tt, HEAD_DIM, V7X_LANES), lambda i: (i, 0, 0))
    ch2_spec = pl.BlockSpec((N_DIR, tt, HEAD_DIM, V7X_LANES), lambda i: (0, i, 0, 0))
    return pl.pallas_call(
        _prep_ctx_kernel,
        grid=(seq // tt,),
        in_specs=[
            pl.BlockSpec((nb, tt, 4 * BRANCH_W), lambda i: (0, i, Z_C // (4 * BRANCH_W))),
            pl.BlockSpec((nb, tt, GATE_LORA), lambda i: (0, i, Z_G // GATE_LORA)),
        ] + _prep_param_specs(p["layer"]),
        out_specs=[tok_spec, tok_spec, ch_spec, ch_spec, ch_spec, ch2_spec, ch2_spec, ch2_spec],
        out_shape=[tok, tok, ch, ch, ch, ch2, ch2, ch2],
        scratch_shapes=[pltpu.VMEM((BRANCH_W // V7X_LANES, nb * tt, V7X_LANES), F32)],
        compiler_params=_cparams(1),
        name="rwkv_prep_ctx",
    )(z3, z3, *_prep_params(p, ones))


def _prep_lat(z, p, ones):
    nb, tt, seq = LAT.nb, LAT.tt, LAT.seq
    z3 = z.reshape(nb, seq, IN_COLS)
    tok = jax.ShapeDtypeStruct((nb, seq, BRANCH_W), F32)
    tok_spec = pl.BlockSpec((nb, tt, BRANCH_W), lambda i, q: (0, i, 0))
    return pl.pallas_call(
        _prep_lat_kernel,
        grid=(seq // tt, LAT_NQ // LAT_Q_PER_STEP),
        in_specs=[
            pl.BlockSpec((nb, tt, 4 * BRANCH_W), lambda i, q: (0, i, Z_C // (4 * BRANCH_W))),
            pl.BlockSpec((nb, tt, GATE_LORA), lambda i, q: (0, i, Z_G // GATE_LORA)),
        ] + _prep_param_specs(p["layer"]),
        out_specs=[
            tok_spec, tok_spec,
            pl.BlockSpec((tt * V7X_SUBLANES, V7X_LANES), lambda i, q: (i, 0)),
            pl.BlockSpec((LAT_Q_PER_STEP, HEAD_DIM // V7X_SUBLANES, tt * V7X_SUBLANES, V7X_LANES),
                         lambda i, q: (q, 0, i, 0)),
        ],
        out_shape=[tok, tok,
                   jax.ShapeDtypeStruct((seq * V7X_SUBLANES, V7X_LANES), F32),
                   jax.ShapeDtypeStruct((LAT_NQ, HEAD_DIM // V7X_SUBLANES, seq * V7X_SUBLANES, V7X_LANES), F32)],
        scratch_shapes=[pltpu.VMEM((LAT_NQ, nb * BRANCH_W, V7X_LANES), F32),
                        pltpu.VMEM((nb * BRANCH_W, V7X_LANES), F32)],
        compiler_params=_cparams(2),
        name="rwkv_prep_lat",
    )(z3, z3, *_prep_params(p, ones))


SCAN_ACCS = 4


def _strided_sum(terms):
    acc = [None] * SCAN_ACCS
    for j, x in enumerate(terms):
        a = j % SCAN_ACCS
        acc[a] = x if acc[a] is None else acc[a] + x
    while len(acc) > 1:
        acc = [acc[j] + acc[j + 1] for j in range(0, len(acc), 2)]
    return acc[0]


def _scan_ctx_kernel(r_ref, kk_ref, v_ref, w_ref, kd_ref, b_ref, y_ref, sfin_ref, S, *, tb):
    d = pl.program_id(0)
    i = pl.program_id(1)
    n8 = V7X_SUBLANES
    n_v8 = HEAD_DIM // n8

    @pl.when(i == 0)
    def _():
        S[...] = jnp.zeros_like(S)

    def row8(ref, t, k):
        return jnp.broadcast_to(ref[t, pl.ds(k, 1), :], (n8, V7X_LANES))

    def step(s, carry):
        t = s + d * (tb - 1 - 2 * s)

        def pass1(k, sa):
            kkb = row8(kk_ref, t, k)
            return tuple(sa[vo] + S[k, vo * n8:(vo + 1) * n8, :] * kkb for vo in range(n_v8))

        zero = tuple(jnp.zeros((n8, V7X_LANES), F32) for _ in range(n_v8))
        sa = lax.fori_loop(0, HEAD_DIM, pass1, zero, unroll=16)
        nsa = tuple(-x for x in sa)
        vt = tuple(v_ref[t, vo * n8:(vo + 1) * n8, :] for vo in range(n_v8))

        def pass2(k, y):
            wb = row8(w_ref, t, k)
            bb = row8(b_ref, t, k)
            kb = row8(kd_ref, t, k)
            rb = row8(r_ref, t, k)
            out = []
            for vo in range(n_v8):
                sl = slice(vo * n8, (vo + 1) * n8)
                sn = S[k, sl, :] * wb + (nsa[vo] * bb + vt[vo] * kb)
                S[k, sl, :] = sn
                out.append(y[vo] + sn * rb)
            return tuple(out)

        y = lax.fori_loop(0, HEAD_DIM, pass2, zero, unroll=16)
        for vo in range(n_v8):
            y_ref[t, vo * n8:(vo + 1) * n8, :] = y[vo]
        return carry

    lax.fori_loop(0, tb, step, 0)

    @pl.when(i == pl.num_programs(1) - 1)
    def _():
        sfin_ref[...] = S[...]


def _scan_ctx(r_s, kk_s, v_s, w_s, kd_s, b_s):
    seq, tb = CTX.seq, SCAN_TB
    nt = seq // tb

    def tm(d, i):
        return i + d * (nt - 1 - 2 * i)
    shared = pl.BlockSpec((tb, HEAD_DIM, V7X_LANES), lambda d, i: (tm(d, i), 0, 0))
    per_dir = pl.BlockSpec((None, tb, HEAD_DIM, V7X_LANES), lambda d, i: (d, tm(d, i), 0, 0))
    state = pl.BlockSpec((None, HEAD_DIM, HEAD_DIM, V7X_LANES), lambda d, i: (d, 0, 0, 0))
    return pl.pallas_call(
        functools.partial(_scan_ctx_kernel, tb=tb),
        grid=(N_DIR, nt),
        in_specs=[shared, shared, shared, per_dir, per_dir, per_dir],
        out_specs=[per_dir, state],
        out_shape=[jax.ShapeDtypeStruct((N_DIR, seq, HEAD_DIM, V7X_LANES), F32),
                   jax.ShapeDtypeStruct((N_DIR, HEAD_DIM, HEAD_DIM, V7X_LANES), F32)],
        scratch_shapes=[pltpu.VMEM((HEAD_DIM, HEAD_DIM, V7X_LANES), F32)],
        compiler_params=_cparams(2),
        name="rwkv_scan_ctx",
    )(r_s, kk_s, v_s, w_s, kd_s, b_s)


def _lat_scan_dir(S, d, t, r_ref, kk_ref, v_ref, w_ref, kd_ref, b_ref, y_ref):
    n8 = V7X_SUBLANES

    def row8(ref, k):
        return jnp.broadcast_to(ref[k // n8, pl.ds(t * n8 + k % n8, 1), :], (n8, V7X_LANES))
    rows = pl.ds(pl.multiple_of(t * n8, n8), n8)
    nsa = -_strided_sum(S[d, k] * row8(kk_ref, k) for k in range(HEAD_DIM))
    vt = v_ref[rows, :]

    def update(k):
        sn = S[d, k] * row8(w_ref, k) + (nsa * row8(b_ref, k) + vt * row8(kd_ref, k))
        S[d, k] = sn
        return sn * row8(r_ref, k)
    y_ref[rows, :] = _strided_sum(update(k) for k in range(HEAD_DIM))


def _lat_scan_specs():
    tb = SCAN_TB
    nt = LAT.seq // tb
    n8 = V7X_SUBLANES

    def specs(d):
        tblk = (lambda i: i) if d == 0 else (lambda i: nt - 1 - i)
        plane = lambda q: pl.BlockSpec((None, HEAD_DIM // n8, tb * n8, V7X_LANES), lambda i: (q, 0, tblk(i), 0))
        rows = pl.BlockSpec((tb * n8, V7X_LANES), lambda i: (tblk(i), 0))
        return [plane(LAT_Q_R), plane(LAT_Q_KK), rows,
                plane(LAT_Q_W + d), plane(LAT_Q_KD + d), plane(LAT_Q_B + d)], rows
    return specs(0), specs(1)


def _scan_lat_kernel(*refs):
    fwd, bwd = refs[0:6], refs[6:12]
    s0_ref, yf_ref, yb_ref, S = refs[12:]

    @pl.when(pl.program_id(0) == 0)
    def _():
        S[...] = s0_ref[...]

    def step(s, carry):
        _lat_scan_dir(S, 0, s, *fwd, yf_ref)
        _lat_scan_dir(S, 1, SCAN_TB - 1 - s, *bwd, yb_ref)
        return carry
    lax.fori_loop(0, SCAN_TB, step, 0)


def _scan_lat(kq, v_s, s0):
    n8 = V7X_SUBLANES
    (in_f, y_f), (in_b, y_b) = _lat_scan_specs()
    y_shape = jax.ShapeDtypeStruct((LAT.seq * n8, V7X_LANES), F32)
    one = (kq, kq, v_s, kq, kq, kq)
    return pl.pallas_call(
        _scan_lat_kernel,
        grid=(LAT.seq // SCAN_TB,),
        in_specs=in_f + in_b + [_const_spec((N_DIR, HEAD_DIM, n8, V7X_LANES))],
        out_specs=[y_f, y_b],
        out_shape=[y_shape, y_shape],
        scratch_shapes=[pltpu.VMEM((N_DIR, HEAD_DIM, n8, V7X_LANES), F32)],
        compiler_params=_cparams(1),
        name="rwkv_scan_lat",
    )(*one, *one, s0)


def _softmax_pv(scores, vals, sink_col):
    m = sink_col
    for s in scores:
        m = jnp.maximum(m, jnp.max(s, axis=-1, keepdims=True))
    den = jnp.exp(sink_col - m)
    out = None
    for s, vx in zip(scores, vals):
        p = jnp.exp(s - m)
        den = den + jnp.sum(p, axis=-1, keepdims=True)
        o = _dot(p.astype(BF16), vx)
        out = o if out is None else out + o
    return out / den


def _sink_col(sink_ref, layer, kv, n_rows):
    row = lax.broadcasted_iota(jnp.int32, (Q_PER_KV * n_rows, 1), 0)
    col = jnp.full((Q_PER_KV * n_rows, 1), sink_ref[layer, kv * Q_PER_KV], F32)
    for g in range(1, Q_PER_KV):
        col = jnp.where(row >= g * n_rows, sink_ref[layer, kv * Q_PER_KV + g], col)
    return col


def _qk(q2, kh):
    return lax.dot_general(q2, kh, (((1,), (1,)), ((), ())), preferred_element_type=F32) * ATT_SCALE


CTX_ATTN_SEQS = 4


def _ctx_attn_kernel(sink_ref, z_ref, o_ref, *, layer):
    hd = HEAD_DIM
    for s in range(CTX_ATTN_SEQS):
        rows = slice(s * SEQ, (s + 1) * SEQ)
        q = z_ref[rows, 0:ATT_HEADS * hd]
        outs = []
        for kv in range(KV_HEADS):
            kh = z_ref[rows, ATT_HEADS * hd + kv * hd:ATT_HEADS * hd + (kv + 1) * hd].astype(BF16)
            voff = (ATT_HEADS + KV_HEADS) * hd + kv * hd
            vh = z_ref[rows, voff:voff + hd].astype(BF16)
            q2 = jnp.concatenate(
                [q[:, (kv * Q_PER_KV + g) * hd:(kv * Q_PER_KV + g + 1) * hd] for g in range(Q_PER_KV)],
                axis=0).astype(BF16)
            o2 = _softmax_pv([_qk(q2, kh)], [vh], _sink_col(sink_ref, layer, kv, SEQ))
            outs += [o2[g * SEQ:(g + 1) * SEQ] for g in range(Q_PER_KV)]
        o_ref[rows, :] = jnp.concatenate(outs, axis=1).astype(BF16)


def _rope(x, cos, sin_signed):
    lane = lax.broadcasted_iota(jnp.int32, x.shape, 1)
    first = (lane % (HEAD_DIM // 2)) < (HEAD_DIM // 4)
    partner = jnp.where(first, pltpu.roll(x, V7X_LANES - HEAD_DIM // 4, 1), pltpu.roll(x, HEAD_DIM // 4, 1))
    return x * cos + partner * sin_signed


def _lat_attn_kernel(sink_ref, z_ref, ck_ref, cv_ref, cos_ref, sin_ref, o_ref, q_s, k_s, *, layer):
    hd = HEAD_DIM
    cos = cos_ref[...]
    sin = sin_ref[...]
    for j in range(ATT_HEADS * hd // V7X_LANES):
        q_s[:, j * V7X_LANES:(j + 1) * V7X_LANES] = _rope(
            z_ref[:, j * V7X_LANES:(j + 1) * V7X_LANES], cos, sin).astype(BF16)
    k_s[...] = _rope(z_ref[:, ATT_HEADS * hd:(ATT_HEADS + KV_HEADS) * hd], cos, sin).astype(BF16)
    voff = (ATT_HEADS + KV_HEADS) * hd
    nb = DEC_SEQ // BLOCK
    for n in range(nb):
        lo = max(n - 1, 0) * BLOCK
        hi = min(n + 2, nb) * BLOCK
        i_abs = n * BLOCK + lax.broadcasted_iota(jnp.int32, (Q_PER_KV * BLOCK, hi - lo), 0) % BLOCK
        j_abs = lo + lax.broadcasted_iota(jnp.int32, (Q_PER_KV * BLOCK, hi - lo), 1)
        band = jnp.abs(i_abs - j_abs) <= WINDOW
        outs = []
        for kv in range(KV_HEADS):
            q2 = jnp.concatenate(
                [q_s[n * BLOCK:(n + 1) * BLOCK, (kv * Q_PER_KV + g) * hd:(kv * Q_PER_KV + g + 1) * hd]
                 for g in range(Q_PER_KV)], axis=0)
            kw = k_s[lo:hi, kv * hd:(kv + 1) * hd]
            vw = z_ref[lo:hi, voff + kv * hd:voff + (kv + 1) * hd].astype(BF16)
            kc = ck_ref[:, kv * hd:(kv + 1) * hd].astype(BF16)
            vc = cv_ref[:, kv * hd:(kv + 1) * hd].astype(BF16)
            s_lat = jnp.where(band, _qk(q2, kw), -1e30)
            o2 = _softmax_pv([s_lat, _qk(q2, kc)], [vw, vc], _sink_col(sink_ref, layer, kv, BLOCK))
            outs += [o2[g * BLOCK:(g + 1) * BLOCK] for g in range(Q_PER_KV)]
        o_ref[n * BLOCK:(n + 1) * BLOCK, :] = jnp.concatenate(outs, axis=1).astype(BF16)


def _ctx_attn(z, sink, layer):
    wd = (ATT_HEADS + 2 * KV_HEADS) * HEAD_DIM
    return pl.pallas_call(
        functools.partial(_ctx_attn_kernel, layer=layer),
        grid=(BATCH // CTX_ATTN_SEQS,),
        in_specs=[
            pl.BlockSpec(memory_space=pltpu.SMEM),
            pl.BlockSpec((CTX_ATTN_SEQS * SEQ, wd), lambda i: (i, Z_D // wd)),
        ],
        out_specs=pl.BlockSpec((CTX_ATTN_SEQS * SEQ, BRANCH_W), lambda i: (i, 0)),
        out_shape=jax.ShapeDtypeStruct((CTX.n_tok, BRANCH_W), BF16),
        compiler_params=_cparams(1),
        name="ctx_attn",
    )(sink, z)


def _lat_attn(z, sink, layer, ck, cv, cos_t, sin_t):
    wd = (ATT_HEADS + 2 * KV_HEADS) * HEAD_DIM
    kvw = KV_HEADS * HEAD_DIM
    return pl.pallas_call(
        functools.partial(_lat_attn_kernel, layer=layer),
        grid=(DEC_BATCH,),
        in_specs=[
            pl.BlockSpec(memory_space=pltpu.SMEM),
            pl.BlockSpec((DEC_SEQ, wd), lambda i: (i, Z_D // wd)),
            pl.BlockSpec((None, None, PAST_LEN, kvw), lambda i: (i, layer, 0, 0)),
            pl.BlockSpec((None, None, PAST_LEN, kvw), lambda i: (i, layer, 0, 0)),
            _const_spec((DEC_SEQ, V7X_LANES)),
            _const_spec((DEC_SEQ, V7X_LANES)),
        ],
        out_specs=pl.BlockSpec((DEC_SEQ, BRANCH_W), lambda i: (i, 0)),
        out_shape=jax.ShapeDtypeStruct((LAT.n_tok, BRANCH_W), BF16),
        scratch_shapes=[pltpu.VMEM((DEC_SEQ, ATT_HEADS * HEAD_DIM), BF16),
                        pltpu.VMEM((DEC_SEQ, kvw), BF16)],
        compiler_params=_cparams(1),
        name="lat_attn",
    )(sink, z, ck, cv, cos_t, sin_t)


def _scan_out_ctx(yf_ref, yb_ref, slab):
    nb, tt = CTX.nb, CTX.tt
    cols = []
    for c in range(tt // RWKV_HEADS):
        slabs = _exchange_lane_groups([yf_ref[RWKV_HEADS * c + j] + yb_ref[RWKV_HEADS * c + j]
                                       for j in range(RWKV_HEADS)])
        cols.append(jnp.concatenate(slabs, axis=0))
    by_t = jnp.concatenate(cols, axis=1).T
    n_slab = BRANCH_W // V7X_LANES
    for t in range(tt):
        for s in range(n_slab):
            slab[s, pl.ds(t, nb, stride=tt), :] = by_t[t * nb:(t + 1) * nb, s * V7X_LANES:(s + 1) * V7X_LANES]
    return jnp.concatenate([slab[s] for s in range(n_slab)], axis=1)


def _scan_out_lat(yf_ref, yb_ref, ysum, gy):
    nb, tt = LAT.nb, LAT.tt
    chains = nb * RWKV_HEADS
    ysum[...] = yf_ref[...] + yb_ref[...]
    for vp in range(V7X_SUBLANES):
        a = ysum[pl.ds(vp, tt, stride=V7X_SUBLANES), :].T
        for vl in range(LAT_VL):
            gy[pl.ds(vp + V7X_SUBLANES * vl, chains, stride=HEAD_DIM), :] = a[vl * chains:(vl + 1) * chains, :]
    return jnp.concatenate([gy[b * BRANCH_W:(b + 1) * BRANCH_W, :].T for b in range(nb)], axis=0)


def _mix_kernel(x_ref, mod_ref, g1_ref, oa_ref, ob_ref, od_ref, yf_ref, yb_ref, g_ref, bonus_ref,
                lnxg_ref, lnxb_ref, ones_ref, wg_ref, bg_ref, wb_ref, wo_ref, x1_ref, *scratch, path):
    nb, tt = path.nb, path.tt
    rows = nb * tt
    r0, nr = path.mod_row0, path.mod_rows

    def mod3(j):
        return mod_ref[r0:r0 + nr, j * D_MODEL:(j + 1) * D_MODEL][:, None, :]

    x3 = x_ref[...]
    h3 = _rms(x3, g1_ref[...]) * (1.0 + mod3(1)) + mod3(0)
    hb = h3.reshape(rows, D_MODEL).astype(BF16)
    y = (_scan_out_ctx if path is CTX else _scan_out_lat)(yf_ref, yb_ref, *scratch)
    ones = ones_ref[...]
    mu = _head_sum(y, ones) * (1.0 / HEAD_DIM)
    dl = y - mu
    var = _head_sum(dl * dl, ones) * (1.0 / HEAD_DIM)
    yn = dl * lax.rsqrt(var + GN_EPS) * lnxg_ref[...] + lnxb_ref[...]
    bonus = bonus_ref[...].reshape(rows, BRANCH_W)
    gate_c = g_ref[...].reshape(rows, BRANCH_W)
    oc = ((yn + bonus) * gate_c).astype(BF16)
    branches = (oa_ref[...].reshape(rows, BRANCH_W), ob_ref[...].reshape(rows, BRANCH_W), oc,
                od_ref[...].reshape(rows, BRANCH_W))
    mixed = None
    for n, br in enumerate(branches):
        cols = slice(n * D_MODEL, (n + 1) * D_MODEL)
        gate = jax.nn.sigmoid(_dot(hb, wg_ref[:, cols]) + bg_ref[:, cols])
        term = gate * _dot(br, wb_ref[n])
        mixed = term if mixed is None else mixed + term
    mix = _dot(mixed.astype(BF16), wo_ref[...])
    x1_ref[...] = x3 + mod3(2) * mix.reshape(nb, tt, D_MODEL)


def _mix(x, mod, p, oa, ob, od, yf, yb, g, bonus, ones, path):
    nb, tt, seq = path.nb, path.tt, path.seq
    l = p["layer"]
    tokw = pl.BlockSpec((nb, tt, BRANCH_W), lambda i: (0, i, 0))
    tokd = pl.BlockSpec((nb, tt, D_MODEL), lambda i: (0, i, 0))
    n_slab = BRANCH_W // V7X_LANES
    if path is CTX:
        y_specs = [pl.BlockSpec((None, tt, HEAD_DIM, V7X_LANES), lambda i, d=d: (d, i, 0, 0)) for d in range(N_DIR)]
        scratch = [pltpu.VMEM((n_slab, nb * tt, V7X_LANES), F32)]
    else:
        y_specs = [pl.BlockSpec((tt * V7X_SUBLANES, V7X_LANES), lambda i: (i, 0))] * N_DIR
        scratch = [pltpu.VMEM((tt * V7X_SUBLANES, V7X_LANES), F32),
                   pltpu.VMEM((nb * BRANCH_W, V7X_LANES), F32)]
    as3 = lambda a: a.reshape(nb, seq, a.shape[-1])
    out = pl.pallas_call(
        functools.partial(_mix_kernel, path=path),
        grid=(seq // tt,),
        in_specs=[
            tokd,
            _layer_spec((MOD_ROWS, N_MOD * D_MODEL), l),
            _layer_spec((1, D_MODEL), l),
            tokw, tokw, tokw, *y_specs, tokw, tokw,
            _layer_spec((1, BRANCH_W), l),
            _layer_spec((1, BRANCH_W), l),
            _const_spec((BRANCH_W, BRANCH_W)),
            _layer_spec((D_MODEL, N_BRANCH * D_MODEL), l),
            _layer_spec((1, N_BRANCH * D_MODEL), l),
            _layer_spec((N_BRANCH, BRANCH_W, D_MODEL), l),
            _layer_spec((D_MODEL, D_MODEL), l),
        ],
        out_specs=tokd,
        out_shape=jax.ShapeDtypeStruct((nb, seq, D_MODEL), F32),
        scratch_shapes=scratch,
        compiler_params=_cparams(1),
        name=f"branch_mix_{seq}",
    )(as3(x), mod, p["norm1_g"], as3(oa), as3(ob), as3(od), yf, yb, g, bonus, p["rwkv_lnx_g"], p["rwkv_lnx_b"],
      ones, p["w_gate"], p["b_gate"], p["w_branch"], p["w_out"])
    return out.reshape(path.n_tok, D_MODEL)


ROUTER_LANES = V7X_LANES


def _route(logits):
    lane = lax.broadcasted_iota(jnp.int32, logits.shape, 1).astype(F32)
    ninf = -jnp.inf
    big = float(ROUTER_LANES)
    gmask = lane < N_GROUPS
    gl = jnp.where(gmask, logits, ninf)
    gmax = jnp.max(gl, axis=-1, keepdims=True)
    gidx = jnp.min(jnp.where(gl == gmax, lane, big), axis=-1, keepdims=True)
    g_w = 1.0 / jnp.sum(jnp.where(gmask, jnp.exp(gl - gmax), 0.0), axis=-1, keepdims=True)
    egroup = jnp.floor((lane - N_GROUPS) * (1.0 / EXPERTS_PER_GROUP))
    emask = (lane >= N_GROUPS) & (lane < N_GROUPS + N_EXPERTS) & (egroup == gidx)
    el = jnp.where(emask, logits, ninf)
    e1 = jnp.max(el, axis=-1, keepdims=True)
    i1 = jnp.min(jnp.where(emask & (el == e1), lane, big), axis=-1, keepdims=True)
    el2 = jnp.where(lane == i1, ninf, el)
    e2 = jnp.max(el2, axis=-1, keepdims=True)
    i2 = jnp.min(jnp.where(emask & (lane != i1) & (el2 == e2), lane, big), axis=-1, keepdims=True)
    t = jnp.exp(e2 - e1)
    den = 1.0 + t
    return jnp.where(lane == i1, g_w * (1.0 / den), 0.0) + jnp.where(lane == i2, g_w * (t / den), 0.0)


def _moe_input(x1, mod_ref, row, g2_ref, wrh_ref, wrl_ref, br_ref):
    h2 = _rms(x1, g2_ref[...]) * (1.0 + _mod_slice(mod_ref, row, 4)) + _mod_slice(mod_ref, row, 3)
    return h2.astype(BF16), _route(_dot_split(h2, wrh_ref[...], wrl_ref[...]) + br_ref[...])


def _expert(hb, weg, weu, wed, c):
    hg = _dot(hb, weg)
    hu = _dot(hb, weu)
    return _dot((hg * jax.nn.sigmoid(hg) * hu * c).astype(BF16), wed)


def _moe_finish(x1, acc, mod_ref, row, fg_ref, out_refs, final):
    x2 = x1 + _mod_slice(mod_ref, row, 5) * acc
    out_refs[0][...] = x2
    if final:
        out_refs[1][...] = _rms(x2, fg_ref[...])


def _moe_kernel(x1_ref, mod_ref, g2_ref, wrh_ref, wrl_ref, br_ref, weg_ref, weu_ref, wed_ref, fg_ref,
                *out_refs, path, final):
    row = _mod_row(pl.program_id(0), path)
    x1 = x1_ref[...]
    hb, comb = _moe_input(x1, mod_ref, row, g2_ref, wrh_ref, wrl_ref, br_ref)
    acc = None
    for e in range(N_EXPERTS):
        lane = N_GROUPS + e
        term = _expert(hb, weg_ref[e], weu_ref[e], wed_ref[e], comb[:, lane:lane + 1])
        acc = term if acc is None else acc + term
    _moe_finish(x1, acc, mod_ref, row, fg_ref, out_refs, final)


def _moe_in_specs(tm, l):
    single = pl.Buffered(1)
    up = _layer_spec((N_EXPERTS, D_MODEL, EXPERT_FF), l, pipeline_mode=single)
    return [
        pl.BlockSpec((tm, D_MODEL), lambda i: (i, 0)),
        _layer_spec((MOD_ROWS, N_MOD * D_MODEL), l),
        _layer_spec((1, D_MODEL), l),
        _layer_spec((D_MODEL, ROUTER_LANES), l),
        _layer_spec((D_MODEL, ROUTER_LANES), l),
        _layer_spec((1, ROUTER_LANES), l),
        up, up,
        _layer_spec((N_EXPERTS, EXPERT_FF, D_MODEL), l, pipeline_mode=single),
        _const_spec((1, D_MODEL)),
    ]


def _moe_args(x1, mod, p, final_g):
    return (x1, mod, p["norm2_g"], *p["w_router"], p["b_router"], p["w_e_gate"], p["w_e_up"], p["w_e_down"], final_g)


def _moe(x1, mod, p, final_g, final, path):
    tm = TOK_TILE
    tokd = pl.BlockSpec((tm, D_MODEL), lambda i: (i, 0))
    out = jax.ShapeDtypeStruct((path.n_tok, D_MODEL), F32)
    return pl.pallas_call(
        functools.partial(_moe_kernel, path=path, final=final),
        grid=(path.n_tok // tm,),
        in_specs=_moe_in_specs(tm, p["layer"]),
        out_specs=[tokd, tokd] if final else [tokd],
        out_shape=[out, out] if final else [out],
        compiler_params=_cparams(1),
        name=f"moe_{path.seq}" + ("_final" if final else ""),
    )(*_moe_args(x1, mod, p, final_g))


def _block_diag2(w):
    z = jnp.zeros_like(w[:, 0])
    return jnp.concatenate([jnp.concatenate([w[:, 0], z], axis=2), jnp.concatenate([z, w[:, 1]], axis=2)], axis=1)


def _all_layer_params(a):
    row = lambda v: v.reshape(DEPTH, 1, -1)
    hi_lo = lambda w: jnp.stack(_split(w), axis=1)
    w_in = a["w_in"]
    w_router = jnp.zeros((DEPTH, D_MODEL, ROUTER_LANES), F32)
    w_router = w_router.at[:, :, 0:N_GROUPS].set(a["w_rg"]).at[:, :, N_GROUPS:N_GROUPS + N_EXPERTS].set(a["w_re"])
    b_router = jnp.zeros((DEPTH, 1, ROUTER_LANES), F32)
    b_router = b_router.at[:, 0, 0:N_GROUPS].set(a["b_rg"]).at[:, 0, N_GROUPS:N_GROUPS + N_EXPERTS].set(a["b_re"])
    return {
        "norm1_g": row(a["norm1_g"]), "norm2_g": row(a["norm2_g"]),
        "w_in": jnp.concatenate([w_in[:, :, 0:2048], w_in[:, :, 2176:2688], w_in[:, :, 2048:2176]], axis=2).astype(BF16),
        "gmlp_ln_g": row(a["gmlp_ln_g"]),
        "gmlp_wcat": a["gmlp_ws"].transpose(0, 2, 1, 3).reshape(DEPTH, CHUNK, GMLP_GROUPS * CHUNK).astype(BF16),
        "gmlp_bsx": jnp.repeat(a["gmlp_bs"].transpose(0, 2, 1), BRANCH_W // GMLP_GROUPS, axis=2),
        "conv_w": a["conv_w"], "conv_b": row(a["conv_b"]),
        "conv_ln_g": row(a["conv_ln_g"]), "conv_ln_b": row(a["conv_ln_b"]),
        "rwkv_w0": row(a["rwkv_w0"]), "rwkv_w2blk": hi_lo(_block_diag2(a["rwkv_w2"])),
        "rwkv_a0": row(a["rwkv_a0"]), "rwkv_a2blk": hi_lo(_block_diag2(a["rwkv_a2"])),
        "rwkv_g2": hi_lo(a["rwkv_g2"]), "rwkv_kk": row(a["rwkv_kk"]), "rwkv_ka": row(a["rwkv_ka"]),
        "rwkv_rk": row(a["rwkv_rk"]),
        "rwkv_lnx_g": row(a["rwkv_lnx_g"]), "rwkv_lnx_b": row(a["rwkv_lnx_b"]),
        "attn_sink": a["attn_sink"],
        "w_gate": a["w_gate"].astype(BF16), "b_gate": row(a["b_gate"]),
        "w_branch": a["w_branch"].astype(BF16), "w_out": a["w_out"].astype(BF16),
        "w_router": _split(w_router), "b_router": b_router,
        "w_e_gate": a["w_e_gate"].astype(BF16), "w_e_up": a["w_e_up"].astype(BF16),
        "w_e_down": a["w_e_down"].astype(BF16),
    }


def _rope_tables():
    half = HEAD_DIM // 4
    inv = ROPE_BASE ** (-jnp.arange(half, dtype=F32) / half)
    t = jnp.arange(DEC_SEQ)
    row = (t // GRID_W).astype(F32)[:, None] * inv[None, :]
    col = (t % GRID_W).astype(F32)[:, None] * inv[None, :]
    cos_h = jnp.concatenate([jnp.cos(row), jnp.cos(row), jnp.cos(col), jnp.cos(col)], axis=1)
    sin_h = jnp.concatenate([-jnp.sin(row), jnp.sin(row), -jnp.sin(col), jnp.sin(col)], axis=1)
    reps = V7X_LANES // HEAD_DIM
    return jnp.tile(cos_h, (1, reps)), jnp.tile(sin_h, (1, reps))


def _mixers(x, mod, p, path, attn):
    z, *kv = _inproj(x, mod, p, path)
    oa, ob = _local_mix(z, p, path)
    return z, kv, oa, ob, attn(z)


def kernel(x_prompt, x_sample, cache_k, cache_v, state_rwkv, c, c_ctx, norm1_g, norm2_g, final_norm_g, w_mod, b_mod, w_in, gmlp_ln_g, gmlp_ws, gmlp_bs, conv_w, conv_b, conv_ln_g, conv_ln_b, rwkv_w0, rwkv_w2, rwkv_a0, rwkv_a2, rwkv_g2, rwkv_kk, rwkv_ka, rwkv_rk, rwkv_lnx_g, rwkv_lnx_b, attn_sink, w_gate, b_gate, w_branch, w_out, w_rg, b_rg, w_re, b_re, w_e_gate, w_e_up, w_e_down):
    arrays = dict(norm1_g=norm1_g, norm2_g=norm2_g, w_in=w_in, gmlp_ln_g=gmlp_ln_g, gmlp_ws=gmlp_ws,
                  gmlp_bs=gmlp_bs, conv_w=conv_w, conv_b=conv_b, conv_ln_g=conv_ln_g, conv_ln_b=conv_ln_b,
                  rwkv_w0=rwkv_w0, rwkv_w2=rwkv_w2, rwkv_a0=rwkv_a0, rwkv_a2=rwkv_a2, rwkv_g2=rwkv_g2,
                  rwkv_kk=rwkv_kk, rwkv_ka=rwkv_ka, rwkv_rk=rwkv_rk, rwkv_lnx_g=rwkv_lnx_g,
                  rwkv_lnx_b=rwkv_lnx_b, attn_sink=attn_sink, w_gate=w_gate, b_gate=b_gate,
                  w_branch=w_branch, w_out=w_out, w_rg=w_rg, b_rg=b_rg, w_re=w_re, b_re=b_re,
                  w_e_gate=w_e_gate, w_e_up=w_e_up, w_e_down=w_e_down)
    xc = x_prompt.reshape(CTX.n_tok, D_MODEL)
    xl = x_sample.reshape(LAT.n_tok, D_MODEL)
    cvec = jnp.concatenate([c_ctx[None], c, jnp.zeros((MOD_ROWS - 1 - DEC_BATCH, D_MODEL), F32)], axis=0)
    mod_all = _modulation(cvec, w_mod, b_mod)
    ones = _head_ones()
    cos_t, sin_t = _rope_tables()
    final_g = final_norm_g.reshape(1, D_MODEL)
    kvw = KV_HEADS * HEAD_DIM
    ks_out, vs_out, ss_out = [], [], []
    yc = yl = None
    params = _all_layer_params(arrays)
    ck = cache_k.reshape(DEC_BATCH, DEPTH, PAST_LEN, kvw)
    cv = cache_v.reshape(DEC_BATCH, DEPTH, PAST_LEN, kvw)
    mod = mod_all
    for l in range(DEPTH):
        p = dict(params, layer=l)
        final = l == DEPTH - 1
        zc, kv_c, oa, ob, od = _mixers(xc, mod, p, CTX, lambda z: _ctx_attn(z, p["attn_sink"], l))
        g, bonus, r_s, kk_s, v_s, w_s, kd_s, b_s = _prep_ctx(zc, p, ones)
        y_c, s_fin = _scan_ctx(r_s, kk_s, v_s, w_s, kd_s, b_s)
        x1c = _mix(xc, mod, p, oa, ob, od, y_c, y_c, g, bonus, ones, CTX)
        zl, _, oa, ob, od = _mixers(xl, mod, p, LAT, lambda z: _lat_attn(z, p["attn_sink"], l, ck, cv, cos_t, sin_t))
        g, bonus, v_l, kq = _prep_lat(zl, p, ones)
        s0 = state_rwkv[:, l].reshape(DEC_BATCH, N_DIR, RWKV_HEADS, LAT_VL, V7X_SUBLANES, HEAD_DIM)
        s0 = s0.transpose(1, 5, 4, 3, 0, 2).reshape(N_DIR, HEAD_DIM, V7X_SUBLANES, V7X_LANES)
        yf, yb = _scan_lat(kq, v_l, s0)
        outs_c = _moe(x1c, mod, p, final_g, final, CTX)
        x1l = _mix(xl, mod, p, oa, ob, od, yf, yb, g, bonus, ones, LAT)
        outs_l = _moe(x1l, mod, p, final_g, final, LAT)
        xc, xl = outs_c[0], outs_l[0]
        if final:
            yc, yl = outs_c[1], outs_l[1]
        ks_out.append(kv_c[0].reshape(BATCH, SEQ, KV_HEADS, HEAD_DIM))
        vs_out.append(kv_c[1].reshape(BATCH, SEQ, KV_HEADS, HEAD_DIM))
        ss_out.append(s_fin.reshape(N_DIR, HEAD_DIM, HEAD_DIM, RWKV_HEADS, BATCH).transpose(4, 0, 3, 2, 1))
    y_prompt = yc.reshape(BATCH, SEQ, D_MODEL)
    y_sample = yl.reshape(DEC_BATCH, DEC_SEQ, D_MODEL)
    return (y_prompt, y_sample, jnp.stack(ks_out, axis=1), jnp.stack(vs_out, axis=1), jnp.stack(ss_out, axis=1))
```

```python
import functools
import math
from typing import NamedTuple

import jax
import jax.numpy as jnp
from jax import lax
from jax.experimental import pallas as pl
from jax.experimental.pallas import tpu as pltpu

D_MODEL = 1024
BATCH = 32
SEQ = 256
DEPTH = 2
DEC_BATCH = 4
DEC_SEQ = 1024
PAST_LEN = 256
GRID_W = 64
HEAD_DIM = 64
BRANCH_W = 256
N_BRANCH = 4
CHUNK = 128
GMLP_GROUPS = 4
CONV_W = 31
RWKV_HEADS = 4
N_DIR = 2
DECAY_LORA = 64
ICL_LORA = 64
GATE_LORA = 128
ATT_HEADS = 4
KV_HEADS = 2
Q_PER_KV = ATT_HEADS // KV_HEADS
WINDOW = 128
BLOCK = 128
ROPE_BASE = 10000.0
ATT_SCALE = HEAD_DIM ** -0.5
N_GROUPS = 4
EXPERTS_PER_GROUP = 4
N_EXPERTS = N_GROUPS * EXPERTS_PER_GROUP
EXPERT_FF = 256
N_MOD = 6
RMS_EPS = 1e-6
LN_EPS = 1e-5
GN_EPS = 64e-5
IN_COLS = 2688

V7X_LANES = 128
V7X_SUBLANES = 8
V7X_VMEM_LIMIT = 56 * 1024 * 1024

MOD_ROWS = 8
TOK_TILE = 512
SCAN_TB = 32
CONV_PAD = 16

Z_AB, Z_C, Z_D, Z_G = 0, 1024, 2048, 2560

LAT_Q_R, LAT_Q_KK, LAT_Q_W, LAT_Q_KD, LAT_Q_B = 0, 1, 2, 4, 6
LAT_NQ = 8
LAT_Q_PER_STEP = 4
LAT_VL = V7X_LANES // (DEC_BATCH * RWKV_HEADS)

F32 = jnp.float32
BF16 = jnp.bfloat16
HIGHEST = lax.Precision.HIGHEST


class _Path(NamedTuple):
    nb: int
    seq: int
    mod_row0: int
    mod_rows: int
    tt: int

    @property
    def n_tok(self):
        return self.nb * self.seq


CTX = _Path(BATCH, SEQ, 0, 1, TOK_TILE // BATCH)
LAT = _Path(DEC_BATCH, DEC_SEQ, 1, DEC_BATCH, TOK_TILE // DEC_BATCH)


def _cparams(n_axes, vmem=V7X_VMEM_LIMIT):
    return pltpu.CompilerParams(dimension_semantics=("arbitrary",) * n_axes, vmem_limit_bytes=vmem)


def _const_spec(shape):
    nd = len(shape)
    return pl.BlockSpec(shape, lambda *_: (0,) * nd)


def _layer_spec(shape, l, **kw):
    nd = len(shape)
    return pl.BlockSpec((None,) + tuple(shape), lambda *_: (l,) + (0,) * nd, **kw)


def _dot(a, b):
    return jnp.dot(a, b, preferred_element_type=F32)


def _dot_hi(a, b):
    return jnp.dot(a, b, preferred_element_type=F32, precision=HIGHEST)


def _split(x):
    hi = x.astype(BF16)
    return hi, (x - hi.astype(F32)).astype(BF16)


def _dot_split(a, b_hi, b_lo):
    a_hi, a_lo = _split(a)
    return _dot(a_hi, b_hi) + (_dot(a_lo, b_hi) + _dot(a_hi, b_lo))


def _head_sum(x, ones_b):
    hi, lo = _split(x)
    return _dot(hi, ones_b) + _dot(lo, ones_b)


def _head_ones():
    r = lax.broadcasted_iota(jnp.int32, (BRANCH_W, BRANCH_W), 0) // HEAD_DIM
    c = lax.broadcasted_iota(jnp.int32, (BRANCH_W, BRANCH_W), 1) // HEAD_DIM
    return (r == c).astype(BF16)


def _mod_row(tile, path):
    if path.mod_rows == 1:
        return path.mod_row0
    return path.mod_row0 + tile // (path.seq // TOK_TILE)


def _mod_slice(mod_ref, row, j):
    return mod_ref[pl.ds(row, 1), j * D_MODEL:(j + 1) * D_MODEL]


def _rms(x, g):
    return x * lax.rsqrt(jnp.mean(x * x, axis=-1, keepdims=True) + RMS_EPS) * g


def _layernorm(x, g, b=None, eps=LN_EPS):
    mu = jnp.mean(x, axis=-1, keepdims=True)
    d = x - mu
    var = jnp.mean(d * d, axis=-1, keepdims=True)
    y = d * lax.rsqrt(var + eps) * g
    return y if b is None else y + b


def _mod_kernel(c_ref, w_ref, b_ref, o_ref):
    c = c_ref[...]
    a = c * jax.nn.sigmoid(c)
    o_ref[0] = _dot_hi(a, w_ref[0]) + b_ref[0]


def _modulation(cvec, w_mod, b_mod):
    cols = 2 * D_MODEL
    return pl.pallas_call(
        _mod_kernel,
        grid=(DEPTH, N_MOD * D_MODEL // cols),
        in_specs=[
            pl.BlockSpec((MOD_ROWS, D_MODEL), lambda l, j: (0, 0)),
            pl.BlockSpec((1, D_MODEL, cols), lambda l, j: (l, 0, j)),
            pl.BlockSpec((1, 1, cols), lambda l, j: (l, 0, j)),
        ],
        out_specs=pl.BlockSpec((1, MOD_ROWS, cols), lambda l, j: (l, 0, j)),
        out_shape=jax.ShapeDtypeStruct((DEPTH, MOD_ROWS, N_MOD * D_MODEL), F32),
        compiler_params=_cparams(2),
        name="modulation",
    )(cvec, w_mod, b_mod.reshape(DEPTH, 1, N_MOD * D_MODEL))


def _inproj_kernel(x_ref, mod_ref, g_ref, w_ref, z_ref, *kv_refs, path):
    row = _mod_row(pl.program_id(0), path)
    h = _rms(x_ref[...], g_ref[...]) * (1.0 + _mod_slice(mod_ref, row, 1)) + _mod_slice(mod_ref, row, 0)
    z = _dot(h.astype(BF16), w_ref[...])
    z_ref[...] = z
    kvw = KV_HEADS * HEAD_DIM
    for j, ref in enumerate(kv_refs):
        off = Z_D + ATT_HEADS * HEAD_DIM + j * kvw
        ref[...] = z[:, off:off + kvw]


def _inproj(x, mod, p, path):
    tm = TOK_TILE
    l = p["layer"]
    kvw = KV_HEADS * HEAD_DIM
    n_kv = 2 if path is CTX else 0
    return pl.pallas_call(
        functools.partial(_inproj_kernel, path=path),
        grid=(path.n_tok // tm,),
        in_specs=[
            pl.BlockSpec((tm, D_MODEL), lambda i: (i, 0)),
            _layer_spec((MOD_ROWS, N_MOD * D_MODEL), l),
            _layer_spec((1, D_MODEL), l),
            _layer_spec((D_MODEL, IN_COLS), l),
        ],
        out_specs=[pl.BlockSpec((tm, IN_COLS), lambda i: (i, 0))] + [pl.BlockSpec((tm, kvw), lambda i: (i, 0))] * n_kv,
        out_shape=[jax.ShapeDtypeStruct((path.n_tok, IN_COLS), F32)]
        + [jax.ShapeDtypeStruct((path.n_tok, kvw), F32)] * n_kv,
        compiler_params=_cparams(1),
        name=f"inproj_{path.seq}",
    )(x, mod, p["norm1_g"], p["w_in"])


def _local_mix_kernel(z_ref, lng_ref, wcat_ref, bsx_ref, cw_ref, cb_ref, clg_ref, clb_ref,
                      oa_ref, ob_ref, ypad, yshift, *, seq_len):
    lane_group = lax.broadcasted_iota(jnp.int32, (CHUNK, BRANCH_W), 1) // (BRANCH_W // GMLP_GROUPS)
    for c in range(seq_len // CHUNK):
        rows = pl.ds(c * CHUNK, CHUNK)
        u = jax.nn.gelu(z_ref[rows, 0:BRANCH_W], approximate=True)
        v = jax.nn.gelu(z_ref[rows, BRANCH_W:2 * BRANCH_W], approximate=True)
        vn = _layernorm(v, lng_ref[...])
        vblk = jnp.concatenate(
            [jnp.where(lane_group == g, vn, 0.0) for g in range(GMLP_GROUPS)], axis=0).astype(BF16)
        mixed = _dot(wcat_ref[...], vblk) + bsx_ref[...]
        oa_ref[rows, :] = (u * mixed).astype(BF16)

    n8 = V7X_SUBLANES
    ypad[0:CONV_PAD, :] = jnp.zeros((CONV_PAD, BRANCH_W), F32)
    ypad[CONV_PAD + seq_len:, :] = jnp.zeros((CONV_PAD + n8, BRANCH_W), F32)
    ypad[CONV_PAD:CONV_PAD + seq_len, :] = (
        z_ref[:, 2 * BRANCH_W:3 * BRANCH_W] * jax.nn.sigmoid(z_ref[:, 3 * BRANCH_W:4 * BRANCH_W]))
    span = seq_len + 2 * CONV_PAD
    for r in range(1, n8):
        yshift[r - 1] = ypad[r:r + span, :]
    base = CONV_PAD - CONV_W // 2
    for c in range(seq_len // CHUNK):
        acc = jnp.zeros((CHUNK, BRANCH_W), F32)
        for j in range(CONV_W):
            off = c * CHUNK + base + j
            r = off % n8
            rows = pl.ds(off - r, CHUNK)
            acc = acc + cw_ref[j:j + 1, :] * (ypad[rows, :] if r == 0 else yshift[r - 1, rows, :])
        y = _layernorm(acc + cb_ref[...], clg_ref[...], clb_ref[...])
        ob_ref[pl.ds(c * CHUNK, CHUNK), :] = (y * jax.nn.sigmoid(y)).astype(BF16)


def _local_mix(z, p, path):
    l = p["layer"]
    out = jax.ShapeDtypeStruct((path.n_tok, BRANCH_W), BF16)
    return pl.pallas_call(
        functools.partial(_local_mix_kernel, seq_len=path.seq),
        grid=(path.nb,),
        in_specs=[
            pl.BlockSpec((path.seq, 4 * BRANCH_W), lambda i: (i, Z_AB // (4 * BRANCH_W))),
            _layer_spec((1, BRANCH_W), l),
            _layer_spec((CHUNK, GMLP_GROUPS * CHUNK), l),
            _layer_spec((CHUNK, BRANCH_W), l),
            _layer_spec((CONV_W, BRANCH_W), l),
            _layer_spec((1, BRANCH_W), l),
            _layer_spec((1, BRANCH_W), l),
            _layer_spec((1, BRANCH_W), l),
        ],
        out_specs=[pl.BlockSpec((path.seq, BRANCH_W), lambda i: (i, 0))] * 2,
        out_shape=[out, out],
        scratch_shapes=[pltpu.VMEM((path.seq + 2 * CONV_PAD + V7X_SUBLANES, BRANCH_W), F32),
                        pltpu.VMEM((V7X_SUBLANES - 1, path.seq + 2 * CONV_PAD, BRANCH_W), F32)],
        compiler_params=_cparams(1),
        name=f"local_mix_{path.seq}",
    )(z, p["gmlp_ln_g"], p["gmlp_wcat"], p["gmlp_bsx"], p["conv_w"], p["conv_b"],
      p["conv_ln_g"], p["conv_ln_b"])


def _rwkv_quantities(zc, gd, w0_ref, w2_ref, a0_ref, a2_ref, g2_ref, kkp_ref, ka_ref, rk_ref, ones):
    W = BRANCH_W
    r = zc[:, 0:W]
    k = zc[:, W:2 * W]
    v = zc[:, 2 * W:3 * W]
    wd = zc[:, 3 * W:3 * W + N_DIR * DECAY_LORA]
    ad = zc[:, 3 * W + N_DIR * DECAY_LORA:4 * W]
    w_raw = _dot_split(jnp.tanh(wd), w2_ref[0], w2_ref[1]) + w0_ref[...]
    decay = jnp.exp(-math.exp(-0.5) * jax.nn.sigmoid(w_raw))
    a = jax.nn.sigmoid(_dot_split(ad, a2_ref[0], a2_ref[1]) + a0_ref[...])
    g = _dot_split(jax.nn.sigmoid(gd), g2_ref[0], g2_ref[1])
    kkr = k * kkp_ref[...]
    kk = kkr / jnp.maximum(jnp.sqrt(_head_sum(kkr * kkr, ones)), 1e-12)
    ka = ka_ref[...]
    w, kd, b = [], [], []
    for d in range(N_DIR):
        a_d = a[:, d * W:(d + 1) * W]
        w.append(decay[:, d * W:(d + 1) * W])
        kd.append(k * (1.0 + (a_d - 1.0) * ka))
        b.append(kk * a_d)
    bonus = _head_sum(r * (kd[0] + kd[1]) * rk_ref[...], ones) * v
    return g, bonus, r, kk, v, w, kd, b


def _exchange_lane_groups(x):
    x0, x1, x2, x3 = x
    quarter = V7X_LANES // 4
    lane = lax.broadcasted_iota(jnp.int32, x0.shape, 1)
    low_half = lane < 2 * quarter
    even_quarter = (lane // quarter) % 2 == 0
    roll = lambda a, s: pltpu.roll(a, s, 1)
    y0 = jnp.where(low_half, x0, roll(x2, 2 * quarter))
    y2 = jnp.where(low_half, roll(x0, 2 * quarter), x2)
    y1 = jnp.where(low_half, x1, roll(x3, 2 * quarter))
    y3 = jnp.where(low_half, roll(x1, 2 * quarter), x3)
    return (jnp.where(even_quarter, y0, roll(y1, quarter)), jnp.where(even_quarter, roll(y0, 3 * quarter), y1),
            jnp.where(even_quarter, y2, roll(y3, quarter)), jnp.where(even_quarter, roll(y2, 3 * quarter), y3))


def _prep_ctx_kernel(zc_ref, zg_ref, w0_ref, w2_ref, a0_ref, a2_ref, g2_ref, kkp_ref, ka_ref, rk_ref,
                     ones_ref, g_o, bonus_o, r_o, kk_o, v_o, w_o, kd_o, b_o, slab):
    nb, tt = CTX.nb, CTX.tt
    rows = nb * tt
    zc = zc_ref[...].reshape(rows, 4 * BRANCH_W)
    gd = zg_ref[...].reshape(rows, GATE_LORA)
    g, bonus, r, kk, v, w, kd, b = _rwkv_quantities(
        zc, gd, w0_ref, w2_ref, a0_ref, a2_ref, g2_ref, kkp_ref, ka_ref, rk_ref, ones_ref[...])
    g_o[...] = g.reshape(nb, tt, BRANCH_W)
    bonus_o[...] = bonus.reshape(nb, tt, BRANCH_W)
    n_slab = BRANCH_W // V7X_LANES
    lanes_per_head = V7X_LANES // RWKV_HEADS

    def to_chains(q, put):
        for s in range(n_slab):
            slab[s] = q[:, s * V7X_LANES:(s + 1) * V7X_LANES]
        by_t = jnp.concatenate(
            [jnp.concatenate([slab[s, pl.ds(t, nb, stride=tt), :] for s in range(n_slab)], axis=1)
             for t in range(tt)], axis=0)
        tr = by_t.T
        for c in range(tt // RWKV_HEADS):
            tiles = _exchange_lane_groups(
                [tr[h * HEAD_DIM:(h + 1) * HEAD_DIM, c * V7X_LANES:(c + 1) * V7X_LANES] for h in range(RWKV_HEADS)])
            for j, tile in enumerate(tiles):
                put(RWKV_HEADS * c + j, tile)
    assert lanes_per_head == nb

    def put_into(ref, *lead):
        def put(t, tile):
            ref[(*lead, t)] = tile
        return put

    to_chains(r, put_into(r_o))
    to_chains(kk, put_into(kk_o))
    to_chains(v, put_into(v_o))
    for d in range(N_DIR):
        to_chains(w[d], put_into(w_o, d))
        to_chains(kd[d], put_into(kd_o, d))
        to_chains(b[d], put_into(b_o, d))


def _prep_lat_kernel(zc_ref, zg_ref, w0_ref, w2_ref, a0_ref, a2_ref, g2_ref, kkp_ref, ka_ref, rk_ref,
                     ones_ref, g_o, bonus_o, v_o, kq_o, gk, gv):
    nb, tt = LAT.nb, LAT.tt
    rows = nb * tt
    q0 = pl.program_id(1) * LAT_Q_PER_STEP
    chains = nb * RWKV_HEADS

    @pl.when(q0 == 0)
    def _():
        zc = zc_ref[...].reshape(rows, 4 * BRANCH_W)
        gd = zg_ref[...].reshape(rows, GATE_LORA)
        g, bonus, r, kk, v, w, kd, b = _rwkv_quantities(
            zc, gd, w0_ref, w2_ref, a0_ref, a2_ref, g2_ref, kkp_ref, ka_ref, rk_ref, ones_ref[...])
        g_o[...] = g.reshape(nb, tt, BRANCH_W)
        bonus_o[...] = bonus.reshape(nb, tt, BRANCH_W)
        k_indexed = {LAT_Q_R: r, LAT_Q_KK: kk, LAT_Q_W: w[0], LAT_Q_W + 1: w[1],
                     LAT_Q_KD: kd[0], LAT_Q_KD + 1: kd[1], LAT_Q_B: b[0], LAT_Q_B + 1: b[1]}
        for bi in range(nb):
            for q_idx, q in k_indexed.items():
                gk[q_idx, bi * BRANCH_W:(bi + 1) * BRANCH_W, :] = q[bi * tt:(bi + 1) * tt, :].T
            gv[bi * BRANCH_W:(bi + 1) * BRANCH_W, :] = v[bi * tt:(bi + 1) * tt, :].T
        for vp in range(V7X_SUBLANES):
            a = jnp.concatenate(
                [gv[pl.ds(vp + V7X_SUBLANES * vl, chains, stride=HEAD_DIM), :] for vl in range(LAT_VL)], axis=0)
            v_o[pl.ds(vp, tt, stride=V7X_SUBLANES), :] = a.T

    def body(k, carry):
        n8 = V7X_SUBLANES
        for dq in range(LAT_Q_PER_STEP):
            a = gk[q0 + dq, pl.ds(k, chains, stride=HEAD_DIM), :]
            kq_o[dq, k // n8, pl.ds(k % n8, tt, stride=n8), :] = jnp.concatenate([a] * LAT_VL, axis=0).T
        return carry
    lax.fori_loop(0, HEAD_DIM, body, 0, unroll=4)


def _prep_param_specs(l):
    return [
        _layer_spec((1, N_DIR * BRANCH_W), l),
        _layer_spec((2, N_DIR * DECAY_LORA, N_DIR * BRANCH_W), l),
        _layer_spec((1, N_DIR * BRANCH_W), l),
        _layer_spec((2, N_DIR * ICL_LORA, N_DIR * BRANCH_W), l),
        _layer_spec((2, GATE_LORA, BRANCH_W), l),
        _layer_spec((1, BRANCH_W), l),
        _layer_spec((1, BRANCH_W), l),
        _layer_spec((1, BRANCH_W), l),
        _const_spec((BRANCH_W, BRANCH_W)),
    ]


def _prep_params(p, ones):
    return (p["rwkv_w0"], p["rwkv_w2blk"], p["rwkv_a0"], p["rwkv_a2blk"], p["rwkv_g2"],
            p["rwkv_kk"], p["rwkv_ka"], p["rwkv_rk"], ones)


def _prep_ctx(z, p, ones):
    nb, tt, seq = CTX.nb, CTX.tt, CTX.seq
    z3 = z.reshape(nb, seq, IN_COLS)
    tok = jax.ShapeDtypeStruct((nb, seq, BRANCH_W), F32)
    tok_spec = pl.BlockSpec((nb, tt, BRANCH_W), lambda i: (0, i, 0))
    ch = jax.ShapeDtypeStruct((seq, HEAD_DIM, V7X_LANES), F32)
    ch2 = jax.ShapeDtypeStruct((N_DIR, seq, HEAD_DIM, V7X_LANES), F32)
    ch_spec = pl.BlockSpec((tt, HEAD_DIM, V7X_LANES), lambda i: (i, 0, 0))
    ch2_spec = pl.BlockSpec((N_DIR, tt, HEAD_DIM, V7X_LANES), lambda i: (0, i, 0, 0))
    return pl.pallas_call(
        _prep_ctx_kernel,
        grid=(seq // tt,),
        in_specs=[
            pl.BlockSpec((nb, tt, 4 * BRANCH_W), lambda i: (0, i, Z_C // (4 * BRANCH_W))),
            pl.BlockSpec((nb, tt, GATE_LORA), lambda i: (0, i, Z_G // GATE_LORA)),
        ] + _prep_param_specs(p["layer"]),
        out_specs=[tok_spec, tok_spec, ch_spec, ch_spec, ch_spec, ch2_spec, ch2_spec, ch2_spec],
        out_shape=[tok, tok, ch, ch, ch, ch2, ch2, ch2],
        scratch_shapes=[pltpu.VMEM((BRANCH_W // V7X_LANES, nb * tt, V7X_LANES), F32)],
        compiler_params=_cparams(1),
        name="rwkv_prep_ctx",
    )(z3, z3, *_prep_params(p, ones))


def _prep_lat(z, p, ones):
    nb, tt, seq = LAT.nb, LAT.tt, LAT.seq
    z3 = z.reshape(nb, seq, IN_COLS)
    tok = jax.ShapeDtypeStruct((nb, seq, BRANCH_W), F32)
    tok_spec = pl.BlockSpec((nb, tt, BRANCH_W), lambda i, q: (0, i, 0))
    return pl.pallas_call(
        _prep_lat_kernel,
        grid=(seq // tt, LAT_NQ // LAT_Q_PER_STEP),
        in_specs=[
            pl.BlockSpec((nb, tt, 4 * BRANCH_W), lambda i, q: (0, i, Z_C // (4 * BRANCH_W))),
            pl.BlockSpec((nb, tt, GATE_LORA), lambda i, q: (0, i, Z_G // GATE_LORA)),
        ] + _prep_param_specs(p["layer"]),
        out_specs=[
            tok_spec, tok_spec,
            pl.BlockSpec((tt * V7X_SUBLANES, V7X_LANES), lambda i, q: (i, 0)),
            pl.BlockSpec((LAT_Q_PER_STEP, HEAD_DIM // V7X_SUBLANES, tt * V7X_SUBLANES, V7X_LANES),
                         lambda i, q: (q, 0, i, 0)),
        ],
        out_shape=[tok, tok,
                   jax.ShapeDtypeStruct((seq * V7X_SUBLANES, V7X_LANES), F32),
                   jax.ShapeDtypeStruct((LAT_NQ, HEAD_DIM // V7X_SUBLANES, seq * V7X_SUBLANES, V7X_LANES), F32)],
        scratch_shapes=[pltpu.VMEM((LAT_NQ, nb * BRANCH_W, V7X_LANES), F32),
                        pltpu.VMEM((nb * BRANCH_W, V7X_LANES), F32)],
        compiler_params=_cparams(2),
        name="rwkv_prep_lat",
    )(z3, z3, *_prep_params(p, ones))


SCAN_ACCS = 4


def _strided_sum(terms):
    acc = [None] * SCAN_ACCS
    for j, x in enumerate(terms):
        a = j % SCAN_ACCS
        acc[a] = x if acc[a] is None else acc[a] + x
    while len(acc) > 1:
        acc = [acc[j] + acc[j + 1] for j in range(0, len(acc), 2)]
    return acc[0]


def _scan_ctx_kernel(r_ref, kk_ref, v_ref, w_ref, kd_ref, b_ref, y_ref, sfin_ref, S, *, tb):
    d = pl.program_id(0)
    i = pl.program_id(1)
    n8 = V7X_SUBLANES
    n_v8 = HEAD_DIM // n8

    @pl.when(i == 0)
    def _():
        S[...] = jnp.zeros_like(S)

    def row8(ref, t, k):
        return jnp.broadcast_to(ref[t, pl.ds(k, 1), :], (n8, V7X_LANES))

    def step(s, carry):
        t = s + d * (tb - 1 - 2 * s)

        def pass1(k, sa):
            kkb = row8(kk_ref, t, k)
            return tuple(sa[vo] + S[k, vo * n8:(vo + 1) * n8, :] * kkb for vo in range(n_v8))

        zero = tuple(jnp.zeros((n8, V7X_LANES), F32) for _ in range(n_v8))
        sa = lax.fori_loop(0, HEAD_DIM, pass1, zero, unroll=32)
        nsa = tuple(-x for x in sa)
        vt = tuple(v_ref[t, vo * n8:(vo + 1) * n8, :] for vo in range(n_v8))

        def pass2(k, y):
            wb = row8(w_ref, t, k)
            bb = row8(b_ref, t, k)
            kb = row8(kd_ref, t, k)
            rb = row8(r_ref, t, k)
            out = []
            for vo in range(n_v8):
                sl = slice(vo * n8, (vo + 1) * n8)
                sn = S[k, sl, :] * wb + (nsa[vo] * bb + vt[vo] * kb)
                S[k, sl, :] = sn
                out.append(y[vo] + sn * rb)
            return tuple(out)

        y = lax.fori_loop(0, HEAD_DIM, pass2, zero, unroll=32)
        for vo in range(n_v8):
            y_ref[t, vo * n8:(vo + 1) * n8, :] = y[vo]
        return carry

    lax.fori_loop(0, tb, step, 0)

    @pl.when(i == pl.num_programs(1) - 1)
    def _():
        sfin_ref[...] = S[...]


def _scan_ctx(r_s, kk_s, v_s, w_s, kd_s, b_s):
    seq, tb = CTX.seq, SCAN_TB
    nt = seq // tb

    def tm(d, i):
        return i + d * (nt - 1 - 2 * i)
    shared = pl.BlockSpec((tb, HEAD_DIM, V7X_LANES), lambda d, i: (tm(d, i), 0, 0))
    per_dir = pl.BlockSpec((None, tb, HEAD_DIM, V7X_LANES), lambda d, i: (d, tm(d, i), 0, 0))
    state = pl.BlockSpec((None, HEAD_DIM, HEAD_DIM, V7X_LANES), lambda d, i: (d, 0, 0, 0))
    return pl.pallas_call(
        functools.partial(_scan_ctx_kernel, tb=tb),
        grid=(N_DIR, nt),
        in_specs=[shared, shared, shared, per_dir, per_dir, per_dir],
        out_specs=[per_dir, state],
        out_shape=[jax.ShapeDtypeStruct((N_DIR, seq, HEAD_DIM, V7X_LANES), F32),
                   jax.ShapeDtypeStruct((N_DIR, HEAD_DIM, HEAD_DIM, V7X_LANES), F32)],
        scratch_shapes=[pltpu.VMEM((HEAD_DIM, HEAD_DIM, V7X_LANES), F32)],
        compiler_params=_cparams(2),
        name="rwkv_scan_ctx",
    )(r_s, kk_s, v_s, w_s, kd_s, b_s)


def _lat_scan_dir(S, d, t, r_ref, kk_ref, v_ref, w_ref, kd_ref, b_ref, y_ref):
    n8 = V7X_SUBLANES

    def row8(ref, k):
        return jnp.broadcast_to(ref[k // n8, pl.ds(t * n8 + k % n8, 1), :], (n8, V7X_LANES))
    rows = pl.ds(pl.multiple_of(t * n8, n8), n8)
    nsa = -_strided_sum(S[d, k] * row8(kk_ref, k) for k in range(HEAD_DIM))
    vt = v_ref[rows, :]

    def update(k):
        sn = S[d, k] * row8(w_ref, k) + (nsa * row8(b_ref, k) + vt * row8(kd_ref, k))
        S[d, k] = sn
        return sn * row8(r_ref, k)
    y_ref[rows, :] = _strided_sum(update(k) for k in range(HEAD_DIM))


def _lat_scan_specs():
    tb = SCAN_TB
    nt = LAT.seq // tb
    n8 = V7X_SUBLANES

    def specs(d):
        tblk = (lambda i: i) if d == 0 else (lambda i: nt - 1 - i)
        plane = lambda q: pl.BlockSpec((None, HEAD_DIM // n8, tb * n8, V7X_LANES), lambda i: (q, 0, tblk(i), 0))
        rows = pl.BlockSpec((tb * n8, V7X_LANES), lambda i: (tblk(i), 0))
        return [plane(LAT_Q_R), plane(LAT_Q_KK), rows,
                plane(LAT_Q_W + d), plane(LAT_Q_KD + d), plane(LAT_Q_B + d)], rows
    return specs(0), specs(1)


def _scan_lat_kernel(*refs):
    fwd, bwd = refs[0:6], refs[6:12]
    s0_ref, yf_ref, yb_ref, S = refs[12:]

    @pl.when(pl.program_id(0) == 0)
    def _():
        S[...] = s0_ref[...]

    def step(s, carry):
        _lat_scan_dir(S, 0, s, *fwd, yf_ref)
        _lat_scan_dir(S, 1, SCAN_TB - 1 - s, *bwd, yb_ref)
        return carry
    lax.fori_loop(0, SCAN_TB, step, 0)


def _scan_lat(kq, v_s, s0):
    n8 = V7X_SUBLANES
    (in_f, y_f), (in_b, y_b) = _lat_scan_specs()
    y_shape = jax.ShapeDtypeStruct((LAT.seq * n8, V7X_LANES), F32)
    one = (kq, kq, v_s, kq, kq, kq)
    return pl.pallas_call(
        _scan_lat_kernel,
        grid=(LAT.seq // SCAN_TB,),
        in_specs=in_f + in_b + [_const_spec((N_DIR, HEAD_DIM, n8, V7X_LANES))],
        out_specs=[y_f, y_b],
        out_shape=[y_shape, y_shape],
        scratch_shapes=[pltpu.VMEM((N_DIR, HEAD_DIM, n8, V7X_LANES), F32)],
        compiler_params=_cparams(1),
        name="rwkv_scan_lat",
    )(*one, *one, s0)


def _softmax_pv(scores, vals, sink_col):
    m = sink_col
    for s in scores:
        m = jnp.maximum(m, jnp.max(s, axis=-1, keepdims=True))
    den = jnp.exp(sink_col - m)
    out = None
    for s, vx in zip(scores, vals):
        p = jnp.exp(s - m)
        den = den + jnp.sum(p, axis=-1, keepdims=True)
        o = _dot(p.astype(BF16), vx)
        out = o if out is None else out + o
    return out / den


def _sink_col(sink_ref, layer, kv, n_rows):
    row = lax.broadcasted_iota(jnp.int32, (Q_PER_KV * n_rows, 1), 0)
    col = jnp.full((Q_PER_KV * n_rows, 1), sink_ref[layer, kv * Q_PER_KV], F32)
    for g in range(1, Q_PER_KV):
        col = jnp.where(row >= g * n_rows, sink_ref[layer, kv * Q_PER_KV + g], col)
    return col


def _qk(q2, kh):
    return lax.dot_general(q2, kh, (((1,), (1,)), ((), ())), preferred_element_type=F32) * ATT_SCALE


CTX_ATTN_SEQS = 4


def _ctx_attn_kernel(sink_ref, z_ref, o_ref, *, layer):
    hd = HEAD_DIM
    for s in range(CTX_ATTN_SEQS):
        rows = slice(s * SEQ, (s + 1) * SEQ)
        q = z_ref[rows, 0:ATT_HEADS * hd]
        outs = []
        for kv in range(KV_HEADS):
            kh = z_ref[rows, ATT_HEADS * hd + kv * hd:ATT_HEADS * hd + (kv + 1) * hd].astype(BF16)
            voff = (ATT_HEADS + KV_HEADS) * hd + kv * hd
            vh = z_ref[rows, voff:voff + hd].astype(BF16)
            q2 = jnp.concatenate(
                [q[:, (kv * Q_PER_KV + g) * hd:(kv * Q_PER_KV + g + 1) * hd] for g in range(Q_PER_KV)],
                axis=0).astype(BF16)
            o2 = _softmax_pv([_qk(q2, kh)], [vh], _sink_col(sink_ref, layer, kv, SEQ))
            outs += [o2[g * SEQ:(g + 1) * SEQ] for g in range(Q_PER_KV)]
        o_ref[rows, :] = jnp.concatenate(outs, axis=1).astype(BF16)


def _rope(x, cos, sin_signed):
    lane = lax.broadcasted_iota(jnp.int32, x.shape, 1)
    first = (lane % (HEAD_DIM // 2)) < (HEAD_DIM // 4)
    partner = jnp.where(first, pltpu.roll(x, V7X_LANES - HEAD_DIM // 4, 1), pltpu.roll(x, HEAD_DIM // 4, 1))
    return x * cos + partner * sin_signed


def _lat_attn_kernel(sink_ref, z_ref, ck_ref, cv_ref, cos_ref, sin_ref, o_ref, q_s, k_s, *, layer):
    hd = HEAD_DIM
    cos = cos_ref[...]
    sin = sin_ref[...]
    for j in range(ATT_HEADS * hd // V7X_LANES):
        q_s[:, j * V7X_LANES:(j + 1) * V7X_LANES] = _rope(
            z_ref[:, j * V7X_LANES:(j + 1) * V7X_LANES], cos, sin).astype(BF16)
    k_s[...] = _rope(z_ref[:, ATT_HEADS * hd:(ATT_HEADS + KV_HEADS) * hd], cos, sin).astype(BF16)
    voff = (ATT_HEADS + KV_HEADS) * hd
    nb = DEC_SEQ // BLOCK
    for n in range(nb):
        lo = max(n - 1, 0) * BLOCK
        hi = min(n + 2, nb) * BLOCK
        i_abs = n * BLOCK + lax.broadcasted_iota(jnp.int32, (Q_PER_KV * BLOCK, hi - lo), 0) % BLOCK
        j_abs = lo + lax.broadcasted_iota(jnp.int32, (Q_PER_KV * BLOCK, hi - lo), 1)
        band = jnp.abs(i_abs - j_abs) <= WINDOW
        outs = []
        for kv in range(KV_HEADS):
            q2 = jnp.concatenate(
                [q_s[n * BLOCK:(n + 1) * BLOCK, (kv * Q_PER_KV + g) * hd:(kv * Q_PER_KV + g + 1) * hd]
                 for g in range(Q_PER_KV)], axis=0)
            kw = k_s[lo:hi, kv * hd:(kv + 1) * hd]
            vw = z_ref[lo:hi, voff + kv * hd:voff + (kv + 1) * hd].astype(BF16)
            kc = ck_ref[:, kv * hd:(kv + 1) * hd].astype(BF16)
            vc = cv_ref[:, kv * hd:(kv + 1) * hd].astype(BF16)
            s_lat = jnp.where(band, _qk(q2, kw), -1e30)
            o2 = _softmax_pv([s_lat, _qk(q2, kc)], [vw, vc], _sink_col(sink_ref, layer, kv, BLOCK))
            outs += [o2[g * BLOCK:(g + 1) * BLOCK] for g in range(Q_PER_KV)]
        o_ref[n * BLOCK:(n + 1) * BLOCK, :] = jnp.concatenate(outs, axis=1).astype(BF16)


def _ctx_attn(z, sink, layer):
    wd = (ATT_HEADS + 2 * KV_HEADS) * HEAD_DIM
    return pl.pallas_call(
        functools.partial(_ctx_attn_kernel, layer=layer),
        grid=(BATCH // CTX_ATTN_SEQS,),
        in_specs=[
            pl.BlockSpec(memory_space=pltpu.SMEM),
            pl.BlockSpec((CTX_ATTN_SEQS * SEQ, wd), lambda i: (i, Z_D // wd)),
        ],
        out_specs=pl.BlockSpec((CTX_ATTN_SEQS * SEQ, BRANCH_W), lambda i: (i, 0)),
        out_shape=jax.ShapeDtypeStruct((CTX.n_tok, BRANCH_W), BF16),
        compiler_params=_cparams(1),
        name="ctx_attn",
    )(sink, z)


def _lat_attn(z, sink, layer, ck, cv, cos_t, sin_t):
    wd = (ATT_HEADS + 2 * KV_HEADS) * HEAD_DIM
    kvw = KV_HEADS * HEAD_DIM
    return pl.pallas_call(
        functools.partial(_lat_attn_kernel, layer=layer),
        grid=(DEC_BATCH,),
        in_specs=[
            pl.BlockSpec(memory_space=pltpu.SMEM),
            pl.BlockSpec((DEC_SEQ, wd), lambda i: (i, Z_D // wd)),
            pl.BlockSpec((None, None, PAST_LEN, kvw), lambda i: (i, layer, 0, 0)),
            pl.BlockSpec((None, None, PAST_LEN, kvw), lambda i: (i, layer, 0, 0)),
            _const_spec((DEC_SEQ, V7X_LANES)),
            _const_spec((DEC_SEQ, V7X_LANES)),
        ],
        out_specs=pl.BlockSpec((DEC_SEQ, BRANCH_W), lambda i: (i, 0)),
        out_shape=jax.ShapeDtypeStruct((LAT.n_tok, BRANCH_W), BF16),
        scratch_shapes=[pltpu.VMEM((DEC_SEQ, ATT_HEADS * HEAD_DIM), BF16),
                        pltpu.VMEM((DEC_SEQ, kvw), BF16)],
        compiler_params=_cparams(1),
        name="lat_attn",
    )(sink, z, ck, cv, cos_t, sin_t)


def _scan_out_ctx(yf_ref, yb_ref, slab):
    nb, tt = CTX.nb, CTX.tt
    cols = []
    for c in range(tt // RWKV_HEADS):
        slabs = _exchange_lane_groups([yf_ref[RWKV_HEADS * c + j] + yb_ref[RWKV_HEADS * c + j]
                                       for j in range(RWKV_HEADS)])
        cols.append(jnp.concatenate(slabs, axis=0))
    by_t = jnp.concatenate(cols, axis=1).T
    n_slab = BRANCH_W // V7X_LANES
    for t in range(tt):
        for s in range(n_slab):
            slab[s, pl.ds(t, nb, stride=tt), :] = by_t[t * nb:(t + 1) * nb, s * V7X_LANES:(s + 1) * V7X_LANES]
    return jnp.concatenate([slab[s] for s in range(n_slab)], axis=1)


def _scan_out_lat(yf_ref, yb_ref, ysum, gy):
    nb, tt = LAT.nb, LAT.tt
    chains = nb * RWKV_HEADS
    ysum[...] = yf_ref[...] + yb_ref[...]
    for vp in range(V7X_SUBLANES):
        a = ysum[pl.ds(vp, tt, stride=V7X_SUBLANES), :].T
        for vl in range(LAT_VL):
            gy[pl.ds(vp + V7X_SUBLANES * vl, chains, stride=HEAD_DIM), :] = a[vl * chains:(vl + 1) * chains, :]
    return jnp.concatenate([gy[b * BRANCH_W:(b + 1) * BRANCH_W, :].T for b in range(nb)], axis=0)


def _mix_kernel(x_ref, mod_ref, g1_ref, oa_ref, ob_ref, od_ref, yf_ref, yb_ref, g_ref, bonus_ref,
                lnxg_ref, lnxb_ref, ones_ref, wg_ref, bg_ref, wb_ref, wo_ref, x1_ref, *scratch, path):
    nb, tt = path.nb, path.tt
    rows = nb * tt
    r0, nr = path.mod_row0, path.mod_rows

    def mod3(j):
        return mod_ref[r0:r0 + nr, j * D_MODEL:(j + 1) * D_MODEL][:, None, :]

    x3 = x_ref[...]
    h3 = _rms(x3, g1_ref[...]) * (1.0 + mod3(1)) + mod3(0)
    hb = h3.reshape(rows, D_MODEL).astype(BF16)
    y = (_scan_out_ctx if path is CTX else _scan_out_lat)(yf_ref, yb_ref, *scratch)
    ones = ones_ref[...]
    mu = _head_sum(y, ones) * (1.0 / HEAD_DIM)
    dl = y - mu
    var = _head_sum(dl * dl, ones) * (1.0 / HEAD_DIM)
    yn = dl * lax.rsqrt(var + GN_EPS) * lnxg_ref[...] + lnxb_ref[...]
    bonus = bonus_ref[...].reshape(rows, BRANCH_W)
    gate_c = g_ref[...].reshape(rows, BRANCH_W)
    oc = ((yn + bonus) * gate_c).astype(BF16)
    branches = (oa_ref[...].reshape(rows, BRANCH_W), ob_ref[...].reshape(rows, BRANCH_W), oc,
                od_ref[...].reshape(rows, BRANCH_W))
    mixed = None
    for n, br in enumerate(branches):
        cols = slice(n * D_MODEL, (n + 1) * D_MODEL)
        gate = jax.nn.sigmoid(_dot(hb, wg_ref[:, cols]) + bg_ref[:, cols])
        term = gate * _dot(br, wb_ref[n])
        mixed = term if mixed is None else mixed + term
    mix = _dot(mixed.astype(BF16), wo_ref[...])
    x1_ref[...] = x3 + mod3(2) * mix.reshape(nb, tt, D_MODEL)


def _mix(x, mod, p, oa, ob, od, yf, yb, g, bonus, ones, path):
    nb, tt, seq = path.nb, path.tt, path.seq
    l = p["layer"]
    tokw = pl.BlockSpec((nb, tt, BRANCH_W), lambda i: (0, i, 0))
    tokd = pl.BlockSpec((nb, tt, D_MODEL), lambda i: (0, i, 0))
    n_slab = BRANCH_W // V7X_LANES
    if path is CTX:
        y_specs = [pl.BlockSpec((None, tt, HEAD_DIM, V7X_LANES), lambda i, d=d: (d, i, 0, 0)) for d in range(N_DIR)]
        scratch = [pltpu.VMEM((n_slab, nb * tt, V7X_LANES), F32)]
    else:
        y_specs = [pl.BlockSpec((tt * V7X_SUBLANES, V7X_LANES), lambda i: (i, 0))] * N_DIR
        scratch = [pltpu.VMEM((tt * V7X_SUBLANES, V7X_LANES), F32),
                   pltpu.VMEM((nb * BRANCH_W, V7X_LANES), F32)]
    as3 = lambda a: a.reshape(nb, seq, a.shape[-1])
    out = pl.pallas_call(
        functools.partial(_mix_kernel, path=path),
        grid=(seq // tt,),
        in_specs=[
            tokd,
            _layer_spec((MOD_ROWS, N_MOD * D_MODEL), l),
            _layer_spec((1, D_MODEL), l),
            tokw, tokw, tokw, *y_specs, tokw, tokw,
            _layer_spec((1, BRANCH_W), l),
            _layer_spec((1, BRANCH_W), l),
            _const_spec((BRANCH_W, BRANCH_W)),
            _layer_spec((D_MODEL, N_BRANCH * D_MODEL), l),
            _layer_spec((1, N_BRANCH * D_MODEL), l),
            _layer_spec((N_BRANCH, BRANCH_W, D_MODEL), l),
            _layer_spec((D_MODEL, D_MODEL), l),
        ],
        out_specs=tokd,
        out_shape=jax.ShapeDtypeStruct((nb, seq, D_MODEL), F32),
        scratch_shapes=scratch,
        compiler_params=_cparams(1),
        name=f"branch_mix_{seq}",
    )(as3(x), mod, p["norm1_g"], as3(oa), as3(ob), as3(od), yf, yb, g, bonus, p["rwkv_lnx_g"], p["rwkv_lnx_b"],
      ones, p["w_gate"], p["b_gate"], p["w_branch"], p["w_out"])
    return out.reshape(path.n_tok, D_MODEL)


ROUTER_LANES = V7X_LANES


def _route(logits):
    lane = lax.broadcasted_iota(jnp.int32, logits.shape, 1).astype(F32)
    ninf = -jnp.inf
    big = float(ROUTER_LANES)
    gmask = lane < N_GROUPS
    gl = jnp.where(gmask, logits, ninf)
    gmax = jnp.max(gl, axis=-1, keepdims=True)
    gidx = jnp.min(jnp.where(gl == gmax, lane, big), axis=-1, keepdims=True)
    g_w = 1.0 / jnp.sum(jnp.where(gmask, jnp.exp(gl - gmax), 0.0), axis=-1, keepdims=True)
    egroup = jnp.floor((lane - N_GROUPS) * (1.0 / EXPERTS_PER_GROUP))
    emask = (lane >= N_GROUPS) & (lane < N_GROUPS + N_EXPERTS) & (egroup == gidx)
    el = jnp.where(emask, logits, ninf)
    e1 = jnp.max(el, axis=-1, keepdims=True)
    i1 = jnp.min(jnp.where(emask & (el == e1), lane, big), axis=-1, keepdims=True)
    el2 = jnp.where(lane == i1, ninf, el)
    e2 = jnp.max(el2, axis=-1, keepdims=True)
    i2 = jnp.min(jnp.where(emask & (lane != i1) & (el2 == e2), lane, big), axis=-1, keepdims=True)
    t = jnp.exp(e2 - e1)
    den = 1.0 + t
    return jnp.where(lane == i1, g_w * (1.0 / den), 0.0) + jnp.where(lane == i2, g_w * (t / den), 0.0)


def _moe_input(x1, mod_ref, row, g2_ref, wrh_ref, wrl_ref, br_ref):
    h2 = _rms(x1, g2_ref[...]) * (1.0 + _mod_slice(mod_ref, row, 4)) + _mod_slice(mod_ref, row, 3)
    return h2.astype(BF16), _route(_dot_split(h2, wrh_ref[...], wrl_ref[...]) + br_ref[...])


def _expert(hb, weg, weu, wed, c):
    hg = _dot(hb, weg)
    hu = _dot(hb, weu)
    return _dot((hg * jax.nn.sigmoid(hg) * hu * c).astype(BF16), wed)


def _moe_finish(x1, acc, mod_ref, row, fg_ref, out_refs, final):
    x2 = x1 + _mod_slice(mod_ref, row, 5) * acc
    out_refs[0][...] = x2
    if final:
        out_refs[1][...] = _rms(x2, fg_ref[...])


def _moe_kernel(x1_ref, mod_ref, g2_ref, wrh_ref, wrl_ref, br_ref, weg_ref, weu_ref, wed_ref, fg_ref,
                *out_refs, path, final):
    row = _mod_row(pl.program_id(0), path)
    x1 = x1_ref[...]
    hb, comb = _moe_input(x1, mod_ref, row, g2_ref, wrh_ref, wrl_ref, br_ref)
    acc = None
    for e in range(N_EXPERTS):
        lane = N_GROUPS + e
        term = _expert(hb, weg_ref[e], weu_ref[e], wed_ref[e], comb[:, lane:lane + 1])
        acc = term if acc is None else acc + term
    _moe_finish(x1, acc, mod_ref, row, fg_ref, out_refs, final)


def _moe_in_specs(tm, l):
    single = pl.Buffered(1)
    up = _layer_spec((N_EXPERTS, D_MODEL, EXPERT_FF), l, pipeline_mode=single)
    return [
        pl.BlockSpec((tm, D_MODEL), lambda i: (i, 0)),
        _layer_spec((MOD_ROWS, N_MOD * D_MODEL), l),
        _layer_spec((1, D_MODEL), l),
        _layer_spec((D_MODEL, ROUTER_LANES), l),
        _layer_spec((D_MODEL, ROUTER_LANES), l),
        _layer_spec((1, ROUTER_LANES), l),
        up, up,
        _layer_spec((N_EXPERTS, EXPERT_FF, D_MODEL), l, pipeline_mode=single),
        _const_spec((1, D_MODEL)),
    ]


def _moe_args(x1, mod, p, final_g):
    return (x1, mod, p["norm2_g"], *p["w_router"], p["b_router"], p["w_e_gate"], p["w_e_up"], p["w_e_down"], final_g)


def _moe(x1, mod, p, final_g, final, path):
    tm = TOK_TILE
    tokd = pl.BlockSpec((tm, D_MODEL), lambda i: (i, 0))
    out = jax.ShapeDtypeStruct((path.n_tok, D_MODEL), F32)
    return pl.pallas_call(
        functools.partial(_moe_kernel, path=path, final=final),
        grid=(path.n_tok // tm,),
        in_specs=_moe_in_specs(tm, p["layer"]),
        out_specs=[tokd, tokd] if final else [tokd],
        out_shape=[out, out] if final else [out],
        compiler_params=_cparams(1),
        name=f"moe_{path.seq}" + ("_final" if final else ""),
    )(*_moe_args(x1, mod, p, final_g))


def _block_diag2(w):
    z = jnp.zeros_like(w[:, 0])
    return jnp.concatenate([jnp.concatenate([w[:, 0], z], axis=2), jnp.concatenate([z, w[:, 1]], axis=2)], axis=1)


def _all_layer_params(a):
    row = lambda v: v.reshape(DEPTH, 1, -1)
    hi_lo = lambda w: jnp.stack(_split(w), axis=1)
    w_in = a["w_in"]
    w_router = jnp.zeros((DEPTH, D_MODEL, ROUTER_LANES), F32)
    w_router = w_router.at[:, :, 0:N_GROUPS].set(a["w_rg"]).at[:, :, N_GROUPS:N_GROUPS + N_EXPERTS].set(a["w_re"])
    b_router = jnp.zeros((DEPTH, 1, ROUTER_LANES), F32)
    b_router = b_router.at[:, 0, 0:N_GROUPS].set(a["b_rg"]).at[:, 0, N_GROUPS:N_GROUPS + N_EXPERTS].set(a["b_re"])
    return {
        "norm1_g": row(a["norm1_g"]), "norm2_g": row(a["norm2_g"]),
        "w_in": jnp.concatenate([w_in[:, :, 0:2048], w_in[:, :, 2176:2688], w_in[:, :, 2048:2176]], axis=2).astype(BF16),
        "gmlp_ln_g": row(a["gmlp_ln_g"]),
        "gmlp_wcat": a["gmlp_ws"].transpose(0, 2, 1, 3).reshape(DEPTH, CHUNK, GMLP_GROUPS * CHUNK).astype(BF16),
        "gmlp_bsx": jnp.repeat(a["gmlp_bs"].transpose(0, 2, 1), BRANCH_W // GMLP_GROUPS, axis=2),
        "conv_w": a["conv_w"], "conv_b": row(a["conv_b"]),
        "conv_ln_g": row(a["conv_ln_g"]), "conv_ln_b": row(a["conv_ln_b"]),
        "rwkv_w0": row(a["rwkv_w0"]), "rwkv_w2blk": hi_lo(_block_diag2(a["rwkv_w2"])),
        "rwkv_a0": row(a["rwkv_a0"]), "rwkv_a2blk": hi_lo(_block_diag2(a["rwkv_a2"])),
        "rwkv_g2": hi_lo(a["rwkv_g2"]), "rwkv_kk": row(a["rwkv_kk"]), "rwkv_ka": row(a["rwkv_ka"]),
        "rwkv_rk": row(a["rwkv_rk"]),
        "rwkv_lnx_g": row(a["rwkv_lnx_g"]), "rwkv_lnx_b": row(a["rwkv_lnx_b"]),
        "attn_sink": a["attn_sink"],
        "w_gate": a["w_gate"].astype(BF16), "b_gate": row(a["b_gate"]),
        "w_branch": a["w_branch"].astype(BF16), "w_out": a["w_out"].astype(BF16),
        "w_router": _split(w_router), "b_router": b_router,
        "w_e_gate": a["w_e_gate"].astype(BF16), "w_e_up": a["w_e_up"].astype(BF16),
        "w_e_down": a["w_e_down"].astype(BF16),
    }


def _rope_tables():
    half = HEAD_DIM // 4
    inv = ROPE_BASE ** (-jnp.arange(half, dtype=F32) / half)
    t = jnp.arange(DEC_SEQ)
    row = (t // GRID_W).astype(F32)[:, None] * inv[None, :]
    col = (t % GRID_W).astype(F32)[:, None] * inv[None, :]
    cos_h = jnp.concatenate([jnp.cos(row), jnp.cos(row), jnp.cos(col), jnp.cos(col)], axis=1)
    sin_h = jnp.concatenate([-jnp.sin(row), jnp.sin(row), -jnp.sin(col), jnp.sin(col)], axis=1)
    reps = V7X_LANES // HEAD_DIM
    return jnp.tile(cos_h, (1, reps)), jnp.tile(sin_h, (1, reps))


def _mixers(x, mod, p, path, attn):
    z, *kv = _inproj(x, mod, p, path)
    oa, ob = _local_mix(z, p, path)
    return z, kv, oa, ob, attn(z)


def kernel(x_prompt, x_sample, cache_k, cache_v, state_rwkv, c, c_ctx, norm1_g, norm2_g, final_norm_g, w_mod, b_mod, w_in, gmlp_ln_g, gmlp_ws, gmlp_bs, conv_w, conv_b, conv_ln_g, conv_ln_b, rwkv_w0, rwkv_w2, rwkv_a0, rwkv_a2, rwkv_g2, rwkv_kk, rwkv_ka, rwkv_rk, rwkv_lnx_g, rwkv_lnx_b, attn_sink, w_gate, b_gate, w_branch, w_out, w_rg, b_rg, w_re, b_re, w_e_gate, w_e_up, w_e_down):
    arrays = dict(norm1_g=norm1_g, norm2_g=norm2_g, w_in=w_in, gmlp_ln_g=gmlp_ln_g, gmlp_ws=gmlp_ws,
                  gmlp_bs=gmlp_bs, conv_w=conv_w, conv_b=conv_b, conv_ln_g=conv_ln_g, conv_ln_b=conv_ln_b,
                  rwkv_w0=rwkv_w0, rwkv_w2=rwkv_w2, rwkv_a0=rwkv_a0, rwkv_a2=rwkv_a2, rwkv_g2=rwkv_g2,
                  rwkv_kk=rwkv_kk, rwkv_ka=rwkv_ka, rwkv_rk=rwkv_rk, rwkv_lnx_g=rwkv_lnx_g,
                  rwkv_lnx_b=rwkv_lnx_b, attn_sink=attn_sink, w_gate=w_gate, b_gate=b_gate,
                  w_branch=w_branch, w_out=w_out, w_rg=w_rg, b_rg=b_rg, w_re=w_re, b_re=b_re,
                  w_e_gate=w_e_gate, w_e_up=w_e_up, w_e_down=w_e_down)
    xc = x_prompt.reshape(CTX.n_tok, D_MODEL)
    xl = x_sample.reshape(LAT.n_tok, D_MODEL)
    cvec = jnp.concatenate([c_ctx[None], c, jnp.zeros((MOD_ROWS - 1 - DEC_BATCH, D_MODEL), F32)], axis=0)
    mod_all = _modulation(cvec, w_mod, b_mod)
    ones = _head_ones()
    cos_t, sin_t = _rope_tables()
    final_g = final_norm_g.reshape(1, D_MODEL)
    kvw = KV_HEADS * HEAD_DIM
    ks_out, vs_out, ss_out = [], [], []
    yc = yl = None
    params = _all_layer_params(arrays)
    ck = cache_k.reshape(DEC_BATCH, DEPTH, PAST_LEN, kvw)
    cv = cache_v.reshape(DEC_BATCH, DEPTH, PAST_LEN, kvw)
    mod = mod_all
    for l in range(DEPTH):
        p = dict(params, layer=l)
        final = l == DEPTH - 1
        zc, kv_c, oa, ob, od = _mixers(xc, mod, p, CTX, lambda z: _ctx_attn(z, p["attn_sink"], l))
        g, bonus, r_s, kk_s, v_s, w_s, kd_s, b_s = _prep_ctx(zc, p, ones)
        y_c, s_fin = _scan_ctx(r_s, kk_s, v_s, w_s, kd_s, b_s)
        x1c = _mix(xc, mod, p, oa, ob, od, y_c, y_c, g, bonus, ones, CTX)
        zl, _, oa, ob, od = _mixers(xl, mod, p, LAT, lambda z: _lat_attn(z, p["attn_sink"], l, ck, cv, cos_t, sin_t))
        g, bonus, v_l, kq = _prep_lat(zl, p, ones)
        s0 = state_rwkv[:, l].reshape(DEC_BATCH, N_DIR, RWKV_HEADS, LAT_VL, V7X_SUBLANES, HEAD_DIM)
        s0 = s0.transpose(1, 5, 4, 3, 0, 2).reshape(N_DIR, HEAD_DIM, V7X_SUBLANES, V7X_LANES)
        yf, yb = _scan_lat(kq, v_l, s0)
        outs_c = _moe(x1c, mod, p, final_g, final, CTX)
        x1l = _mix(xl, mod, p, oa, ob, od, yf, yb, g, bonus, ones, LAT)
        outs_l = _moe(x1l, mod, p, final_g, final, LAT)
        xc, xl = outs_c[0], outs_l[0]
        if final:
            yc, yl = outs_c[1], outs_l[1]
        ks_out.append(kv_c[0].reshape(BATCH, SEQ, KV_HEADS, HEAD_DIM))
        vs_out.append(kv_c[1].reshape(BATCH, SEQ, KV_HEADS, HEAD_DIM))
        ss_out.append(s_fin.reshape(N_DIR, HEAD_DIM, HEAD_DIM, RWKV_HEADS, BATCH).transpose(4, 0, 3, 2, 1))
    y_prompt = yc.reshape(BATCH, SEQ, D_MODEL)
    y_sample = yl.reshape(DEC_BATCH, DEC_SEQ, D_MODEL)
    return (y_prompt, y_sample, jnp.stack(ks_out, axis=1), jnp.stack(vs_out, axis=1), jnp.stack(ss_out, axis=1))
```
